```python
import jax, jax.numpy as jnp
from jax import lax
import numpy as np

D_MODEL = 2048
BATCH = 32
SEQ = 256
DEPTH = 4
DEC_BATCH = 4
DEC_SEQ = 1024
PAST_LEN = 256

GRID_W = 64
N_MIXERS = 3
N_ATTN = (DEPTH + 2) // 3
N_MLSTM = (DEPTH + 1) // 3
N_RWKV = DEPTH // 3
N_MOD = 9
D_FF = 5632
EPS = 1e-6

ATTN_HEADS = 16
ATTN_KV_HEADS = 4
ATTN_GROUP = ATTN_HEADS // ATTN_KV_HEADS
ATTN_HEAD_DIM = D_MODEL // ATTN_HEADS
ATTN_WINDOW = 128
ATTN_BLOCK = 128
ROPE_THETA = 10000.0
QKV_DIM = (ATTN_HEADS + 2 * ATTN_KV_HEADS) * ATTN_HEAD_DIM

MLSTM_HEADS = 8
MLSTM_DV = D_MODEL // MLSTM_HEADS
MLSTM_DK = MLSTM_DV // 2
MLSTM_CHUNK = 64
MLSTM_IN_DIM = 2 * MLSTM_HEADS * MLSTM_DK + 2 * D_MODEL

RWKV_HEAD = 64
RWKV_HEADS = D_MODEL // RWKV_HEAD
RWKV_LORA_W = 96
RWKV_LORA_A = 96
RWKV_LORA_G = 256
RWKV_GN_EPS = 64e-5

kernel_name = 'hybrid_dit_attn_mlstm_rwkv7_step'


def rmsnorm(x, g, eps=EPS):
    xf = x.astype(jnp.float32)
    y = xf * lax.rsqrt(jnp.mean(xf * xf, axis=-1, keepdims=True) + eps)
    return (y * g.astype(jnp.float32)).astype(x.dtype)


def modulate(x, g, shift, scale):
    return rmsnorm(x, g) * (1.0 + scale) + shift


def swiglu(h, w_in, w_out):
    gate, up = jnp.split(h @ w_in, 2, axis=-1)
    return (jax.nn.silu(gate) * up) @ w_out


def adaln(cond, mod_w, mod_b):
    m = jax.nn.silu(cond) @ mod_w + mod_b
    return m.reshape(cond.shape[0], N_MOD, D_MODEL)


def macaron_layer(x, mod, norm_g, ffn_w_in, ffn_w_out, mixer):
    m = lambda j: mod[:, None, j, :]
    x = x + 0.5 * m(2) * swiglu(modulate(x, norm_g[0], m(0), m(1)), ffn_w_in[0], ffn_w_out[0])
    y, st = mixer(modulate(x, norm_g[1], m(3), m(4)))
    x = x + m(5) * y
    x = x + 0.5 * m(8) * swiglu(modulate(x, norm_g[2], m(6), m(7)), ffn_w_in[1], ffn_w_out[1])
    return x, st


def _rotate(x, ang):
    f = ang.shape[-1]
    cos = jnp.cos(ang)[:, None, :].astype(x.dtype)
    sin = jnp.sin(ang)[:, None, :].astype(x.dtype)
    xa, xb = x[..., :f], x[..., f:]
    return jnp.concatenate([xa * cos - xb * sin, xb * cos + xa * sin], axis=-1)


def rope_2d(x):
    n_tok = x.shape[1]
    rows = n_tok // GRID_W
    row = jnp.repeat(jnp.arange(rows, dtype=jnp.float32), GRID_W)
    col = jnp.tile(jnp.arange(GRID_W, dtype=jnp.float32), rows)
    n_freq = ATTN_HEAD_DIM // 4
    inv_freq = ROPE_THETA ** (-jnp.arange(n_freq, dtype=jnp.float32) / n_freq)
    half = ATTN_HEAD_DIM // 2
    return jnp.concatenate([_rotate(x[..., :half], row[:, None] * inv_freq),
                            _rotate(x[..., half:], col[:, None] * inv_freq)], axis=-1)


def attn_qkv(h, w_qkv, q_norm, k_norm):
    b, t, _ = h.shape
    q, k, v = jnp.split(h @ w_qkv, [ATTN_HEADS * ATTN_HEAD_DIM, (ATTN_HEADS + ATTN_KV_HEADS) * ATTN_HEAD_DIM], axis=-1)
    q = rmsnorm(q.reshape(b, t, ATTN_HEADS, ATTN_HEAD_DIM), q_norm)
    k = rmsnorm(k.reshape(b, t, ATTN_KV_HEADS, ATTN_HEAD_DIM), k_norm)
    v = v.reshape(b, t, ATTN_KV_HEADS, ATTN_HEAD_DIM)
    return q, k, v


def sink_attend(q, k, v, valid, sink):
    s = jnp.einsum('bqhgd,bshd->bhgqs', q.astype(jnp.float32), k.astype(jnp.float32)) * ATTN_HEAD_DIM ** -0.5
    if valid is not None:
        s = jnp.where(valid, s, -jnp.inf)
    sk = jnp.broadcast_to(sink.astype(jnp.float32).reshape(1, ATTN_KV_HEADS, ATTN_GROUP, 1, 1), s.shape[:-1] + (1,))
    p = jax.nn.softmax(jnp.concatenate([s, sk], axis=-1), axis=-1)[..., :-1]
    return jnp.einsum('bhgqs,bshd->bqhgd', p.astype(v.dtype), v)


def attn_context(h, w_qkv, q_norm, k_norm, sink, w_o):
    b, t, _ = h.shape
    q, k, v = attn_qkv(h, w_qkv, q_norm, k_norm)
    nb = t // ATTN_BLOCK
    qb = q.reshape(b, nb, ATTN_BLOCK, ATTN_KV_HEADS, ATTN_GROUP, ATTN_HEAD_DIM).swapaxes(0, 1)
    o = lax.map(lambda qblk: sink_attend(qblk, k, v, None, sink), qb)
    o = o.swapaxes(0, 1).reshape(b, t, ATTN_HEADS * ATTN_HEAD_DIM)
    return o @ w_o, (k, v)


def attn_latent(h, ctx_k, ctx_v, w_qkv, q_norm, k_norm, sink, w_o):
    b, t, _ = h.shape
    q, k, v = attn_qkv(h, w_qkv, q_norm, k_norm)
    q, k = rope_2d(q), rope_2d(k)
    span = ATTN_BLOCK + 2 * ATTN_WINDOW
    pad = ((0, 0), (ATTN_WINDOW, ATTN_WINDOW), (0, 0), (0, 0))
    k_pad, v_pad = jnp.pad(k, pad), jnp.pad(v, pad)
    ctx_k = ctx_k.astype(k.dtype)
    ctx_v = ctx_v.astype(v.dtype)
    n_ctx = ctx_k.shape[1]
    q_idx = jnp.arange(ATTN_BLOCK)[:, None]
    k_idx = jnp.arange(span)[None, :]
    in_window = jnp.abs(k_idx - ATTN_WINDOW - q_idx) <= ATTN_WINDOW
    ctx_valid = jnp.ones((ATTN_BLOCK, n_ctx), dtype=bool)

    def block(args):
        qblk, blk = args
        start = blk * ATTN_BLOCK
        kb = lax.dynamic_slice_in_dim(k_pad, start, span, axis=1)
        vb = lax.dynamic_slice_in_dim(v_pad, start, span, axis=1)
        key_pos = start - ATTN_WINDOW + k_idx
        valid = in_window & (key_pos >= 0) & (key_pos < t)
        valid = jnp.concatenate([valid, ctx_valid], axis=-1)
        return sink_attend(qblk, jnp.concatenate([kb, ctx_k], axis=1), jnp.concatenate([vb, ctx_v], axis=1), valid, sink)

    nb = t // ATTN_BLOCK
    qb = q.reshape(b, nb, ATTN_BLOCK, ATTN_KV_HEADS, ATTN_GROUP, ATTN_HEAD_DIM).swapaxes(0, 1)
    o = lax.map(block, (qb, jnp.arange(nb)))
    o = o.swapaxes(0, 1).reshape(b, t, ATTN_HEADS * ATTN_HEAD_DIM)
    return o @ w_o, None


def mlstm_chunkwise(q, k, v, i_pre, f_pre, c0, n0, m0):
    b, nh, t, _ = q.shape
    L = MLSTM_CHUNK
    nc = t // L

    def chunks(x):
        return jnp.moveaxis(x.reshape(x.shape[:2] + (nc, L) + x.shape[3:]), 2, 0)

    causal = jnp.tril(jnp.ones((L, L), dtype=bool))

    def step(carry, inp):
        C, n, m = carry
        qc, kc, vc, ic, lfc = inp
        bcum = jnp.cumsum(lfc, axis=-1)
        d = jnp.where(causal, bcum[..., :, None] - bcum[..., None, :] + ic[..., None, :], -jnp.inf)
        inter = bcum + m[..., None]
        m_c = jnp.maximum(inter, jnp.max(d, axis=-1))
        s = jnp.einsum('bhtd,bhsd->bhts', qc, kc) * jnp.exp(d - m_c[..., None])
        a = jnp.exp(inter - m_c)
        num = jnp.einsum('bhts,bhsv->bhtv', s, vc) + a[..., None] * jnp.einsum('bhtd,bhdv->bhtv', qc, C)
        den = jnp.sum(s, axis=-1) + a * jnp.einsum('bhtd,bhd->bht', qc, n)
        hc = num / jnp.maximum(jnp.abs(den), jnp.exp(-m_c))[..., None]
        bl = bcum[..., -1]
        g = bl[..., None] - bcum + ic
        m_new = jnp.maximum(bl + m, jnp.max(g, axis=-1))
        decay = jnp.exp(bl + m - m_new)
        wg = jnp.exp(g - m_new[..., None])
        C_new = decay[..., None, None] * C + jnp.einsum('bhs,bhsd,bhsv->bhdv', wg, kc, vc)
        n_new = decay[..., None] * n + jnp.einsum('bhs,bhsd->bhd', wg, kc)
        return (C_new, n_new, m_new), hc

    lf = jax.nn.log_sigmoid(f_pre)
    (C, n, m), h = lax.scan(step, (c0, n0, m0), (chunks(q), chunks(k), chunks(v), chunks(i_pre), chunks(lf)))
    h = jnp.moveaxis(h, 0, 2).reshape(b, nh, t, -1)
    return h, C, n, m


def mlstm_mixer(h, state0, w_in, w_gate, b_gate, out_norm, w_o):
    b, t, _ = h.shape
    f32 = jnp.float32
    qk = MLSTM_HEADS * MLSTM_DK
    q, k, v, og = jnp.split(h @ w_in, [qk, 2 * qk, 2 * qk + D_MODEL], axis=-1)

    def heads(x, d):
        return x.reshape(b, t, MLSTM_HEADS, d).transpose(0, 2, 1, 3).astype(f32)

    q = heads(q, MLSTM_DK) * MLSTM_DK ** -0.5
    k = heads(k, MLSTM_DK)
    v = heads(v, MLSTM_DV)
    gates = (h @ w_gate + b_gate).astype(f32).reshape(b, t, 4, MLSTM_HEADS).transpose(2, 0, 3, 1)
    c0, n0, m0 = [s.astype(f32) for s in state0]
    h_f, cf, nf, mf = mlstm_chunkwise(q, k, v, gates[0], gates[1], c0[:, 0], n0[:, 0], m0[:, 0])
    flip = lambda x: jnp.flip(x, axis=2)
    h_b, cb, nbk, mb = mlstm_chunkwise(flip(q), flip(k), flip(v), flip(gates[2]), flip(gates[3]), c0[:, 1], n0[:, 1], m0[:, 1])
    hs = rmsnorm(h_f + flip(h_b), out_norm.reshape(MLSTM_HEADS, 1, MLSTM_DV))
    hs = hs.transpose(0, 2, 1, 3).reshape(b, t, D_MODEL)
    y = (jax.nn.sigmoid(og.astype(f32)) * hs).astype(h.dtype) @ w_o
    return y, (jnp.stack([cf, cb], axis=1), jnp.stack([nf, nbk], axis=1), jnp.stack([mf, mb], axis=1))


def centred_shift(x):
    prev = jnp.pad(x[:, :-1], ((0, 0), (1, 0), (0, 0)))
    nxt = jnp.pad(x[:, 1:], ((0, 0), (0, 1), (0, 0)))
    return 0.5 * (prev + nxt) - x


def rwkv_scan(s0, r, w, k, v, kk, a, reverse):
    def step(S, inp):
        rt, wt, kt, vt, kkt, at = inp
        sa = jnp.einsum('bhvk,bhk->bhv', S, kkt)
        S = S * wt[:, :, None, :] - sa[..., None] * (kkt * at)[:, :, None, :] + vt[..., None] * kt[:, :, None, :]
        return S, jnp.einsum('bhvk,bhk->bhv', S, rt)
    return lax.scan(step, s0, (r, w, k, v, kk, a), reverse=reverse)


def rwkv_mixer(h, s0, mu, w_rkv, w0, wA, wB, a0, aA, aB, gA, gB, k_k, k_a, r_k, ln_g, ln_b, w_o):
    b, t, d = h.shape
    f32 = jnp.float32
    xs = h[None] + centred_shift(h)[None] * mu[:, None, None, :]
    r, k, v = jnp.einsum('pbtd,pde->pbte', xs[:3], w_rkv)
    lw = jnp.einsum('zbtr,zrd->zbtd', jnp.tanh(jnp.einsum('btd,zdr->zbtr', xs[3], wA)), wB)
    w_par = (w0[:, None, None, :] + lw).astype(f32)
    decay = jnp.exp(-jnp.exp(-jax.nn.softplus(-w_par) - 0.5))
    a = jax.nn.sigmoid((a0[:, None, None, :] + jnp.einsum('zbtr,zrd->zbtd', jnp.einsum('btd,zdr->zbtr', xs[4], aA), aB)).astype(f32))
    g = jax.nn.sigmoid(xs[5] @ gA) @ gB
    r, k, v = r.astype(f32), k.astype(f32), v.astype(f32)
    kkh = (k * k_k.astype(f32)).reshape(b, t, RWKV_HEADS, RWKV_HEAD)
    kkh = kkh / jnp.maximum(jnp.linalg.norm(kkh, axis=-1, keepdims=True), 1e-12)
    k_dir = k[None] * (1.0 + (a - 1.0) * k_a.astype(f32))
    tm = lambda x: x.reshape(b, t, RWKV_HEADS, RWKV_HEAD).swapaxes(0, 1)
    s0 = s0.astype(f32)
    kk_t = kkh.swapaxes(0, 1)
    s_f, y_f = rwkv_scan(s0[:, 0], tm(r), tm(decay[0]), tm(k_dir[0]), tm(v), kk_t, tm(a[0]), False)
    s_b, y_b = rwkv_scan(s0[:, 1], tm(r), tm(decay[1]), tm(k_dir[1]), tm(v), kk_t, tm(a[1]), True)
    y = (y_f + y_b).swapaxes(0, 1)
    mean = jnp.mean(y, axis=-1, keepdims=True)
    var = jnp.mean(jnp.square(y - mean), axis=-1, keepdims=True)
    y = (y - mean) * lax.rsqrt(var + RWKV_GN_EPS) * ln_g.astype(f32).reshape(RWKV_HEADS, RWKV_HEAD) + ln_b.astype(f32).reshape(RWKV_HEADS, RWKV_HEAD)
    rh = r.reshape(b, t, RWKV_HEADS, RWKV_HEAD)
    kd = k_dir.reshape(2, b, t, RWKV_HEADS, RWKV_HEAD)
    bonus = jnp.sum(rh[None] * kd * r_k.astype(f32), axis=(0, -1))[..., None] * v.reshape(b, t, RWKV_HEADS, RWKV_HEAD)
    y = ((y + bonus).reshape(b, t, d) * g.astype(f32)).astype(h.dtype)
    return y @ w_o, jnp.stack([s_f, s_b], axis=1)


def setup_inputs(seed: int = 0) -> dict:
    key = jax.random.key(seed)
    ks = iter(jax.random.split(key, 48))
    D = D_MODEL

    def nrm(shape, scale=1.0):
        return jax.random.normal(next(ks), shape, jnp.float32) * scale

    inp = {}
    inp['x_prompt'] = nrm((BATCH, SEQ, D))
    inp['x_sample'] = nrm((DEC_BATCH, DEC_SEQ, D))
    inp['cache_k'] = nrm((DEC_BATCH, N_ATTN, PAST_LEN, ATTN_KV_HEADS, ATTN_HEAD_DIM))
    inp['cache_v'] = nrm((DEC_BATCH, N_ATTN, PAST_LEN, ATTN_KV_HEADS, ATTN_HEAD_DIM))
    inp['state_mlstm_C'] = nrm((DEC_BATCH, N_MLSTM, 2, MLSTM_HEADS, MLSTM_DK, MLSTM_DV), 0.1)
    inp['state_mlstm_n'] = nrm((DEC_BATCH, N_MLSTM, 2, MLSTM_HEADS, MLSTM_DK), 0.1)
    inp['state_mlstm_m'] = nrm((DEC_BATCH, N_MLSTM, 2, MLSTM_HEADS))
    inp['state_rwkv'] = nrm((DEC_BATCH, N_RWKV, 2, RWKV_HEADS, RWKV_HEAD, RWKV_HEAD), 0.1)
    inp['c'] = nrm((DEC_BATCH, D))
    inp['c_ctx'] = nrm((D,))
    inp['mod_w'] = nrm((DEPTH, D, N_MOD * D), 0.5 * D ** -0.5)
    inp['mod_b'] = nrm((DEPTH, N_MOD * D), 0.02)
    inp['norm_g'] = 1.0 + nrm((DEPTH, 3, D), 0.02)
    inp['ffn_w_in'] = nrm((DEPTH, 2, D, 2 * D_FF), D ** -0.5)
    inp['ffn_w_out'] = nrm((DEPTH, 2, D_FF, D), D_FF ** -0.5)
    inp['attn_w_qkv'] = nrm((N_ATTN, D, QKV_DIM), D ** -0.5)
    inp['attn_q_norm'] = 1.0 + nrm((N_ATTN, ATTN_HEAD_DIM), 0.02)
    inp['attn_k_norm'] = 1.0 + nrm((N_ATTN, ATTN_HEAD_DIM), 0.02)
    inp['attn_sink'] = nrm((N_ATTN, ATTN_HEADS), 0.5)
    inp['attn_w_o'] = nrm((N_ATTN, ATTN_HEADS * ATTN_HEAD_DIM, D), (ATTN_HEADS * ATTN_HEAD_DIM) ** -0.5)
    inp['mlstm_w_in'] = nrm((N_MLSTM, D, MLSTM_IN_DIM), D ** -0.5)
    inp['mlstm_w_gate'] = nrm((N_MLSTM, D, 4 * MLSTM_HEADS), D ** -0.5)
    gate_base = jnp.tile(jnp.array([0.0, 3.0, 0.0, 3.0], jnp.float32)[:, None], (1, MLSTM_HEADS)).reshape(-1)
    inp['mlstm_b_gate'] = gate_base[None, :] + nrm((N_MLSTM, 4 * MLSTM_HEADS), 0.1)
    inp['mlstm_out_norm'] = 1.0 + nrm((N_MLSTM, D), 0.02)
    inp['mlstm_w_o'] = nrm((N_MLSTM, D, D), D ** -0.5)
    inp['rwkv_mu'] = jax.random.uniform(next(ks), (N_RWKV, 6, D), jnp.float32)
    inp['rwkv_w_rkv'] = nrm((N_RWKV, 3, D, D), D ** -0.5)
    inp['rwkv_w0'] = nrm((N_RWKV, 2, D), 0.5)
    inp['rwkv_wA'] = nrm((N_RWKV, 2, D, RWKV_LORA_W), D ** -0.5)
    inp['rwkv_wB'] = nrm((N_RWKV, 2, RWKV_LORA_W, D), 0.5 * RWKV_LORA_W ** -0.5)
    inp['rwkv_a0'] = nrm((N_RWKV, 2, D), 0.5)
    inp['rwkv_aA'] = nrm((N_RWKV, 2, D, RWKV_LORA_A), D ** -0.5)
    inp['rwkv_aB'] = nrm((N_RWKV, 2, RWKV_LORA_A, D), 0.5 * RWKV_LORA_A ** -0.5)
    inp['rwkv_gA'] = nrm((N_RWKV, D, RWKV_LORA_G), D ** -0.5)
    inp['rwkv_gB'] = nrm((N_RWKV, RWKV_LORA_G, D), RWKV_LORA_G ** -0.5)
    inp['rwkv_k_k'] = 0.85 + nrm((N_RWKV, D), 0.05)
    inp['rwkv_k_a'] = 1.0 + nrm((N_RWKV, D), 0.05)
    inp['rwkv_r_k'] = nrm((N_RWKV, RWKV_HEADS, RWKV_HEAD), 0.1)
    inp['rwkv_ln_g'] = 1.0 + nrm((N_RWKV, D), 0.02)
    inp['rwkv_ln_b'] = nrm((N_RWKV, D), 0.02)
    inp['rwkv_w_o'] = nrm((N_RWKV, D, D), D ** -0.5)
    return inp


def reference(x_prompt, x_sample, cache_k, cache_v, state_mlstm_C, state_mlstm_n, state_mlstm_m, state_rwkv,
              c, c_ctx, mod_w, mod_b, norm_g, ffn_w_in, ffn_w_out,
              attn_w_qkv, attn_q_norm, attn_k_norm, attn_sink, attn_w_o,
              mlstm_w_in, mlstm_w_gate, mlstm_b_gate, mlstm_out_norm, mlstm_w_o,
              rwkv_mu, rwkv_w_rkv, rwkv_w0, rwkv_wA, rwkv_wB, rwkv_a0, rwkv_aA, rwkv_aB,
              rwkv_gA, rwkv_gB, rwkv_k_k, rwkv_k_a, rwkv_r_k, rwkv_ln_g, rwkv_ln_b, rwkv_w_o):
    xc = x_prompt
    xl = x_sample
    b_ctx = x_prompt.shape[0]
    new_k, new_v, new_C, new_n, new_m, new_S = [], [], [], [], [], []
    for i in range(DEPTH):
        kind, slot = i % N_MIXERS, i // N_MIXERS
        mod_ctx = adaln(c_ctx[None, :], mod_w[i], mod_b[i])
        mod_lat = adaln(c, mod_w[i], mod_b[i])
        lp = (norm_g[i], ffn_w_in[i], ffn_w_out[i])
        if kind == 0:
            ap = (attn_w_qkv[slot], attn_q_norm[slot], attn_k_norm[slot], attn_sink[slot], attn_w_o[slot])
            xc, (k_ctx, v_ctx) = macaron_layer(xc, mod_ctx, *lp, lambda h: attn_context(h, *ap))
            xl, _ = macaron_layer(xl, mod_lat, *lp, lambda h: attn_latent(h, cache_k[:, slot], cache_v[:, slot], *ap))
            new_k.append(k_ctx)
            new_v.append(v_ctx)
        elif kind == 1:
            mp = (mlstm_w_in[slot], mlstm_w_gate[slot], mlstm_b_gate[slot], mlstm_out_norm[slot], mlstm_w_o[slot])
            zero_state = (jnp.zeros((b_ctx, 2, MLSTM_HEADS, MLSTM_DK, MLSTM_DV), jnp.float32),
                          jnp.zeros((b_ctx, 2, MLSTM_HEADS, MLSTM_DK), jnp.float32),
                          jnp.zeros((b_ctx, 2, MLSTM_HEADS), jnp.float32))
            lat_state = (state_mlstm_C[:, slot], state_mlstm_n[:, slot], state_mlstm_m[:, slot])
            xc, (C_ctx, n_ctx, m_ctx) = macaron_layer(xc, mod_ctx, *lp, lambda h: mlstm_mixer(h, zero_state, *mp))
            xl, _ = macaron_layer(xl, mod_lat, *lp, lambda h: mlstm_mixer(h, lat_state, *mp))
            new_C.append(C_ctx)
            new_n.append(n_ctx)
            new_m.append(m_ctx)
        else:
            rp = (rwkv_mu[slot], rwkv_w_rkv[slot], rwkv_w0[slot], rwkv_wA[slot], rwkv_wB[slot], rwkv_a0[slot],
                  rwkv_aA[slot], rwkv_aB[slot], rwkv_gA[slot], rwkv_gB[slot], rwkv_k_k[slot], rwkv_k_a[slot],
                  rwkv_r_k[slot], rwkv_ln_g[slot], rwkv_ln_b[slot], rwkv_w_o[slot])
            zero_s = jnp.zeros((b_ctx, 2, RWKV_HEADS, RWKV_HEAD, RWKV_HEAD), jnp.float32)
            xc, S_ctx = macaron_layer(xc, mod_ctx, *lp, lambda h: rwkv_mixer(h, zero_s, *rp))
            xl, _ = macaron_layer(xl, mod_lat, *lp, lambda h: rwkv_mixer(h, state_rwkv[:, slot], *rp))
            new_S.append(S_ctx)
    new_cache_k = jnp.stack(new_k, axis=1)
    new_cache_v = jnp.stack(new_v, axis=1)
    new_state_mlstm_C = jnp.stack(new_C, axis=1)
    new_state_mlstm_n = jnp.stack(new_n, axis=1)
    new_state_mlstm_m = jnp.stack(new_m, axis=1)
    new_state_rwkv = jnp.stack(new_S, axis=1)
    return (xc, xl, new_cache_k, new_cache_v, new_state_mlstm_C, new_state_mlstm_n, new_state_mlstm_m, new_state_rwkv)
```

```python
import functools

import jax
import jax.numpy as jnp
from jax import lax
from jax.experimental import pallas as pl
from jax.experimental.pallas import tpu as pltpu

D_MODEL = 2048
BATCH = 32
SEQ = 256
DEPTH = 4
DEC_BATCH = 4
DEC_SEQ = 1024
GRID_W = 64
N_MIXERS = 3
N_MOD = 9
D_FF = 5632
EPS = 1e-6

ATTN_HEADS = 16
ATTN_KV_HEADS = 4
ATTN_GROUP = ATTN_HEADS // ATTN_KV_HEADS
ATTN_HEAD_DIM = D_MODEL // ATTN_HEADS
ATTN_WINDOW = 128
ATTN_BLOCK = 128
ROPE_THETA = 10000.0

MLSTM_HEADS = 8
MLSTM_DV = D_MODEL // MLSTM_HEADS
MLSTM_DK = MLSTM_DV // 2
MLSTM_CHUNK = 64

RWKV_HEAD = 64
RWKV_HEADS = D_MODEL // RWKV_HEAD
RWKV_GN_EPS = 64e-5

N_CTX_TOK = BATCH * SEQ
N_LAT_TOK = DEC_BATCH * DEC_SEQ
N_TOK = N_CTX_TOK + N_LAT_TOK
MOD_ROWS = 8

VMEM_LIMIT = 52 * 1024 * 1024
BF16 = jnp.bfloat16
F32 = jnp.float32


def _mod_row(i, tm):
    n_ctx_tiles = N_CTX_TOK // tm
    tiles_per_lat = DEC_SEQ // tm
    return jnp.where(i < n_ctx_tiles, 0, 1 + (i - n_ctx_tiles) // tiles_per_lat)


def _adaln_kernel(cond_ref, w_ref, b_ref, o_ref):
    c = cond_ref[...]
    a = (c * jax.nn.sigmoid(c)).astype(BF16)
    o_ref[0] = jnp.dot(a, w_ref[0].astype(BF16), preferred_element_type=F32) + b_ref[0]


def _adaln_all(cond8, mod_w, mod_b):
    tn = 1024
    n_out = N_MOD * D_MODEL
    return pl.pallas_call(
        _adaln_kernel,
        grid=(DEPTH, n_out // tn),
        in_specs=[
            pl.BlockSpec((MOD_ROWS, D_MODEL), lambda l, j: (0, 0)),
            pl.BlockSpec((1, D_MODEL, tn), lambda l, j: (l, 0, j)),
            pl.BlockSpec((1, 1, tn), lambda l, j: (l, 0, j)),
        ],
        out_specs=pl.BlockSpec((1, MOD_ROWS, tn), lambda l, j: (l, 0, j)),
        out_shape=jax.ShapeDtypeStruct((DEPTH, MOD_ROWS, n_out), F32),
        compiler_params=pltpu.CompilerParams(
            dimension_semantics=("arbitrary", "arbitrary"), vmem_limit_bytes=VMEM_LIMIT),
        name="adaln",
    )(cond8, mod_w, mod_b.reshape(DEPTH, 1, n_out))


def _modulated(x, g, shift, scale):
    ms = jnp.mean(x * x, axis=-1, keepdims=True)
    y = x * lax.rsqrt(ms + EPS) * g
    return y * (1.0 + scale) + shift


def _ffn_kernel(x_ref, shift_ref, scale_ref, gate_ref, g_ref, wg_ref, wu_ref, wo_ref, o_ref, h_scr, acc_scr):
    f = pl.program_id(1)

    @pl.when(f == 0)
    def _():
        h = _modulated(x_ref[...], g_ref[...], shift_ref[0], scale_ref[0])
        h_scr[...] = h.astype(BF16)
        acc_scr[...] = jnp.zeros_like(acc_scr)

    h = h_scr[...]
    a = jnp.dot(h, wg_ref[...], preferred_element_type=F32)
    b = jnp.dot(h, wu_ref[...], preferred_element_type=F32)
    act = (a * jax.nn.sigmoid(a)) * b
    acc_scr[...] += jnp.dot(act.astype(BF16), wo_ref[...], preferred_element_type=F32)

    @pl.when(f == pl.num_programs(1) - 1)
    def _():
        o_ref[...] = x_ref[...] + (0.5 * gate_ref[0]) * acc_scr[...]


def _ffn(x, mods, slot, g_row, w_in, w_out):
    tm, tf = 512, 512
    nf = D_FF // tf
    mod_spec = lambda s: pl.BlockSpec((1, 1, D_MODEL), lambda i, f, s=s: (_mod_row(i, tm), 0, s))
    return pl.pallas_call(
        _ffn_kernel,
        grid=(N_TOK // tm, nf),
        in_specs=[
            pl.BlockSpec((tm, D_MODEL), lambda i, f: (i, 0)),
            mod_spec(slot), mod_spec(slot + 1), mod_spec(slot + 2),
            pl.BlockSpec((1, D_MODEL), lambda i, f: (0, 0)),
            pl.BlockSpec((D_MODEL, tf), lambda i, f: (0, f)),
            pl.BlockSpec((D_MODEL, tf), lambda i, f: (0, f + nf)),
            pl.BlockSpec((tf, D_MODEL), lambda i, f: (f, 0)),
        ],
        out_specs=pl.BlockSpec((tm, D_MODEL), lambda i, f: (i, 0)),
        out_shape=jax.ShapeDtypeStruct((N_TOK, D_MODEL), F32),
        scratch_shapes=[pltpu.VMEM((tm, D_MODEL), BF16), pltpu.VMEM((tm, D_MODEL), F32)],
        compiler_params=pltpu.CompilerParams(
            dimension_semantics=("arbitrary", "arbitrary"), vmem_limit_bytes=VMEM_LIMIT),
        name="ffn",
    )(x, mods, mods, mods, g_row, w_in, w_in, w_out)


def _rmsnorm(x, g, eps=EPS):
    xf = x.astype(F32)
    y = xf * lax.rsqrt(jnp.mean(xf * xf, axis=-1, keepdims=True) + eps)
    return (y * g.astype(F32)).astype(x.dtype)


def _rotate(x, ang):
    f = ang.shape[-1]
    cos = jnp.cos(ang)[:, None, :].astype(x.dtype)
    sin = jnp.sin(ang)[:, None, :].astype(x.dtype)
    xa, xb = x[..., :f], x[..., f:]
    return jnp.concatenate([xa * cos - xb * sin, xb * cos + xa * sin], axis=-1)


def _rope_2d(x):
    n_tok = x.shape[1]
    rows = n_tok // GRID_W
    row = jnp.repeat(jnp.arange(rows, dtype=F32), GRID_W)
    col = jnp.tile(jnp.arange(GRID_W, dtype=F32), rows)
    n_freq = ATTN_HEAD_DIM // 4
    inv_freq = ROPE_THETA ** (-jnp.arange(n_freq, dtype=F32) / n_freq)
    half = ATTN_HEAD_DIM // 2
    return jnp.concatenate([_rotate(x[..., :half], row[:, None] * inv_freq),
                            _rotate(x[..., half:], col[:, None] * inv_freq)], axis=-1)


def _attn_qkv(h, w_qkv, q_norm, k_norm):
    b, t, _ = h.shape
    q, k, v = jnp.split(h @ w_qkv, [ATTN_HEADS * ATTN_HEAD_DIM, (ATTN_HEADS + ATTN_KV_HEADS) * ATTN_HEAD_DIM], axis=-1)
    q = _rmsnorm(q.reshape(b, t, ATTN_HEADS, ATTN_HEAD_DIM), q_norm)
    k = _rmsnorm(k.reshape(b, t, ATTN_KV_HEADS, ATTN_HEAD_DIM), k_norm)
    v = v.reshape(b, t, ATTN_KV_HEADS, ATTN_HEAD_DIM)
    return q, k, v


def _sink_attend(q, k, v, valid, sink):
    s = jnp.einsum('bqhgd,bshd->bhgqs', q.astype(F32), k.astype(F32)) * ATTN_HEAD_DIM ** -0.5
    if valid is not None:
        s = jnp.where(valid, s, -jnp.inf)
    sk = jnp.broadcast_to(sink.astype(F32).reshape(1, ATTN_KV_HEADS, ATTN_GROUP, 1, 1), s.shape[:-1] + (1,))
    p = jax.nn.softmax(jnp.concatenate([s, sk], axis=-1), axis=-1)[..., :-1]
    return jnp.einsum('bhgqs,bshd->bqhgd', p.astype(v.dtype), v)


def _attn_context(h, w_qkv, q_norm, k_norm, sink, w_o):
    b, t, _ = h.shape
    q, k, v = _attn_qkv(h, w_qkv, q_norm, k_norm)
    nb = t // ATTN_BLOCK
    qb = q.reshape(b, nb, ATTN_BLOCK, ATTN_KV_HEADS, ATTN_GROUP, ATTN_HEAD_DIM).swapaxes(0, 1)
    o = lax.map(lambda qblk: _sink_attend(qblk, k, v, None, sink), qb)
    o = o.swapaxes(0, 1).reshape(b, t, ATTN_HEADS * ATTN_HEAD_DIM)
    return o @ w_o, (k, v)


def _attn_latent(h, ctx_k, ctx_v, w_qkv, q_norm, k_norm, sink, w_o):
    b, t, _ = h.shape
    q, k, v = _attn_qkv(h, w_qkv, q_norm, k_norm)
    q, k = _rope_2d(q), _rope_2d(k)
    span = ATTN_BLOCK + 2 * ATTN_WINDOW
    pad = ((0, 0), (ATTN_WINDOW, ATTN_WINDOW), (0, 0), (0, 0))
    k_pad, v_pad = jnp.pad(k, pad), jnp.pad(v, pad)
    n_ctx = ctx_k.shape[1]
    q_idx = jnp.arange(ATTN_BLOCK)[:, None]
    k_idx = jnp.arange(span)[None, :]
    in_window = jnp.abs(k_idx - ATTN_WINDOW - q_idx) <= ATTN_WINDOW
    ctx_valid = jnp.ones((ATTN_BLOCK, n_ctx), dtype=bool)

    def block(args):
        qblk, blk = args
        start = blk * ATTN_BLOCK
        kb = lax.dynamic_slice_in_dim(k_pad, start, span, axis=1)
        vb = lax.dynamic_slice_in_dim(v_pad, start, span, axis=1)
        key_pos = start - ATTN_WINDOW + k_idx
        valid = in_window & (key_pos >= 0) & (key_pos < t)
        valid = jnp.concatenate([valid, ctx_valid], axis=-1)
        return _sink_attend(qblk, jnp.concatenate([kb, ctx_k], axis=1), jnp.concatenate([vb, ctx_v], axis=1), valid, sink)

    nb = t // ATTN_BLOCK
    qb = q.reshape(b, nb, ATTN_BLOCK, ATTN_KV_HEADS, ATTN_GROUP, ATTN_HEAD_DIM).swapaxes(0, 1)
    o = lax.map(block, (qb, jnp.arange(nb)))
    o = o.swapaxes(0, 1).reshape(b, t, ATTN_HEADS * ATTN_HEAD_DIM)
    return o @ w_o, None


def _mlstm_chunkwise(q, k, v, i_pre, f_pre, c0, n0, m0):
    b, nh, t, _ = q.shape
    L = MLSTM_CHUNK
    nc = t // L

    def chunks(x):
        return jnp.moveaxis(x.reshape(x.shape[:2] + (nc, L) + x.shape[3:]), 2, 0)

    causal = jnp.tril(jnp.ones((L, L), dtype=bool))

    def step(carry, inp):
        C, n, m = carry
        qc, kc, vc, ic, lfc = inp
        bcum = jnp.cumsum(lfc, axis=-1)
        d = jnp.where(causal, bcum[..., :, None] - bcum[..., None, :] + ic[..., None, :], -jnp.inf)
        inter = bcum + m[..., None]
        m_c = jnp.maximum(inter, jnp.max(d, axis=-1))
        s = jnp.einsum('bhtd,bhsd->bhts', qc, kc) * jnp.exp(d - m_c[..., None])
        a = jnp.exp(inter - m_c)
        num = jnp.einsum('bhts,bhsv->bhtv', s, vc) + a[..., None] * jnp.einsum('bhtd,bhdv->bhtv', qc, C)
        den = jnp.sum(s, axis=-1) + a * jnp.einsum('bhtd,bhd->bht', qc, n)
        hc = num / jnp.maximum(jnp.abs(den), jnp.exp(-m_c))[..., None]
        bl = bcum[..., -1]
        g = bl[..., None] - bcum + ic
        m_new = jnp.maximum(bl + m, jnp.max(g, axis=-1))
        decay = jnp.exp(bl + m - m_new)
        wg = jnp.exp(g - m_new[..., None])
        C_new = decay[..., None, None] * C + jnp.einsum('bhs,bhsd,bhsv->bhdv', wg, kc, vc)
        n_new = decay[..., None] * n + jnp.einsum('bhs,bhsd->bhd', wg, kc)
        return (C_new, n_new, m_new), hc

    lf = jax.nn.log_sigmoid(f_pre)
    (C, n, m), h = lax.scan(step, (c0, n0, m0), (chunks(q), chunks(k), chunks(v), chunks(i_pre), chunks(lf)))
    h = jnp.moveaxis(h, 0, 2).reshape(b, nh, t, -1)
    return h, C, n, m


def _mlstm_mixer(h, state0, w_in, w_gate, b_gate, out_norm, w_o):
    b, t, _ = h.shape
    qk = MLSTM_HEADS * MLSTM_DK
    q, k, v, og = jnp.split(h @ w_in, [qk, 2 * qk, 2 * qk + D_MODEL], axis=-1)

    def heads(x, d):
        return x.reshape(b, t, MLSTM_HEADS, d).transpose(0, 2, 1, 3).astype(F32)

    q = heads(q, MLSTM_DK) * MLSTM_DK ** -0.5
    k = heads(k, MLSTM_DK)
    v = heads(v, MLSTM_DV)
    gates = (h @ w_gate + b_gate).astype(F32).reshape(b, t, 4, MLSTM_HEADS).transpose(2, 0, 3, 1)
    c0, n0, m0 = [s.astype(F32) for s in state0]
    h_f, cf, nf, mf = _mlstm_chunkwise(q, k, v, gates[0], gates[1], c0[:, 0], n0[:, 0], m0[:, 0])
    flip = lambda x: jnp.flip(x, axis=2)
    h_b, cb, nbk, mb = _mlstm_chunkwise(flip(q), flip(k), flip(v), flip(gates[2]), flip(gates[3]), c0[:, 1], n0[:, 1], m0[:, 1])
    hs = _rmsnorm(h_f + flip(h_b), out_norm.reshape(MLSTM_HEADS, 1, MLSTM_DV))
    hs = hs.transpose(0, 2, 1, 3).reshape(b, t, D_MODEL)
    y = (jax.nn.sigmoid(og.astype(F32)) * hs).astype(h.dtype) @ w_o
    return y, (jnp.stack([cf, cb], axis=1), jnp.stack([nf, nbk], axis=1), jnp.stack([mf, mb], axis=1))


def _centred_shift(x):
    prev = jnp.pad(x[:, :-1], ((0, 0), (1, 0), (0, 0)))
    nxt = jnp.pad(x[:, 1:], ((0, 0), (0, 1), (0, 0)))
    return 0.5 * (prev + nxt) - x


def _rwkv_scan(s0, r, w, k, v, kk, a, reverse):
    def step(S, inp):
        rt, wt, kt, vt, kkt, at = inp
        sa = jnp.einsum('bhvk,bhk->bhv', S, kkt)
        S = S * wt[:, :, None, :] - sa[..., None] * (kkt * at)[:, :, None, :] + vt[..., None] * kt[:, :, None, :]
        return S, jnp.einsum('bhvk,bhk->bhv', S, rt)
    return lax.scan(step, s0, (r, w, k, v, kk, a), reverse=reverse)


def _rwkv_mixer(h, s0, mu, w_rkv, w0, wA, wB, a0, aA, aB, gA, gB, k_k, k_a, r_k, ln_g, ln_b, w_o):
    b, t, d = h.shape
    xs = h[None] + _centred_shift(h)[None] * mu[:, None, None, :]
    r, k, v = jnp.einsum('pbtd,pde->pbte', xs[:3], w_rkv)
    lw = jnp.einsum('zbtr,zrd->zbtd', jnp.tanh(jnp.einsum('btd,zdr->zbtr', xs[3], wA)), wB)
    w_par = (w0[:, None, None, :] + lw).astype(F32)
    decay = jnp.exp(-jnp.exp(-jax.nn.softplus(-w_par) - 0.5))
    a = jax.nn.sigmoid((a0[:, None, None, :] + jnp.einsum('zbtr,zrd->zbtd', jnp.einsum('btd,zdr->zbtr', xs[4], aA), aB)).astype(F32))
    g = jax.nn.sigmoid(xs[5] @ gA) @ gB
    r, k, v = r.astype(F32), k.astype(F32), v.astype(F32)
    kkh = (k * k_k.astype(F32)).reshape(b, t, RWKV_HEADS, RWKV_HEAD)
    kkh = kkh / jnp.maximum(jnp.linalg.norm(kkh, axis=-1, keepdims=True), 1e-12)
    k_dir = k[None] * (1.0 + (a - 1.0) * k_a.astype(F32))
    tm = lambda x: x.reshape(b, t, RWKV_HEADS, RWKV_HEAD).swapaxes(0, 1)
    s0 = s0.astype(F32)
    kk_t = kkh.swapaxes(0, 1)
    s_f, y_f = _rwkv_scan(s0[:, 0], tm(r), tm(decay[0]), tm(k_dir[0]), tm(v), kk_t, tm(a[0]), False)
    s_b, y_b = _rwkv_scan(s0[:, 1], tm(r), tm(decay[1]), tm(k_dir[1]), tm(v), kk_t, tm(a[1]), True)
    y = (y_f + y_b).swapaxes(0, 1)
    mean = jnp.mean(y, axis=-1, keepdims=True)
    var = jnp.mean(jnp.square(y - mean), axis=-1, keepdims=True)
    y = (y - mean) * lax.rsqrt(var + RWKV_GN_EPS) * ln_g.astype(F32).reshape(RWKV_HEADS, RWKV_HEAD) + ln_b.astype(F32).reshape(RWKV_HEADS, RWKV_HEAD)
    rh = r.reshape(b, t, RWKV_HEADS, RWKV_HEAD)
    kd = k_dir.reshape(2, b, t, RWKV_HEADS, RWKV_HEAD)
    bonus = jnp.sum(rh[None] * kd * r_k.astype(F32), axis=(0, -1))[..., None] * v.reshape(b, t, RWKV_HEADS, RWKV_HEAD)
    y = ((y + bonus).reshape(b, t, d) * g.astype(F32)).astype(h.dtype)
    return y @ w_o, jnp.stack([s_f, s_b], axis=1)


def _mixer_sublayer(x, mods_l, g_row, mixer_ctx, mixer_lat):
    xc = x[:N_CTX_TOK].reshape(BATCH, SEQ, D_MODEL)
    xl = x[N_CTX_TOK:].reshape(DEC_BATCH, DEC_SEQ, D_MODEL)
    m = lambda rows, j: mods_l[rows, None, j * D_MODEL:(j + 1) * D_MODEL]
    rc, rl = slice(0, 1), slice(1, 1 + DEC_BATCH)
    hc = _rmsnorm(xc, g_row) * (1.0 + m(rc, 4)) + m(rc, 3)
    hl = _rmsnorm(xl, g_row) * (1.0 + m(rl, 4)) + m(rl, 3)
    yc, st = mixer_ctx(hc)
    yl, _ = mixer_lat(hl)
    xc = xc + m(rc, 5) * yc
    xl = xl + m(rl, 5) * yl
    return jnp.concatenate([xc.reshape(N_CTX_TOK, D_MODEL), xl.reshape(N_LAT_TOK, D_MODEL)], axis=0), st


def kernel(x_prompt, x_sample, cache_k, cache_v, state_mlstm_C, state_mlstm_n, state_mlstm_m, state_rwkv, c, c_ctx, mod_w, mod_b, norm_g, ffn_w_in, ffn_w_out, attn_w_qkv, attn_q_norm, attn_k_norm, attn_sink, attn_w_o, mlstm_w_in, mlstm_w_gate, mlstm_b_gate, mlstm_out_norm, mlstm_w_o, rwkv_mu, rwkv_w_rkv, rwkv_w0, rwkv_wA, rwkv_wB, rwkv_a0, rwkv_aA, rwkv_aB, rwkv_gA, rwkv_gB, rwkv_k_k, rwkv_k_a, rwkv_r_k, rwkv_ln_g, rwkv_ln_b, rwkv_w_o):
    x = jnp.concatenate([x_prompt.reshape(N_CTX_TOK, D_MODEL), x_sample.reshape(N_LAT_TOK, D_MODEL)], axis=0)
    cond8 = jnp.concatenate([c_ctx[None, :], c, jnp.zeros((MOD_ROWS - 1 - DEC_BATCH, D_MODEL), F32)], axis=0)
    mods = _adaln_all(cond8, mod_w, mod_b)
    ffn_w_in_b = ffn_w_in.astype(BF16)
    ffn_w_out_b = ffn_w_out.astype(BF16)

    new_k, new_v, new_C, new_n, new_m, new_S = [], [], [], [], [], []
    for i in range(DEPTH):
        kind, slot = i % N_MIXERS, i // N_MIXERS
        mods_l = mods[i]
        mods3 = mods_l.reshape(MOD_ROWS, 1, N_MOD * D_MODEL)
        x = _ffn(x, mods3, 0, norm_g[i, 0][None, :], ffn_w_in_b[i, 0], ffn_w_out_b[i, 0])
        if kind == 0:
            ap = (attn_w_qkv[slot], attn_q_norm[slot], attn_k_norm[slot], attn_sink[slot], attn_w_o[slot])
            x, (k_ctx, v_ctx) = _mixer_sublayer(
                x, mods_l, norm_g[i, 1],
                lambda h: _attn_context(h, *ap),
                lambda h: _attn_latent(h, cache_k[:, slot], cache_v[:, slot], *ap))
            new_k.append(k_ctx)
            new_v.append(v_ctx)
        elif kind == 1:
            mp = (mlstm_w_in[slot], mlstm_w_gate[slot], mlstm_b_gate[slot], mlstm_out_norm[slot], mlstm_w_o[slot])
            zero_state = (jnp.zeros((BATCH, 2, MLSTM_HEADS, MLSTM_DK, MLSTM_DV), F32),
                          jnp.zeros((BATCH, 2, MLSTM_HEADS, MLSTM_DK), F32),
                          jnp.zeros((BATCH, 2, MLSTM_HEADS), F32))
            lat_state = (state_mlstm_C[:, slot], state_mlstm_n[:, slot], state_mlstm_m[:, slot])
            x, (C_ctx, n_ctx, m_ctx) = _mixer_sublayer(
                x, mods_l, norm_g[i, 1],
                lambda h: _mlstm_mixer(h, zero_state, *mp),
                lambda h: _mlstm_mixer(h, lat_state, *mp))
            new_C.append(C_ctx)
            new_n.append(n_ctx)
            new_m.append(m_ctx)
        else:
            rp = (rwkv_mu[slot], rwkv_w_rkv[slot], rwkv_w0[slot], rwkv_wA[slot], rwkv_wB[slot], rwkv_a0[slot],
                  rwkv_aA[slot], rwkv_aB[slot], rwkv_gA[slot], rwkv_gB[slot], rwkv_k_k[slot], rwkv_k_a[slot],
                  rwkv_r_k[slot], rwkv_ln_g[slot], rwkv_ln_b[slot], rwkv_w_o[slot])
            zero_s = jnp.zeros((BATCH, 2, RWKV_HEADS, RWKV_HEAD, RWKV_HEAD), F32)
            x, S_ctx = _mixer_sublayer(
                x, mods_l, norm_g[i, 1],
                lambda h: _rwkv_mixer(h, zero_s, *rp),
                lambda h: _rwkv_mixer(h, state_rwkv[:, slot], *rp))
            new_S.append(S_ctx)
        x = _ffn(x, mods3, 6, norm_g[i, 2][None, :], ffn_w_in_b[i, 1], ffn_w_out_b[i, 1])

    y_prompt = x[:N_CTX_TOK].reshape(BATCH, SEQ, D_MODEL)
    y_sample = x[N_CTX_TOK:].reshape(DEC_BATCH, DEC_SEQ, D_MODEL)
    return (y_prompt, y_sample, jnp.stack(new_k, axis=1), jnp.stack(new_v, axis=1),
            jnp.stack(new_C, axis=1), jnp.stack(new_n, axis=1), jnp.stack(new_m, axis=1), jnp.stack(new_S, axis=1))
```

```python
import functools

import jax
import jax.numpy as jnp
from jax import lax
from jax.experimental import pallas as pl
from jax.experimental.pallas import tpu as pltpu

D_MODEL = 2048
BATCH = 32
SEQ = 256
DEPTH = 4
DEC_BATCH = 4
DEC_SEQ = 1024
PAST_LEN = 256
GRID_W = 64
N_MIXERS = 3
N_MOD = 9
D_FF = 5632
EPS = 1e-6

ATTN_HEADS = 16
ATTN_KV_HEADS = 4
ATTN_GROUP = ATTN_HEADS // ATTN_KV_HEADS
ATTN_HEAD_DIM = D_MODEL // ATTN_HEADS
ATTN_WINDOW = 128
ATTN_BLOCK = 128
ROPE_THETA = 10000.0
QKV_DIM = (ATTN_HEADS + 2 * ATTN_KV_HEADS) * ATTN_HEAD_DIM

MLSTM_HEADS = 8
MLSTM_DV = D_MODEL // MLSTM_HEADS
MLSTM_DK = MLSTM_DV // 2
MLSTM_CHUNK = 64
MLSTM_IN_DIM = 2 * MLSTM_HEADS * MLSTM_DK + 2 * D_MODEL

RWKV_HEAD = 64
RWKV_HEADS = D_MODEL // RWKV_HEAD
RWKV_PAIRS = RWKV_HEADS // 2
RWKV_CHUNK = 64
RWKV_GN_EPS = 64e-5

N_CTX_TOK = BATCH * SEQ
N_LAT_TOK = DEC_BATCH * DEC_SEQ
N_TOK = N_CTX_TOK + N_LAT_TOK
MOD_ROWS = 8
LANES = 128

VMEM_LIMIT = 52 * 1024 * 1024
BF16 = jnp.bfloat16
F32 = jnp.float32
NEG_INF = float("-inf")


def _cparams(n_axes):
    return pltpu.CompilerParams(dimension_semantics=("arbitrary",) * n_axes, vmem_limit_bytes=VMEM_LIMIT)


def _mod_row(i, tm):
    n_ctx_tiles = N_CTX_TOK // tm
    tiles_per_lat = DEC_SEQ // tm
    return jnp.where(i < n_ctx_tiles, 0, 1 + (i - n_ctx_tiles) // tiles_per_lat)


def _mod_spec(slot, tm):
    return pl.BlockSpec((1, 1, D_MODEL), lambda i, j, s=slot: (_mod_row(i, tm), 0, s))


def _dot(a, b):
    return jnp.dot(a, b, preferred_element_type=F32)


def _dot_nt(a, b):
    return lax.dot_general(a, b, (((1,), (1,)), ((), ())), preferred_element_type=F32)


def _dot_tn(a, b):
    return lax.dot_general(a, b, (((0,), (0,)), ((), ())), preferred_element_type=F32)


def _sigmoid(x):
    return jax.nn.sigmoid(x)


def _modulated(x, g, shift, scale):
    ms = jnp.mean(x * x, axis=-1, keepdims=True)
    y = x * lax.rsqrt(ms + EPS) * g
    return y * (1.0 + scale) + shift


def _adaln_kernel(cond_ref, w_ref, b_ref, o_ref):
    c = cond_ref[...]
    a = (c * _sigmoid(c)).astype(BF16)
    o_ref[0] = _dot(a, w_ref[0].astype(BF16)) + b_ref[0]


def _adaln_all(cond8, mod_w, mod_b):
    tn = 1024
    n_out = N_MOD * D_MODEL
    return pl.pallas_call(
        _adaln_kernel,
        grid=(DEPTH, n_out // tn),
        in_specs=[
            pl.BlockSpec((MOD_ROWS, D_MODEL), lambda l, j: (0, 0)),
            pl.BlockSpec((1, D_MODEL, tn), lambda l, j: (l, 0, j)),
            pl.BlockSpec((1, 1, tn), lambda l, j: (l, 0, j)),
        ],
        out_specs=pl.BlockSpec((1, MOD_ROWS, tn), lambda l, j: (l, 0, j)),
        out_shape=jax.ShapeDtypeStruct((DEPTH, MOD_ROWS, n_out), F32),
        compiler_params=_cparams(2),
        name="adaln",
    )(cond8, mod_w, mod_b.reshape(DEPTH, 1, n_out))


def _ffn_kernel(x_ref, shift_ref, scale_ref, gate_ref, g_ref, wg_ref, wu_ref, wo_ref, o_ref, h_scr, acc_scr):
    f = pl.program_id(1)

    @pl.when(f == 0)
    def _():
        h = _modulated(x_ref[...], g_ref[...], shift_ref[0], scale_ref[0])
        h_scr[...] = h.astype(BF16)
        acc_scr[...] = jnp.zeros_like(acc_scr)

    h = h_scr[...]
    a = _dot(h, wg_ref[...])
    b = _dot(h, wu_ref[...])
    act = (a * _sigmoid(a)) * b
    acc_scr[...] += _dot(act.astype(BF16), wo_ref[...])

    @pl.when(f == pl.num_programs(1) - 1)
    def _():
        o_ref[...] = x_ref[...] + (0.5 * gate_ref[0]) * acc_scr[...]


def _ffn(x, mods3, slot, g_row, w_in, w_out):
    tm, tf = 512, 512
    nf = D_FF // tf
    return pl.pallas_call(
        _ffn_kernel,
        grid=(N_TOK // tm, nf),
        in_specs=[
            pl.BlockSpec((tm, D_MODEL), lambda i, f: (i, 0)),
            _mod_spec(slot, tm), _mod_spec(slot + 1, tm), _mod_spec(slot + 2, tm),
            pl.BlockSpec((1, D_MODEL), lambda i, f: (0, 0)),
            pl.BlockSpec((D_MODEL, tf), lambda i, f: (0, f)),
            pl.BlockSpec((D_MODEL, tf), lambda i, f: (0, f + nf)),
            pl.BlockSpec((tf, D_MODEL), lambda i, f: (f, 0)),
        ],
        out_specs=pl.BlockSpec((tm, D_MODEL), lambda i, f: (i, 0)),
        out_shape=jax.ShapeDtypeStruct((N_TOK, D_MODEL), F32),
        scratch_shapes=[pltpu.VMEM((tm, D_MODEL), BF16), pltpu.VMEM((tm, D_MODEL), F32)],
        compiler_params=_cparams(2),
        name="ffn",
    )(x, mods3, mods3, mods3, g_row, w_in, w_in, w_out)


def _mm_mod_kernel(x_ref, shift_ref, scale_ref, g_ref, w_ref, o_ref, h_scr):
    @pl.when(pl.program_id(1) == 0)
    def _():
        h_scr[...] = _modulated(x_ref[...], g_ref[...], shift_ref[0], scale_ref[0]).astype(BF16)

    o_ref[...] = _dot(h_scr[...], w_ref[...]).astype(o_ref.dtype)


def _mm_mod(x, mods3, slot, g_row, w, tn, name):
    tm = 512
    n = w.shape[1]
    return pl.pallas_call(
        _mm_mod_kernel,
        grid=(N_TOK // tm, n // tn),
        in_specs=[
            pl.BlockSpec((tm, D_MODEL), lambda i, j: (i, 0)),
            _mod_spec(slot, tm), _mod_spec(slot + 1, tm),
            pl.BlockSpec((1, D_MODEL), lambda i, j: (0, 0)),
            pl.BlockSpec((D_MODEL, tn), lambda i, j: (0, j)),
        ],
        out_specs=pl.BlockSpec((tm, tn), lambda i, j: (i, j)),
        out_shape=jax.ShapeDtypeStruct((N_TOK, n), F32),
        scratch_shapes=[pltpu.VMEM((tm, D_MODEL), BF16)],
        compiler_params=_cparams(2),
        name=name,
    )(x, mods3, mods3, g_row, w)


def _mm_res_kernel(y_ref, w_ref, x_ref, gate_ref, o_ref):
    o_ref[...] = x_ref[...] + gate_ref[0] * _dot(y_ref[...], w_ref[...])


def _mm_res(y, w, x, mods3, slot, name):
    tm, tn = 512, 512
    k = y.shape[1]
    return pl.pallas_call(
        _mm_res_kernel,
        grid=(N_TOK // tm, D_MODEL // tn),
        in_specs=[
            pl.BlockSpec((tm, k), lambda i, j: (i, 0)),
            pl.BlockSpec((k, tn), lambda i, j: (0, j)),
            pl.BlockSpec((tm, tn), lambda i, j: (i, j)),
            pl.BlockSpec((1, 1, tn), lambda i, j, s=slot: (_mod_row(i, tm), 0, s * (D_MODEL // tn) + j)),
        ],
        out_specs=pl.BlockSpec((tm, tn), lambda i, j: (i, j)),
        out_shape=jax.ShapeDtypeStruct((N_TOK, D_MODEL), F32),
        compiler_params=_cparams(2),
        name=name,
    )(y, w, x, mods3)


def _mm_kernel(a_ref, w_ref, o_ref):
    o_ref[...] = _dot(a_ref[...], w_ref[...])


def _mm(a, w, name):
    tm, tn = 512, 512
    m, k = a.shape
    n = w.shape[1]
    return pl.pallas_call(
        _mm_kernel,
        grid=(m // tm, n // tn),
        in_specs=[pl.BlockSpec((tm, k), lambda i, j: (i, 0)), pl.BlockSpec((k, tn), lambda i, j: (0, j))],
        out_specs=pl.BlockSpec((tm, tn), lambda i, j: (i, j)),
        out_shape=jax.ShapeDtypeStruct((m, n), F32),
        compiler_params=_cparams(2),
        name=name,
    )(a, w)


def _rms_w(x, w):
    return x * lax.rsqrt(jnp.mean(x * x, axis=-1, keepdims=True) + EPS) * w


def _softmax_parts(scores, sink):
    m = sink
    for s in scores:
        m = jnp.maximum(jnp.max(s, axis=-1, keepdims=True), m)
    es = [jnp.exp(s - m) for s in scores]
    den = jnp.exp(sink - m)
    for e in es:
        den = den + jnp.sum(e, axis=-1, keepdims=True)
    return es, den


def _attn_ctx_kernel(sink_ref, q_ref, k_ref, v_ref, qn_ref, kn_ref, o_ref, ko_ref, vo_ref):
    kvh = pl.program_id(1)
    hd = ATTN_HEAD_DIM
    k = _rms_w(k_ref[...], kn_ref[...])
    v = v_ref[...]
    ko_ref[...] = k
    vo_ref[...] = v
    kb = k.astype(BF16)
    vb = v.astype(BF16)
    for g in range(ATTN_GROUP):
        q = _rms_w(q_ref[:, g * hd:(g + 1) * hd], qn_ref[...])
        s = _dot_nt(q.astype(BF16), kb) * hd ** -0.5
        sink = sink_ref[kvh * ATTN_GROUP + g]
        (e,), den = _softmax_parts([s], sink)
        o_ref[:, g * hd:(g + 1) * hd] = _dot((e / den).astype(BF16), vb).astype(BF16)


def _attn_ctx(qkv, sink, q_norm, k_norm, n_seq, t):
    hd = ATTN_HEAD_DIM
    gw = ATTN_GROUP * hd
    kcol = ATTN_HEADS * hd // hd
    vcol = kcol + ATTN_KV_HEADS
    n_tok = n_seq * t
    return pl.pallas_call(
        _attn_ctx_kernel,
        grid=(n_seq, ATTN_KV_HEADS),
        in_specs=[
            pl.BlockSpec(memory_space=pltpu.SMEM),
            pl.BlockSpec((t, gw), lambda b, h: (b, h)),
            pl.BlockSpec((t, hd), lambda b, h: (b, kcol + h)),
            pl.BlockSpec((t, hd), lambda b, h: (b, vcol + h)),
            pl.BlockSpec((1, hd), lambda b, h: (0, 0)),
            pl.BlockSpec((1, hd), lambda b, h: (0, 0)),
        ],
        out_specs=[
            pl.BlockSpec((t, gw), lambda b, h: (b, h)),
            pl.BlockSpec((t, hd), lambda b, h: (b, h)),
            pl.BlockSpec((t, hd), lambda b, h: (b, h)),
        ],
        out_shape=[
            jax.ShapeDtypeStruct((n_tok, ATTN_HEADS * hd), BF16),
            jax.ShapeDtypeStruct((n_tok, ATTN_KV_HEADS * hd), F32),
            jax.ShapeDtypeStruct((n_tok, ATTN_KV_HEADS * hd), F32),
        ],
        compiler_params=_cparams(2),
        name="attn_ctx",
    )(sink, qkv, qkv, qkv, q_norm[None, :], k_norm[None, :])


def _rope(x, cos, sin_a, sin_b):
    quarter = ATTN_HEAD_DIM // 4
    return x * cos + pltpu.roll(x, ATTN_HEAD_DIM - quarter, 1) * sin_a + pltpu.roll(x, quarter, 1) * sin_b


def _rope_tables(t):
    hd = ATTN_HEAD_DIM
    quarter = hd // 4
    pos = jnp.arange(t)
    row = (pos // GRID_W).astype(F32)
    col = (pos % GRID_W).astype(F32)
    inv_freq = ROPE_THETA ** (-jnp.arange(quarter, dtype=F32) / quarter)
    lane = jnp.arange(hd)
    ang = jnp.where(lane[None, :] < hd // 2, row[:, None], col[:, None]) * inv_freq[lane % quarter][None, :]
    first = (lane % (hd // 2)) < quarter
    sin = jnp.sin(ang)
    return jnp.cos(ang), jnp.where(first[None, :], -sin, 0.0), jnp.where(first[None, :], 0.0, sin)


def _attn_lat_kernel(sink_ref, q_ref, k_ref, v_ref, ck_ref, cv_ref, qn_ref, kn_ref, cos_ref, sa_ref, sb_ref,
                     o_ref, kb_scr, vb_scr):
    kvh = pl.program_id(1)
    hd = ATTN_HEAD_DIM
    t = q_ref.shape[0]
    span = ATTN_BLOCK + 2 * ATTN_WINDOW
    k = _rope(_rms_w(k_ref[...], kn_ref[...]), cos_ref[...], sa_ref[...], sb_ref[...])
    kb_scr[...] = k.astype(BF16)
    vb_scr[...] = v_ref[...].astype(BF16)
    ckb = ck_ref[0, 0].astype(BF16)
    cvb = cv_ref[0, 0].astype(BF16)
    q_iota = lax.broadcasted_iota(jnp.int32, (ATTN_BLOCK, span), 0)
    k_iota = lax.broadcasted_iota(jnp.int32, (ATTN_BLOCK, span), 1)

    def block(j, carry):
        q0 = pl.multiple_of(j * ATTN_BLOCK, ATTN_BLOCK)
        ws = pl.multiple_of(jnp.clip(q0 - ATTN_WINDOW, 0, t - span), ATTN_BLOCK)
        kw = kb_scr[pl.ds(ws, span), :]
        vw = vb_scr[pl.ds(ws, span), :]
        valid = jnp.abs((k_iota + ws) - (q_iota + q0)) <= ATTN_WINDOW
        cos = cos_ref[pl.ds(q0, ATTN_BLOCK), :]
        sa = sa_ref[pl.ds(q0, ATTN_BLOCK), :]
        sb = sb_ref[pl.ds(q0, ATTN_BLOCK), :]
        for g in range(ATTN_GROUP):
            q = _rope(_rms_w(q_ref[pl.ds(q0, ATTN_BLOCK), g * hd:(g + 1) * hd], qn_ref[...]), cos, sa, sb)
            qb = q.astype(BF16)
            s_win = jnp.where(valid, _dot_nt(qb, kw) * hd ** -0.5, NEG_INF)
            s_ctx = _dot_nt(qb, ckb) * hd ** -0.5
            sink = sink_ref[kvh * ATTN_GROUP + g]
            (e_win, e_ctx), den = _softmax_parts([s_win, s_ctx], sink)
            o = _dot((e_win / den).astype(BF16), vw) + _dot((e_ctx / den).astype(BF16), cvb)
            o_ref[pl.ds(q0, ATTN_BLOCK), g * hd:(g + 1) * hd] = o.astype(BF16)
        return carry

    lax.fori_loop(0, t // ATTN_BLOCK, block, 0)


def _attn_lat(qkv, cache_k, cache_v, slot, sink, q_norm, k_norm, n_seq, t, row_off):
    hd = ATTN_HEAD_DIM
    gw = ATTN_GROUP * hd
    kcol = ATTN_HEADS
    vcol = kcol + ATTN_KV_HEADS
    n_past = cache_k.shape[2]
    ck = cache_k.reshape(cache_k.shape[0], cache_k.shape[1], n_past, ATTN_KV_HEADS * hd)
    cv = cache_v.reshape(ck.shape)
    cos, sin_a, sin_b = _rope_tables(t)
    tab = pl.BlockSpec((t, hd), lambda b, h: (0, 0))
    return pl.pallas_call(
        _attn_lat_kernel,
        grid=(n_seq, ATTN_KV_HEADS),
        in_specs=[
            pl.BlockSpec(memory_space=pltpu.SMEM),
            pl.BlockSpec((t, gw), lambda b, h: (b + row_off, h)),
            pl.BlockSpec((t, hd), lambda b, h: (b + row_off, kcol + h)),
            pl.BlockSpec((t, hd), lambda b, h: (b + row_off, vcol + h)),
            pl.BlockSpec((1, 1, n_past, hd), lambda b, h: (b, slot, 0, h)),
            pl.BlockSpec((1, 1, n_past, hd), lambda b, h: (b, slot, 0, h)),
            pl.BlockSpec((1, hd), lambda b, h: (0, 0)),
            pl.BlockSpec((1, hd), lambda b, h: (0, 0)),
            tab, tab, tab,
        ],
        out_specs=pl.BlockSpec((t, gw), lambda b, h: (b, h)),
        out_shape=jax.ShapeDtypeStruct((n_seq * t, ATTN_HEADS * hd), BF16),
        scratch_shapes=[pltpu.VMEM((t, hd), BF16), pltpu.VMEM((t, hd), BF16)],
        compiler_params=_cparams(2),
        name="attn_lat",
    )(sink, qkv, qkv, qkv, ck, cv, q_norm[None, :], k_norm[None, :], cos, sin_a, sin_b)


def _cumsum_rows(x, reverse):
    n = x.shape[0]
    row = lax.broadcasted_iota(jnp.int32, x.shape, 0)
    sh = 1
    while sh < n:
        if reverse:
            x = x + jnp.where(row < n - sh, pltpu.roll(x, n - sh, 0), 0.0)
        else:
            x = x + jnp.where(row >= sh, pltpu.roll(x, sh, 0), 0.0)
        sh *= 2
    return x


def _log_sigmoid(x):
    return jnp.minimum(x, 0.0) - jnp.log(1.0 + jnp.exp(-jnp.abs(x)))


def _mlstm_kernel(has_init, emit_state, *refs):
    refs = list(refs)
    q_ref, k_ref, v_ref, og_ref, gt_ref, bg_ref, on_ref = refs[:7]
    refs = refs[7:]
    if has_init:
        c0_ref, n0_ref, m0_ref = refs[:3]
        refs = refs[3:]
    y_ref = refs[0]
    refs = refs[1:]
    if emit_state:
        co_ref, no_ref, mo_ref = refs[:3]
        refs = refs[3:]
    hf_scr, hb_scr = refs

    L = MLSTM_CHUNK
    t = q_ref.shape[0]
    nc = t // L
    dk, dv = MLSTM_DK, MLSTM_DV
    row = lax.broadcasted_iota(jnp.int32, (L, L), 0)
    col = lax.broadcasted_iota(jnp.int32, (L, L), 1)
    lane = lax.broadcasted_iota(jnp.int32, (L, LANES), 1)
    bias = bg_ref[0]

    def one_dir(z, c, C, n, m, out_scr):
        t0 = pl.multiple_of(c * L, L)
        qc = q_ref[pl.ds(t0, L), :] * dk ** -0.5
        kc = k_ref[pl.ds(t0, L), :]
        vc = v_ref[pl.ds(t0, L), :]
        G = gt_ref[pl.ds(t0, L), :] + bias
        CUM = _cumsum_rows(_log_sigmoid(G), reverse=(z == 1))
        ci, cf = 2 * z, 2 * z + 1
        ZT = jnp.where(lane == cf, CUM, G).T
        i_row, cum_row = ZT[ci:ci + 1, :L], ZT[cf:cf + 1, :L]
        i_col, cum_col = G[:, ci:ci + 1], CUM[:, cf:cf + 1]
        mask = (col <= row) if z == 0 else (col >= row)
        d = jnp.where(mask, cum_col - cum_row + i_row, NEG_INF)
        inter = cum_col + m
        m_c = jnp.maximum(inter, jnp.max(d, axis=-1, keepdims=True))
        qb, kb, vb = qc.astype(BF16), kc.astype(BF16), vc.astype(BF16)
        s = _dot_nt(qb, kb) * jnp.exp(d - m_c)
        a = jnp.exp(inter - m_c)
        num = _dot(s.astype(BF16), vb) + a * _dot(qb, C.astype(BF16))
        den = jnp.sum(s, axis=-1, keepdims=True) + a * jnp.sum(qc * n, axis=-1, keepdims=True)
        out_scr[pl.ds(t0, L), :] = num / jnp.maximum(jnp.abs(den), jnp.exp(-m_c))
        bl = cum_row[:, L - 1:L] if z == 0 else cum_row[:, 0:1]
        g_row = bl - cum_row + i_row
        m_new = jnp.maximum(bl + m, jnp.max(g_row, axis=-1, keepdims=True))
        decay = jnp.exp(bl + m - m_new)
        kw = kc * jnp.exp(bl - cum_col + i_col - m_new)
        C_new = decay * C + _dot_tn(kw.astype(BF16), vb)
        n_new = decay * n + jnp.sum(kw, axis=0, keepdims=True)
        return C_new, n_new, m_new

    def body(ci, carry):
        Cf, nf, mf, Cb, nb, mb = carry
        Cf, nf, mf = one_dir(0, ci, Cf, nf, mf, hf_scr)
        Cb, nb, mb = one_dir(1, nc - 1 - ci, Cb, nb, mb, hb_scr)
        return Cf, nf, mf, Cb, nb, mb

    if has_init:
        init = (c0_ref[0, 0, 0], n0_ref[0, 0, 0], m0_ref[0, 0, 0][:, 0:1],
                c0_ref[0, 1, 0], n0_ref[0, 1, 0], m0_ref[0, 1, 0][:, 0:1])
    else:
        init = (jnp.zeros((dk, dv), F32), jnp.zeros((1, dk), F32), jnp.zeros((1, 1), F32)) * 2
    Cf, nf, mf, Cb, nb, mb = lax.fori_loop(0, nc, body, init)

    hs = hf_scr[...] + hb_scr[...]
    y = _rms_w(hs, on_ref[...]) * _sigmoid(og_ref[...])
    y_ref[...] = y.astype(BF16)
    if emit_state:
        co_ref[0, 0, 0] = Cf
        co_ref[0, 1, 0] = Cb
        no_ref[0, 0, 0] = nf
        no_ref[0, 1, 0] = nb
        mo_ref[0, 0, 0] = jnp.broadcast_to(mf, (1, LANES))
        mo_ref[0, 1, 0] = jnp.broadcast_to(mb, (1, LANES))


def _mlstm(proj, gates, b_gate_h, out_norm, state0, n_seq, t, row_off, emit_state):
    h_, dk, dv = MLSTM_HEADS, MLSTM_DK, MLSTM_DV
    has_init = state0 is not None
    kcol = h_
    vcol = 2 * h_ * dk // dv
    ogcol = vcol + h_
    in_specs = [
        pl.BlockSpec((t, dk), lambda b, h: (b + row_off, h)),
        pl.BlockSpec((t, dk), lambda b, h: (b + row_off, kcol + h)),
        pl.BlockSpec((t, dv), lambda b, h: (b + row_off, vcol + h)),
        pl.BlockSpec((t, dv), lambda b, h: (b + row_off, ogcol + h)),
        pl.BlockSpec((t, LANES), lambda b, h: (b + row_off, h)),
        pl.BlockSpec((1, 1, LANES), lambda b, h: (h, 0, 0)),
        pl.BlockSpec((1, dv), lambda b, h: (0, h)),
    ]
    args = [proj, proj, proj, proj, gates, b_gate_h[:, None, :], out_norm[None, :]]
    if has_init:
        c0, n0, m0 = state0
        in_specs += [
            pl.BlockSpec((1, 2, 1, dk, dv), lambda b, h: (b, 0, h, 0, 0)),
            pl.BlockSpec((1, 2, 1, 1, dk), lambda b, h: (b, 0, h, 0, 0)),
            pl.BlockSpec((1, 2, 1, 1, LANES), lambda b, h: (b, 0, h, 0, 0)),
        ]
        args += [c0, n0.reshape(n_seq, 2, h_, 1, dk),
                 jnp.broadcast_to(m0[..., None, None], (n_seq, 2, h_, 1, LANES))]
    out_specs = [pl.BlockSpec((t, dv), lambda b, h: (b, h))]
    out_shape = [jax.ShapeDtypeStruct((n_seq * t, D_MODEL), BF16)]
    if emit_state:
        out_specs += [
            pl.BlockSpec((1, 2, 1, dk, dv), lambda b, h: (b, 0, h, 0, 0)),
            pl.BlockSpec((1, 2, 1, 1, dk), lambda b, h: (b, 0, h, 0, 0)),
            pl.BlockSpec((1, 2, 1, 1, LANES), lambda b, h: (b, 0, h, 0, 0)),
        ]
        out_shape += [
            jax.ShapeDtypeStruct((n_seq, 2, h_, dk, dv), F32),
            jax.ShapeDtypeStruct((n_seq, 2, h_, 1, dk), F32),
            jax.ShapeDtypeStruct((n_seq, 2, h_, 1, LANES), F32),
        ]
    return pl.pallas_call(
        functools.partial(_mlstm_kernel, has_init, emit_state),
        grid=(n_seq, h_),
        in_specs=in_specs,
        out_specs=out_specs,
        out_shape=out_shape,
        scratch_shapes=[pltpu.VMEM((t, dv), F32), pltpu.VMEM((t, dv), F32)],
        compiler_params=_cparams(2),
        name="mlstm",
    )(*args)


def _rwkv_mix_kernel(x_ref, xp_ref, xn_ref, shift_ref, scale_ref, g_ref, mu_ref, o_ref):
    i = pl.program_id(0)
    tm = x_ref.shape[0]
    g, sh, sc = g_ref[...], shift_ref[0], scale_ref[0]
    h = _modulated(x_ref[...], g, sh, sc)
    hp = _modulated(xp_ref[...], g, sh, sc)[7:8, :]
    hn = _modulated(xn_ref[...], g, sh, sc)[0:1, :]
    n_ctx_tiles = N_CTX_TOK // tm
    tiles_per_lat = DEC_SEQ // tm
    tile_in_seq = jnp.where(i < n_ctx_tiles, i % (SEQ // tm), (i - n_ctx_tiles) % tiles_per_lat)
    tiles_in_seq = jnp.where(i < n_ctx_tiles, SEQ // tm, tiles_per_lat)
    has_prev = (tile_in_seq > 0).astype(F32)
    has_next = (tile_in_seq < tiles_in_seq - 1).astype(F32)
    row = lax.broadcasted_iota(jnp.int32, h.shape, 0)
    prev = jnp.where(row == 0, hp * has_prev, pltpu.roll(h, 1, 0))
    nxt = jnp.where(row == tm - 1, hn * has_next, pltpu.roll(h, tm - 1, 0))
    cs = 0.5 * (prev + nxt) - h
    for p in range(o_ref.shape[0]):
        o_ref[p] = (h + cs * mu_ref[p:p + 1, :]).astype(BF16)


def _rwkv_mix(x, mods3, g_row, mu):
    tm = 256
    nb8 = tm // 8
    last = N_TOK // 8 - 1
    return pl.pallas_call(
        _rwkv_mix_kernel,
        grid=(N_TOK // tm,),
        in_specs=[
            pl.BlockSpec((tm, D_MODEL), lambda i: (i, 0)),
            pl.BlockSpec((8, D_MODEL), lambda i: (jnp.maximum(i * nb8 - 1, 0), 0)),
            pl.BlockSpec((8, D_MODEL), lambda i: (jnp.minimum((i + 1) * nb8, last), 0)),
            pl.BlockSpec((1, 1, D_MODEL), lambda i: (_mod_row(i, tm), 0, 3)),
            pl.BlockSpec((1, 1, D_MODEL), lambda i: (_mod_row(i, tm), 0, 4)),
            pl.BlockSpec((1, D_MODEL), lambda i: (0, 0)),
            pl.BlockSpec((6, D_MODEL), lambda i: (0, 0)),
        ],
        out_specs=pl.BlockSpec((6, tm, D_MODEL), lambda i: (0, i, 0)),
        out_shape=jax.ShapeDtypeStruct((6, N_TOK, D_MODEL), BF16),
        compiler_params=_cparams(1),
        name="rwkv_mix",
    )(x, x, x, mods3, mods3, g_row, mu)


def _lora_kernel(act, post, x_ref, a_ref, b_ref, bias_ref, o_ref):
    u = _dot(x_ref[0], a_ref[0])
    if act == "tanh":
        u = jnp.tanh(u)
    elif act == "sigmoid":
        u = _sigmoid(u)
    y = _dot(u.astype(BF16), b_ref[0]) + bias_ref[0]
    if post == "log_decay":
        sp = jnp.maximum(-y, 0.0) + jnp.log(1.0 + jnp.exp(-jnp.abs(y)))
        y = -jnp.exp(-sp - 0.5)
    elif post == "sigmoid":
        y = _sigmoid(y)
    o_ref[0] = y


def _lora(xs, p, wa, wb, bias, act, post, name):
    tm = 512
    nz, _, r = wa.shape
    return pl.pallas_call(
        functools.partial(_lora_kernel, act, post),
        grid=(nz, N_TOK // tm),
        in_specs=[
            pl.BlockSpec((1, tm, D_MODEL), lambda z, i: (p, i, 0)),
            pl.BlockSpec((1, D_MODEL, r), lambda z, i: (z, 0, 0)),
            pl.BlockSpec((1, r, D_MODEL), lambda z, i: (z, 0, 0)),
            pl.BlockSpec((1, 1, D_MODEL), lambda z, i: (z, 0, 0)),
        ],
        out_specs=pl.BlockSpec((1, tm, D_MODEL), lambda z, i: (z, i, 0)),
        out_shape=jax.ShapeDtypeStruct((nz, N_TOK, D_MODEL), F32),
        compiler_params=_cparams(2),
        name=name,
    )(xs, wa, wb, bias)


def _rkv_kernel(x_ref, w_ref, o_ref):
    o_ref[0] = _dot(x_ref[0], w_ref[0])


def _rkv(xs, w_rkv):
    tm, tn = 512, 512
    return pl.pallas_call(
        _rkv_kernel,
        grid=(3, N_TOK // tm, D_MODEL // tn),
        in_specs=[pl.BlockSpec((1, tm, D_MODEL), lambda p, i, j: (p, i, 0)),
                  pl.BlockSpec((1, D_MODEL, tn), lambda p, i, j: (p, 0, j))],
        out_specs=pl.BlockSpec((1, tm, tn), lambda p, i, j: (p, i, j)),
        out_shape=jax.ShapeDtypeStruct((3, N_TOK, D_MODEL), F32),
        compiler_params=_cparams(3),
        name="rwkv_rkv",
    )(xs, w_rkv)


def _head_sum(x, first_head):
    s0 = jnp.sum(jnp.where(first_head, x, 0.0), axis=-1, keepdims=True)
    s1 = jnp.sum(jnp.where(first_head, 0.0, x), axis=-1, keepdims=True)
    return jnp.where(first_head, s0, s1)


def _rwkv_kernel(has_init, emit_state, *refs):
    refs = list(refs)
    r_ref, k_ref, v_ref, lw_ref, a_ref, g_ref, kk_w_ref, ka_ref, rk_ref, lng_ref, lnb_ref = refs[:11]
    refs = refs[11:]
    if has_init:
        s0_ref = refs[0]
        refs = refs[1:]
    y_ref = refs[0]
    refs = refs[1:]
    if emit_state:
        so_ref = refs[0]
        refs = refs[1:]
    kk_scr, yf_scr, yb_scr = refs

    L = RWKV_CHUNK
    N = RWKV_HEAD
    t = r_ref.shape[1]
    nc = t // L
    row = lax.broadcasted_iota(jnp.int32, (L, L), 0)
    col = lax.broadcasted_iota(jnp.int32, (L, L), 1)
    first_head = lax.broadcasted_iota(jnp.int32, (L, LANES), 1) < N
    sq_row = lax.broadcasted_iota(jnp.int32, (LANES, LANES), 0)
    sq_col = lax.broadcasted_iota(jnp.int32, (LANES, LANES), 1)
    same_head = (sq_row < N) == (sq_col < N)

    first_head_t = lax.broadcasted_iota(jnp.int32, (t, LANES), 1) < N
    kkx = k_ref[0] * kk_w_ref[...]
    kk_scr[...] = kkx / jnp.maximum(jnp.sqrt(_head_sum(kkx * kkx, first_head_t)), 1e-12)

    def one_dir(z, c, S):
        t0 = pl.multiple_of(c * L, L)
        sl = pl.ds(t0, L)
        r_, k_, v_ = r_ref[0, sl, :], k_ref[0, sl, :], v_ref[0, sl, :]
        lw, a_ = lw_ref[z, sl, :], a_ref[z, sl, :]
        kk = kk_scr[sl, :]
        kd = k_ * (1.0 + (a_ - 1.0) * ka_ref[...])
        b_ = kk * a_
        cum = _cumsum_rows(lw, reverse=(z == 1))
        cum_end = cum[L - 1:L, :] if z == 0 else cum[0:1, :]
        kp = (kk * jnp.exp(cum - lw)).astype(BF16)
        rp = (r_ * jnp.exp(cum)).astype(BF16)
        inv = jnp.exp(-cum)
        kt = (kd * inv).astype(BF16)
        bt = (b_ * inv).astype(BF16)
        to_end = jnp.exp(cum_end - cum)
        k_end = kd * to_end
        b_end = b_ * to_end
        strict = (col < row) if z == 0 else (col > row)
        incl = (col <= row) if z == 0 else (col >= row)
        Sb = S.astype(BF16)
        vb = v_.astype(BF16)
        zero = jnp.zeros_like(kp)

        g1v, n_h, g3, g4 = [], [], [], []
        for hh in range(2):
            hm = first_head if hh == 0 else jnp.logical_not(first_head)
            probe = jnp.concatenate([jnp.where(hm, kp, zero), jnp.where(hm, rp, zero)], axis=0)
            gk = _dot_nt(probe, kt)
            gb = _dot_nt(probe, bt)
            g1 = jnp.where(strict, gk[:L], 0.0).astype(BF16)
            g1v.append(_dot(g1, vb))
            n_h.append(jnp.where(strict, gb[:L], 0.0).astype(BF16))
            g3.append(jnp.where(incl, gk[L:], 0.0).astype(BF16))
            g4.append(jnp.where(incl, gb[L:], 0.0).astype(BF16))

        def per_head(mats, x):
            xb = x.astype(BF16)
            return jnp.where(first_head, _dot(mats[0], xb), _dot(mats[1], xb))

        rhs = _dot_nt(kp, Sb) + jnp.where(first_head, g1v[0], g1v[1])
        u = rhs - per_head(n_h, rhs)
        p = 1
        while 2 * p < L:
            n_h = [_dot(m, m).astype(BF16) for m in n_h]
            u = u + per_head(n_h, u)
            p *= 2
        ub = u.astype(BF16)
        y = _dot_nt(rp, Sb) + jnp.where(first_head,
                                        _dot(g3[0], vb) - _dot(g4[0], ub),
                                        _dot(g3[1], vb) - _dot(g4[1], ub))
        vu = jnp.concatenate([vb, ub], axis=0)
        kb_end = jnp.concatenate([k_end, -b_end], axis=0).astype(BF16)
        S_new = S * jnp.exp(cum_end) + jnp.where(same_head, _dot_tn(vu, kb_end), 0.0)
        return S_new, y

    def body(ci, carry):
        Sf, Sbk = carry
        Sf, yf = one_dir(0, ci, Sf)
        yf_scr[pl.ds(pl.multiple_of(ci * L, L), L), :] = yf
        cb = nc - 1 - ci
        Sbk, yb = one_dir(1, cb, Sbk)
        yb_scr[pl.ds(pl.multiple_of(cb * L, L), L), :] = yb
        return Sf, Sbk

    if has_init:
        def block_diag(s):
            return jnp.where(sq_row < N, s, pltpu.roll(s, N, 1))
        init = (block_diag(s0_ref[0, 0, 0]), block_diag(s0_ref[0, 1, 0]))
    else:
        init = (jnp.zeros((LANES, LANES), F32),) * 2
    Sf, Sbk = lax.fori_loop(0, nc, body, init)

    r_, k_, v_ = r_ref[0], k_ref[0], v_ref[0]
    y = yf_scr[...] + yb_scr[...]
    mean = _head_sum(y, first_head_t) * (1.0 / N)
    yc = y - mean
    var = _head_sum(yc * yc, first_head_t) * (1.0 / N)
    yn = yc * lax.rsqrt(var + RWKV_GN_EPS) * lng_ref[...] + lnb_ref[...]
    ka = ka_ref[...]
    kd_sum = k_ * (1.0 + (a_ref[0] - 1.0) * ka) + k_ * (1.0 + (a_ref[1] - 1.0) * ka)
    bonus = _head_sum(r_ * kd_sum * rk_ref[...], first_head_t) * v_
    y_ref[...] = ((yn + bonus) * g_ref[...]).astype(BF16)
    if emit_state:
        for z, S in enumerate((Sf, Sbk)):
            so_ref[0, z, 0] = (S + pltpu.roll(S, N, 1))[:, :N]


def _rwkv(rkv, lw, a, g, k_k, k_a, r_k, ln_g, ln_b, s0, n_seq, t, row_off, emit_state):
    has_init = s0 is not None
    tok = lambda lead: pl.BlockSpec((lead, t, LANES), lambda b, p: (0, b + row_off, p))
    par = pl.BlockSpec((1, LANES), lambda b, p: (0, p))
    in_specs = [
        pl.BlockSpec((1, t, LANES), lambda b, p: (0, b + row_off, p)),
        pl.BlockSpec((1, t, LANES), lambda b, p: (1, b + row_off, p)),
        pl.BlockSpec((1, t, LANES), lambda b, p: (2, b + row_off, p)),
        tok(2), tok(2),
        pl.BlockSpec((t, LANES), lambda b, p: (b + row_off, p)),
        par, par, par, par, par,
    ]
    args = [rkv, rkv, rkv, lw, a, g, k_k[None, :], k_a[None, :], r_k.reshape(1, D_MODEL), ln_g[None, :], ln_b[None, :]]
    if has_init:
        in_specs.append(pl.BlockSpec((1, 2, 1, LANES, LANES), lambda b, p: (b, 0, p, 0, 0)))
        args.append(s0)
    out_specs = [pl.BlockSpec((t, LANES), lambda b, p: (b, p))]
    out_shape = [jax.ShapeDtypeStruct((n_seq * t, D_MODEL), BF16)]
    if emit_state:
        out_specs.append(pl.BlockSpec((1, 2, 1, LANES, RWKV_HEAD), lambda b, p: (b, 0, p, 0, 0)))
        out_shape.append(jax.ShapeDtypeStruct((n_seq, 2, RWKV_PAIRS, LANES, RWKV_HEAD), F32))
    return pl.pallas_call(
        functools.partial(_rwkv_kernel, has_init, emit_state),
        grid=(n_seq, RWKV_PAIRS),
        in_specs=in_specs,
        out_specs=out_specs,
        out_shape=out_shape,
        scratch_shapes=[pltpu.VMEM((t, LANES), F32)] * 3,
        compiler_params=_cparams(2),
        name="rwkv",
    )(*args)


def _attn_layer(x, mods3, g_row, cache_k, cache_v, slot, w_qkv, q_norm, k_norm, sink, w_o):
    qkv = _mm_mod(x, mods3, 3, g_row, w_qkv.astype(BF16), 512, "attn_qkv")
    o_ctx, k_ctx, v_ctx = _attn_ctx(qkv, sink, q_norm, k_norm, BATCH, SEQ)
    o_lat = _attn_lat(qkv, cache_k, cache_v, slot, sink, q_norm, k_norm, DEC_BATCH, DEC_SEQ, N_CTX_TOK // DEC_SEQ)
    o = jnp.concatenate([o_ctx, o_lat], axis=0)
    x = _mm_res(o, w_o.astype(BF16), x, mods3, 5, "attn_out")
    shape = (BATCH, SEQ, ATTN_KV_HEADS, ATTN_HEAD_DIM)
    return x, k_ctx.reshape(shape), v_ctx.reshape(shape)


def _mlstm_layer(x, mods3, g_row, state_lat, w_in, w_gate, b_gate, out_norm, w_o):
    h_ = MLSTM_HEADS
    proj = _mm_mod(x, mods3, 3, g_row, w_in.astype(BF16), 512, "mlstm_in")
    wg = jnp.transpose(w_gate.reshape(D_MODEL, 4, h_), (0, 2, 1))
    wg = jnp.pad(wg, ((0, 0), (0, 0), (0, LANES - 4))).reshape(D_MODEL, h_ * LANES).astype(BF16)
    bg = jnp.pad(b_gate.reshape(4, h_).T, ((0, 0), (0, LANES - 4)))
    gates = _mm_mod(x, mods3, 3, g_row, wg, 512, "mlstm_gates")
    y_ctx, C, n, m = _mlstm(proj, gates, bg, out_norm, None, BATCH, SEQ, 0, True)
    (y_lat,) = _mlstm(proj, gates, bg, out_norm, state_lat, DEC_BATCH, DEC_SEQ, N_CTX_TOK // DEC_SEQ, False)
    y = jnp.concatenate([y_ctx, y_lat], axis=0)
    x = _mm_res(y, w_o.astype(BF16), x, mods3, 5, "mlstm_out")
    return x, C, n[:, :, :, 0, :], m[:, :, :, 0, 0]


def _rwkv_layer(x, mods3, g_row, s0_lat, mu, w_rkv, w0, wA, wB, a0, aA, aB, gA, gB, k_k, k_a, r_k, ln_g, ln_b, w_o):
    xs = _rwkv_mix(x, mods3, g_row, mu)
    rkv = _rkv(xs, w_rkv.astype(BF16))
    pad_r = lambda w, axis: jnp.pad(w, [(0, 0) if ax != axis else (0, LANES - w.shape[axis]) for ax in range(w.ndim)])
    lw = _lora(xs, 3, pad_r(wA, 2).astype(BF16), pad_r(wB, 1).astype(BF16), w0[:, None, :], "tanh", "log_decay", "rwkv_w")
    a = _lora(xs, 4, pad_r(aA, 2).astype(BF16), pad_r(aB, 1).astype(BF16), a0[:, None, :], "none", "sigmoid", "rwkv_a")
    g = _lora(xs, 5, gA[None].astype(BF16), gB[None].astype(BF16), jnp.zeros((1, 1, D_MODEL), F32), "sigmoid", "none", "rwkv_g")[0]
    s0 = jnp.pad(s0_lat.reshape(DEC_BATCH, 2, RWKV_PAIRS, LANES, RWKV_HEAD), ((0, 0),) * 4 + ((0, LANES - RWKV_HEAD),))
    y_ctx, S = _rwkv(rkv, lw, a, g, k_k, k_a, r_k, ln_g, ln_b, None, BATCH, SEQ, 0, True)
    (y_lat,) = _rwkv(rkv, lw, a, g, k_k, k_a, r_k, ln_g, ln_b, s0, DEC_BATCH, DEC_SEQ, N_CTX_TOK // DEC_SEQ, False)
    y = jnp.concatenate([y_ctx, y_lat], axis=0)
    x = _mm_res(y, w_o.astype(BF16), x, mods3, 5, "rwkv_out")
    return x, S.reshape(BATCH, 2, RWKV_HEADS, RWKV_HEAD, RWKV_HEAD)


def kernel(x_prompt, x_sample, cache_k, cache_v, state_mlstm_C, state_mlstm_n, state_mlstm_m, state_rwkv, c, c_ctx, mod_w, mod_b, norm_g, ffn_w_in, ffn_w_out, attn_w_qkv, attn_q_norm, attn_k_norm, attn_sink, attn_w_o, mlstm_w_in, mlstm_w_gate, mlstm_b_gate, mlstm_out_norm, mlstm_w_o, rwkv_mu, rwkv_w_rkv, rwkv_w0, rwkv_wA, rwkv_wB, rwkv_a0, rwkv_aA, rwkv_aB, rwkv_gA, rwkv_gB, rwkv_k_k, rwkv_k_a, rwkv_r_k, rwkv_ln_g, rwkv_ln_b, rwkv_w_o):
    x = jnp.concatenate([x_prompt.reshape(N_CTX_TOK, D_MODEL), x_sample.reshape(N_LAT_TOK, D_MODEL)], axis=0)
    cond8 = jnp.concatenate([c_ctx[None, :], c, jnp.zeros((MOD_ROWS - 1 - DEC_BATCH, D_MODEL), F32)], axis=0)
    mods = _adaln_all(cond8, mod_w, mod_b)
    ffn_w_in_b = ffn_w_in.astype(BF16)
    ffn_w_out_b = ffn_w_out.astype(BF16)

    new_k, new_v, new_C, new_n, new_m, new_S = [], [], [], [], [], []
    for i in range(DEPTH):
        kind, slot = i % N_MIXERS, i // N_MIXERS
        mods3 = mods[i].reshape(MOD_ROWS, 1, N_MOD * D_MODEL)
        x = _ffn(x, mods3, 0, norm_g[i, 0][None, :], ffn_w_in_b[i, 0], ffn_w_out_b[i, 0])
        g_row = norm_g[i, 1][None, :]
        if kind == 0:
            x, k_ctx, v_ctx = _attn_layer(x, mods3, g_row, cache_k, cache_v, slot, attn_w_qkv[slot], attn_q_norm[slot],
                                          attn_k_norm[slot], attn_sink[slot], attn_w_o[slot])
            new_k.append(k_ctx)
            new_v.append(v_ctx)
        elif kind == 1:
            state_lat = (state_mlstm_C[:, slot], state_mlstm_n[:, slot], state_mlstm_m[:, slot])
            x, C, n, m = _mlstm_layer(x, mods3, g_row, state_lat, mlstm_w_in[slot], mlstm_w_gate[slot], mlstm_b_gate[slot],
                                      mlstm_out_norm[slot], mlstm_w_o[slot])
            new_C.append(C)
            new_n.append(n)
            new_m.append(m)
        else:
            x, S = _rwkv_layer(x, mods3, g_row, state_rwkv[:, slot], rwkv_mu[slot], rwkv_w_rkv[slot], rwkv_w0[slot],
                               rwkv_wA[slot], rwkv_wB[slot], rwkv_a0[slot], rwkv_aA[slot], rwkv_aB[slot], rwkv_gA[slot],
                               rwkv_gB[slot], rwkv_k_k[slot], rwkv_k_a[slot], rwkv_r_k[slot], rwkv_ln_g[slot],
                               rwkv_ln_b[slot], rwkv_w_o[slot])
            new_S.append(S)
        x = _ffn(x, mods3, 6, norm_g[i, 2][None, :], ffn_w_in_b[i, 1], ffn_w_out_b[i, 1])

    y_prompt = x[:N_CTX_TOK].reshape(BATCH, SEQ, D_MODEL)
    y_sample = x[N_CTX_TOK:].reshape(DEC_BATCH, DEC_SEQ, D_MODEL)
    return (y_prompt, y_sample, jnp.stack(new_k, axis=1), jnp.stack(new_v, axis=1),
            jnp.stack(new_C, axis=1), jnp.stack(new_n, axis=1), jnp.stack(new_m, axis=1), jnp.stack(new_S, axis=1))
```

```python
import functools

import jax
import jax.numpy as jnp
from jax import lax
from jax.experimental import pallas as pl
from jax.experimental.pallas import tpu as pltpu

D_MODEL = 2048
BATCH = 32
SEQ = 256
DEPTH = 4
DEC_BATCH = 4
DEC_SEQ = 1024
PAST_LEN = 256
GRID_W = 64
N_MIXERS = 3
N_MOD = 9
D_FF = 5632
EPS = 1e-6

ATTN_HEADS = 16
ATTN_KV_HEADS = 4
ATTN_GROUP = ATTN_HEADS // ATTN_KV_HEADS
ATTN_HEAD_DIM = D_MODEL // ATTN_HEADS
ATTN_WINDOW = 128
ATTN_BLOCK = 128
ROPE_THETA = 10000.0
QKV_DIM = (ATTN_HEADS + 2 * ATTN_KV_HEADS) * ATTN_HEAD_DIM

MLSTM_HEADS = 8
MLSTM_DV = D_MODEL // MLSTM_HEADS
MLSTM_DK = MLSTM_DV // 2
MLSTM_CHUNK = 64
MLSTM_IN_DIM = 2 * MLSTM_HEADS * MLSTM_DK + 2 * D_MODEL

RWKV_HEAD = 64
RWKV_HEADS = D_MODEL // RWKV_HEAD
RWKV_PAIRS = RWKV_HEADS // 2
RWKV_CHUNK = 64
RWKV_GN_EPS = 64e-5

N_CTX_TOK = BATCH * SEQ
N_LAT_TOK = DEC_BATCH * DEC_SEQ
N_TOK = N_CTX_TOK + N_LAT_TOK
MOD_ROWS = 8
LANES = 128

VMEM_LIMIT = 52 * 1024 * 1024
BF16 = jnp.bfloat16
F32 = jnp.float32
NEG_INF = float("-inf")


def _cparams(n_axes):
    return pltpu.CompilerParams(dimension_semantics=("arbitrary",) * n_axes, vmem_limit_bytes=VMEM_LIMIT)


def _mod_row(i, tm):
    n_ctx_tiles = N_CTX_TOK // tm
    tiles_per_lat = DEC_SEQ // tm
    return jnp.where(i < n_ctx_tiles, 0, 1 + (i - n_ctx_tiles) // tiles_per_lat)


def _mod_spec(slot, tm):
    return pl.BlockSpec((1, 1, D_MODEL), lambda i, j, s=slot: (_mod_row(i, tm), 0, s))


def _dot(a, b):
    return jnp.dot(a, b, preferred_element_type=F32)


def _dot_nt(a, b):
    return lax.dot_general(a, b, (((1,), (1,)), ((), ())), preferred_element_type=F32)


def _dot_tn(a, b):
    return lax.dot_general(a, b, (((0,), (0,)), ((), ())), preferred_element_type=F32)


def _sigmoid(x):
    return jax.nn.sigmoid(x)


def _modulated(x, g, shift, scale):
    ms = jnp.mean(x * x, axis=-1, keepdims=True)
    y = x * lax.rsqrt(ms + EPS) * g
    return y * (1.0 + scale) + shift


def _adaln_kernel(cond_ref, w_ref, b_ref, o_ref):
    c = cond_ref[...]
    a = (c * _sigmoid(c)).astype(BF16)
    o_ref[0] = _dot(a, w_ref[0].astype(BF16)) + b_ref[0]


def _adaln_all(cond8, mod_w, mod_b):
    tn = 1024
    n_out = N_MOD * D_MODEL
    return pl.pallas_call(
        _adaln_kernel,
        grid=(DEPTH, n_out // tn),
        in_specs=[
            pl.BlockSpec((MOD_ROWS, D_MODEL), lambda l, j: (0, 0)),
            pl.BlockSpec((1, D_MODEL, tn), lambda l, j: (l, 0, j)),
            pl.BlockSpec((1, 1, tn), lambda l, j: (l, 0, j)),
        ],
        out_specs=pl.BlockSpec((1, MOD_ROWS, tn), lambda l, j: (l, 0, j)),
        out_shape=jax.ShapeDtypeStruct((DEPTH, MOD_ROWS, n_out), F32),
        compiler_params=_cparams(2),
        name="adaln",
    )(cond8, mod_w, mod_b.reshape(DEPTH, 1, n_out))


def _ffn_kernel(x_ref, shift_ref, scale_ref, gate_ref, g_ref, wg_ref, wu_ref, wo_ref, o_ref, h_scr, acc_scr):
    f = pl.program_id(1)

    @pl.when(f == 0)
    def _():
        h = _modulated(x_ref[...], g_ref[...], shift_ref[0], scale_ref[0])
        h_scr[...] = h.astype(BF16)
        acc_scr[...] = jnp.zeros_like(acc_scr)

    h = h_scr[...]
    a = _dot(h, wg_ref[...])
    b = _dot(h, wu_ref[...])
    act = (a * _sigmoid(a)) * b
    acc_scr[...] += _dot(act.astype(BF16), wo_ref[...])

    @pl.when(f == pl.num_programs(1) - 1)
    def _():
        o_ref[...] = x_ref[...] + (0.5 * gate_ref[0]) * acc_scr[...]


def _ffn(x, mods3, slot, g_row, w_in, w_out):
    tm, tf = 512, 512
    nf = D_FF // tf
    return pl.pallas_call(
        _ffn_kernel,
        grid=(N_TOK // tm, nf),
        in_specs=[
            pl.BlockSpec((tm, D_MODEL), lambda i, f: (i, 0)),
            _mod_spec(slot, tm), _mod_spec(slot + 1, tm), _mod_spec(slot + 2, tm),
            pl.BlockSpec((1, D_MODEL), lambda i, f: (0, 0)),
            pl.BlockSpec((D_MODEL, tf), lambda i, f: (0, f)),
            pl.BlockSpec((D_MODEL, tf), lambda i, f: (0, f + nf)),
            pl.BlockSpec((tf, D_MODEL), lambda i, f: (f, 0)),
        ],
        out_specs=pl.BlockSpec((tm, D_MODEL), lambda i, f: (i, 0)),
        out_shape=jax.ShapeDtypeStruct((N_TOK, D_MODEL), F32),
        scratch_shapes=[pltpu.VMEM((tm, D_MODEL), BF16), pltpu.VMEM((tm, D_MODEL), F32)],
        compiler_params=_cparams(2),
        name="ffn",
    )(x, mods3, mods3, mods3, g_row, w_in, w_in, w_out)


MM_COLS = 512


def _resident(shape):
    return pl.BlockSpec(shape, lambda *_: (0,) * len(shape), pipeline_mode=pl.Buffered(1))


def _mm_mod_kernel(x_ref, shift_ref, scale_ref, g_ref, w_ref, o_ref):
    h = _modulated(x_ref[...], g_ref[...], shift_ref[0], scale_ref[0]).astype(BF16)
    for c in range(o_ref.shape[1] // MM_COLS):
        cols = slice(c * MM_COLS, (c + 1) * MM_COLS)
        o_ref[:, cols] = _dot(h, w_ref[:, cols])


def _mm_mod(x, mods3, slot, g_row, w, tm, name):
    n = w.shape[1]
    mod = lambda s: pl.BlockSpec((1, 1, D_MODEL), lambda i: (_mod_row(i, tm), 0, s))
    return pl.pallas_call(
        _mm_mod_kernel,
        grid=(N_TOK // tm,),
        in_specs=[
            pl.BlockSpec((tm, D_MODEL), lambda i: (i, 0)),
            mod(slot), mod(slot + 1),
            _resident((1, D_MODEL)),
            _resident((D_MODEL, n)),
        ],
        out_specs=pl.BlockSpec((tm, n), lambda i: (i, 0)),
        out_shape=jax.ShapeDtypeStruct((N_TOK, n), F32),
        compiler_params=_cparams(1),
        name=name,
    )(x, mods3, mods3, g_row, w)


def _mm_res_kernel(y_ref, w_ref, x_ref, gate_ref, o_ref):
    y = y_ref[...]
    for c in range(o_ref.shape[1] // MM_COLS):
        cols = slice(c * MM_COLS, (c + 1) * MM_COLS)
        o_ref[:, cols] = x_ref[:, cols] + gate_ref[0][:, cols] * _dot(y, w_ref[:, cols])


def _mm_res(y, w, x, mods3, slot, name):
    tm = 512
    k = y.shape[1]
    return pl.pallas_call(
        _mm_res_kernel,
        grid=(N_TOK // tm,),
        in_specs=[
            pl.BlockSpec((tm, k), lambda i: (i, 0)),
            _resident((k, D_MODEL)),
            pl.BlockSpec((tm, D_MODEL), lambda i: (i, 0)),
            pl.BlockSpec((1, 1, D_MODEL), lambda i: (_mod_row(i, tm), 0, slot)),
        ],
        out_specs=pl.BlockSpec((tm, D_MODEL), lambda i: (i, 0)),
        out_shape=jax.ShapeDtypeStruct((N_TOK, D_MODEL), F32),
        compiler_params=_cparams(1),
        name=name,
    )(y, w, x, mods3)


def _rms_w(x, w):
    return x * lax.rsqrt(jnp.mean(x * x, axis=-1, keepdims=True) + EPS) * w


def _softmax_parts(scores, sink):
    m = sink
    for s in scores:
        m = jnp.maximum(jnp.max(s, axis=-1, keepdims=True), m)
    es = [jnp.exp(s - m) for s in scores]
    den = jnp.exp(sink - m)
    for e in es:
        den = den + jnp.sum(e, axis=-1, keepdims=True)
    return es, den


def _attn_ctx_kernel(sink_ref, q_ref, k_ref, v_ref, qn_ref, kn_ref, o_ref, ko_ref, vo_ref):
    kvh = pl.program_id(1)
    hd = ATTN_HEAD_DIM
    k = _rms_w(k_ref[...], kn_ref[...])
    v = v_ref[...]
    ko_ref[...] = k
    vo_ref[...] = v
    kb = k.astype(BF16)
    vb = v.astype(BF16)
    qs = [_rms_w(q_ref[:, g * hd:(g + 1) * hd], qn_ref[...]).astype(BF16) for g in range(ATTN_GROUP)]
    ss = [_dot_nt(q, kb) * hd ** -0.5 for q in qs]
    ps = []
    for g, s in enumerate(ss):
        (e,), den = _softmax_parts([s], sink_ref[kvh * ATTN_GROUP + g])
        ps.append((e / den).astype(BF16))
    os_ = [_dot(p, vb) for p in ps]
    for g, o in enumerate(os_):
        o_ref[:, g * hd:(g + 1) * hd] = o.astype(BF16)


def _attn_ctx(qkv, sink, q_norm, k_norm, n_seq, t):
    hd = ATTN_HEAD_DIM
    gw = ATTN_GROUP * hd
    kcol = ATTN_HEADS * hd // hd
    vcol = kcol + ATTN_KV_HEADS
    n_tok = n_seq * t
    return pl.pallas_call(
        _attn_ctx_kernel,
        grid=(n_seq, ATTN_KV_HEADS),
        in_specs=[
            pl.BlockSpec(memory_space=pltpu.SMEM),
            pl.BlockSpec((t, gw), lambda b, h: (b, h)),
            pl.BlockSpec((t, hd), lambda b, h: (b, kcol + h)),
            pl.BlockSpec((t, hd), lambda b, h: (b, vcol + h)),
            pl.BlockSpec((1, hd), lambda b, h: (0, 0)),
            pl.BlockSpec((1, hd), lambda b, h: (0, 0)),
        ],
        out_specs=[
            pl.BlockSpec((t, gw), lambda b, h: (b, h)),
            pl.BlockSpec((t, hd), lambda b, h: (b, h)),
            pl.BlockSpec((t, hd), lambda b, h: (b, h)),
        ],
        out_shape=[
            jax.ShapeDtypeStruct((N_TOK, ATTN_HEADS * hd), BF16),
            jax.ShapeDtypeStruct((n_tok, ATTN_KV_HEADS * hd), F32),
            jax.ShapeDtypeStruct((n_tok, ATTN_KV_HEADS * hd), F32),
        ],
        compiler_params=_cparams(2),
        name="attn_ctx",
    )(sink, qkv, qkv, qkv, q_norm[None, :], k_norm[None, :])


def _rope(x, cos, sin_a, sin_b):
    quarter = ATTN_HEAD_DIM // 4
    return x * cos + pltpu.roll(x, ATTN_HEAD_DIM - quarter, 1) * sin_a + pltpu.roll(x, quarter, 1) * sin_b


def _rope_tables(t):
    hd = ATTN_HEAD_DIM
    quarter = hd // 4
    pos = jnp.arange(t)
    row = (pos // GRID_W).astype(F32)
    col = (pos % GRID_W).astype(F32)
    inv_freq = ROPE_THETA ** (-jnp.arange(quarter, dtype=F32) / quarter)
    lane = jnp.arange(hd)
    ang = jnp.where(lane[None, :] < hd // 2, row[:, None], col[:, None]) * inv_freq[lane % quarter][None, :]
    first = (lane % (hd // 2)) < quarter
    sin = jnp.sin(ang)
    return jnp.cos(ang), jnp.where(first[None, :], -sin, 0.0), jnp.where(first[None, :], 0.0, sin)


def _attn_lat_kernel(sink_ref, q_ref, k_ref, v_ref, ck_ref, cv_ref, qn_ref, kn_ref, cos_ref, sa_ref, sb_ref, _y_ctx,
                     o_ref, kb_scr, vb_scr):
    kvh = pl.program_id(1)
    hd = ATTN_HEAD_DIM
    t = q_ref.shape[0]
    span = ATTN_BLOCK + 2 * ATTN_WINDOW
    k = _rope(_rms_w(k_ref[...], kn_ref[...]), cos_ref[...], sa_ref[...], sb_ref[...])
    kb_scr[...] = k.astype(BF16)
    vb_scr[...] = v_ref[...].astype(BF16)
    ckb = ck_ref[0, 0].astype(BF16)
    cvb = cv_ref[0, 0].astype(BF16)
    q_iota = lax.broadcasted_iota(jnp.int32, (ATTN_BLOCK, span), 0)
    k_iota = lax.broadcasted_iota(jnp.int32, (ATTN_BLOCK, span), 1)

    def block(j, carry):
        q0 = pl.multiple_of(j * ATTN_BLOCK, ATTN_BLOCK)
        ws = pl.multiple_of(jnp.clip(q0 - ATTN_WINDOW, 0, t - span), ATTN_BLOCK)
        kw = kb_scr[pl.ds(ws, span), :]
        vw = vb_scr[pl.ds(ws, span), :]
        valid = jnp.abs((k_iota + ws) - (q_iota + q0)) <= ATTN_WINDOW
        cos = cos_ref[pl.ds(q0, ATTN_BLOCK), :]
        sa = sa_ref[pl.ds(q0, ATTN_BLOCK), :]
        sb = sb_ref[pl.ds(q0, ATTN_BLOCK), :]
        qbs = [_rope(_rms_w(q_ref[pl.ds(q0, ATTN_BLOCK), g * hd:(g + 1) * hd], qn_ref[...]), cos, sa, sb).astype(BF16)
               for g in range(ATTN_GROUP)]
        s_win = [jnp.where(valid, _dot_nt(qb, kw) * hd ** -0.5, NEG_INF) for qb in qbs]
        s_ctx = [_dot_nt(qb, ckb) * hd ** -0.5 for qb in qbs]
        ps = []
        for g in range(ATTN_GROUP):
            (e_win, e_ctx), den = _softmax_parts([s_win[g], s_ctx[g]], sink_ref[kvh * ATTN_GROUP + g])
            ps.append(((e_win / den).astype(BF16), (e_ctx / den).astype(BF16)))
        os_ = [_dot(pw, vw) + _dot(pc, cvb) for pw, pc in ps]
        for g, o in enumerate(os_):
            o_ref[pl.ds(q0, ATTN_BLOCK), g * hd:(g + 1) * hd] = o.astype(BF16)
        return carry

    lax.fori_loop(0, t // ATTN_BLOCK, block, 0)


def _attn_lat(qkv, cache_k, cache_v, slot, sink, q_norm, k_norm, y_ctx, n_seq, t, row_off):
    hd = ATTN_HEAD_DIM
    gw = ATTN_GROUP * hd
    kcol = ATTN_HEADS
    vcol = kcol + ATTN_KV_HEADS
    n_past = cache_k.shape[2]
    ck = cache_k.reshape(cache_k.shape[0], cache_k.shape[1], n_past, ATTN_KV_HEADS * hd)
    cv = cache_v.reshape(ck.shape)
    cos, sin_a, sin_b = _rope_tables(t)
    tab = pl.BlockSpec((t, hd), lambda b, h: (0, 0))
    return pl.pallas_call(
        _attn_lat_kernel,
        grid=(n_seq, ATTN_KV_HEADS),
        in_specs=[
            pl.BlockSpec(memory_space=pltpu.SMEM),
            pl.BlockSpec((t, gw), lambda b, h: (b + row_off, h)),
            pl.BlockSpec((t, hd), lambda b, h: (b + row_off, kcol + h)),
            pl.BlockSpec((t, hd), lambda b, h: (b + row_off, vcol + h)),
            pl.BlockSpec((1, 1, n_past, hd), lambda b, h: (b, slot, 0, h)),
            pl.BlockSpec((1, 1, n_past, hd), lambda b, h: (b, slot, 0, h)),
            pl.BlockSpec((1, hd), lambda b, h: (0, 0)),
            pl.BlockSpec((1, hd), lambda b, h: (0, 0)),
            tab, tab, tab,
            pl.BlockSpec(memory_space=pl.ANY),
        ],
        out_specs=pl.BlockSpec((t, gw), lambda b, h: (b + row_off, h)),
        out_shape=jax.ShapeDtypeStruct(y_ctx.shape, BF16),
        input_output_aliases={11: 0},
        scratch_shapes=[pltpu.VMEM((t, hd), BF16), pltpu.VMEM((t, hd), BF16)],
        compiler_params=_cparams(2),
        name="attn_lat",
    )(sink, qkv, qkv, qkv, ck, cv, q_norm[None, :], k_norm[None, :], cos, sin_a, sin_b, y_ctx)


def _cumsum_rows(x, reverse):
    n = x.shape[0]
    row = lax.broadcasted_iota(jnp.int32, x.shape, 0)
    sh = 1
    while sh < n:
        if reverse:
            x = x + jnp.where(row < n - sh, pltpu.roll(x, n - sh, 0), 0.0)
        else:
            x = x + jnp.where(row >= sh, pltpu.roll(x, sh, 0), 0.0)
        sh *= 2
    return x


def _log_sigmoid(x):
    return jnp.minimum(x, 0.0) - jnp.log(1.0 + jnp.exp(-jnp.abs(x)))


MLSTM_HP = 4


def _mlstm_kernel(has_init, emit_state, *refs):
    refs = list(refs)
    q_ref, k_ref, v_ref, og_ref, gt_ref, bg_ref, on_ref = refs[:7]
    refs = refs[7:]
    if has_init:
        c0_ref, n0_ref, m0_ref, _y_ctx = refs[:4]
        refs = refs[4:]
    y_ref = refs[0]
    refs = refs[1:]
    if emit_state:
        co_ref, no_ref, mo_ref = refs[:3]
        refs = refs[3:]
    hf_scr, hb_scr = refs

    L = MLSTM_CHUNK
    t = q_ref.shape[0]
    nc = t // L
    dk, dv = MLSTM_DK, MLSTM_DV
    row = lax.broadcasted_iota(jnp.int32, (L, L), 0)
    col = lax.broadcasted_iota(jnp.int32, (L, L), 1)
    lane = lax.broadcasted_iota(jnp.int32, (L, LANES), 1)

    def chunk_step(chains):
        st = []
        for z, hh, c, C, n, m in chains:
            rows = pl.ds(pl.multiple_of(c * L, L), L)
            qc = q_ref[rows, hh * dk:(hh + 1) * dk] * dk ** -0.5
            kc = k_ref[rows, hh * dk:(hh + 1) * dk]
            vc = v_ref[rows, hh * dv:(hh + 1) * dv]
            G = gt_ref[rows, hh * LANES:(hh + 1) * LANES] + bg_ref[hh]
            CUM = _cumsum_rows(_log_sigmoid(G), reverse=(z == 1))
            ci, cf = 2 * z, 2 * z + 1
            ZT = jnp.where(lane == cf, CUM, G).T
            i_row, cum_row = ZT[ci:ci + 1, :L], ZT[cf:cf + 1, :L]
            i_col, cum_col = G[:, ci:ci + 1], CUM[:, cf:cf + 1]
            mask = (col <= row) if z == 0 else (col >= row)
            dmat = jnp.where(mask, cum_col - cum_row + i_row, NEG_INF)
            inter = cum_col + m
            m_c = jnp.maximum(inter, jnp.max(dmat, axis=-1, keepdims=True))
            bl = cum_row[:, L - 1:L] if z == 0 else cum_row[:, 0:1]
            m_new = jnp.maximum(bl + m, jnp.max(bl - cum_row + i_row, axis=-1, keepdims=True))
            kw = kc * jnp.exp(bl - cum_col + i_col - m_new)
            st.append(dict(
                z=z, hh=hh, rows=rows, C=C, n=n, qc=qc, m_c=m_c, m_new=m_new,
                qb=qc.astype(BF16), kb=kc.astype(BF16), vb=vc.astype(BF16), Cb=C.astype(BF16),
                w=jnp.exp(dmat - m_c), a=jnp.exp(inter - m_c), decay=jnp.exp(bl + m - m_new),
                kw=kw, kwb=kw.astype(BF16)))
        for d in st:
            d["qk"] = _dot_nt(d["qb"], d["kb"])
            d["qC"] = _dot(d["qb"], d["Cb"])
            d["kv"] = _dot_tn(d["kwb"], d["vb"])
        for d in st:
            d["s"] = d["qk"] * d["w"]
        for d in st:
            d["sv"] = _dot(d["s"].astype(BF16), d["vb"])
        out = []
        for d in st:
            num = d["sv"] + d["a"] * d["qC"]
            den = jnp.sum(d["s"], axis=-1, keepdims=True) + d["a"] * jnp.sum(d["qc"] * d["n"], axis=-1, keepdims=True)
            scr = hf_scr if d["z"] == 0 else hb_scr
            scr[d["rows"], d["hh"] * dv:(d["hh"] + 1) * dv] = num / jnp.maximum(jnp.abs(den), jnp.exp(-d["m_c"]))
            out += [d["decay"] * d["C"] + d["kv"],
                    d["decay"] * d["n"] + jnp.sum(d["kw"], axis=0, keepdims=True),
                    d["m_new"]]
        return out

    def body(ci, carry):
        chains = []
        for hh in range(MLSTM_HP):
            Cf, nf, mf, Cb, nb, mb = carry[6 * hh:6 * hh + 6]
            chains.append((0, hh, ci, Cf, nf, mf))
            chains.append((1, hh, nc - 1 - ci, Cb, nb, mb))
        return tuple(chunk_step(chains))

    init = []
    for hh in range(MLSTM_HP):
        for z in range(2):
            if has_init:
                init += [c0_ref[0, z, hh], n0_ref[0, z, hh], m0_ref[0, z, hh][:, 0:1]]
            else:
                init += [jnp.zeros((dk, dv), F32), jnp.zeros((1, dk), F32), jnp.zeros((1, 1), F32)]
    final = lax.fori_loop(0, nc, body, tuple(init))

    for hh in range(MLSTM_HP):
        cols = slice(hh * dv, (hh + 1) * dv)
        hs = hf_scr[:, cols] + hb_scr[:, cols]
        y_ref[:, cols] = (_rms_w(hs, on_ref[:, cols]) * _sigmoid(og_ref[:, cols])).astype(BF16)
        if emit_state:
            for z in range(2):
                C, n, m = final[6 * hh + 3 * z:6 * hh + 3 * z + 3]
                co_ref[0, z, hh] = C
                no_ref[0, z, hh] = n
                mo_ref[0, z, hh] = jnp.broadcast_to(m, (1, LANES))


def _mlstm(proj, gates, b_gate_h, out_norm, state0, y_ctx, n_seq, t, row_off, emit_state):
    h_, dk, dv, hp = MLSTM_HEADS, MLSTM_DK, MLSTM_DV, MLSTM_HP
    nhb = h_ // hp
    has_init = state0 is not None
    in_specs = [
        pl.BlockSpec((t, hp * dk), lambda b, h: (b + row_off, h)),
        pl.BlockSpec((t, hp * dk), lambda b, h: (b + row_off, nhb + h)),
        pl.BlockSpec((t, hp * dv), lambda b, h: (b + row_off, nhb + h)),
        pl.BlockSpec((t, hp * dv), lambda b, h: (b + row_off, 2 * nhb + h)),
        pl.BlockSpec((t, hp * LANES), lambda b, h: (b + row_off, h)),
        pl.BlockSpec((hp, 1, LANES), lambda b, h: (h, 0, 0)),
        pl.BlockSpec((1, hp * dv), lambda b, h: (0, h)),
    ]
    args = [proj, proj, proj, proj, gates, b_gate_h[:, None, :], out_norm[None, :]]
    aliases = {}
    if has_init:
        c0, n0, m0 = state0
        in_specs += [
            pl.BlockSpec((1, 2, hp, dk, dv), lambda b, h: (b, 0, h, 0, 0)),
            pl.BlockSpec((1, 2, hp, 1, dk), lambda b, h: (b, 0, h, 0, 0)),
            pl.BlockSpec((1, 2, hp, 1, LANES), lambda b, h: (b, 0, h, 0, 0)),
            pl.BlockSpec(memory_space=pl.ANY),
        ]
        args += [c0, n0.reshape(n_seq, 2, h_, 1, dk),
                 jnp.broadcast_to(m0[..., None, None], (n_seq, 2, h_, 1, LANES)), y_ctx]
        aliases = {len(args) - 1: 0}
    out_specs = [pl.BlockSpec((t, hp * dv), lambda b, h: (b + row_off, h))]
    out_shape = [jax.ShapeDtypeStruct((N_TOK, D_MODEL), BF16)]
    if emit_state:
        out_specs += [
            pl.BlockSpec((1, 2, hp, dk, dv), lambda b, h: (b, 0, h, 0, 0)),
            pl.BlockSpec((1, 2, hp, 1, dk), lambda b, h: (b, 0, h, 0, 0)),
            pl.BlockSpec((1, 2, hp, 1, LANES), lambda b, h: (b, 0, h, 0, 0)),
        ]
        out_shape += [
            jax.ShapeDtypeStruct((n_seq, 2, h_, dk, dv), F32),
            jax.ShapeDtypeStruct((n_seq, 2, h_, 1, dk), F32),
            jax.ShapeDtypeStruct((n_seq, 2, h_, 1, LANES), F32),
        ]
    return pl.pallas_call(
        functools.partial(_mlstm_kernel, has_init, emit_state),
        grid=(n_seq, nhb),
        in_specs=in_specs,
        out_specs=out_specs,
        out_shape=out_shape,
        input_output_aliases=aliases,
        scratch_shapes=[pltpu.VMEM((t, hp * dv), F32), pltpu.VMEM((t, hp * dv), F32)],
        compiler_params=_cparams(2),
        name="mlstm",
    )(*args)


def _rwkv_mix_kernel(x_ref, xp_ref, xn_ref, shift_ref, scale_ref, g_ref, mu_ref, o_ref):
    i = pl.program_id(0)
    tm = x_ref.shape[0]
    g, sh, sc = g_ref[...], shift_ref[0], scale_ref[0]
    h = _modulated(x_ref[...], g, sh, sc)
    hp = _modulated(xp_ref[...], g, sh, sc)[7:8, :]
    hn = _modulated(xn_ref[...], g, sh, sc)[0:1, :]
    n_ctx_tiles = N_CTX_TOK // tm
    tiles_per_lat = DEC_SEQ // tm
    tile_in_seq = jnp.where(i < n_ctx_tiles, i % (SEQ // tm), (i - n_ctx_tiles) % tiles_per_lat)
    tiles_in_seq = jnp.where(i < n_ctx_tiles, SEQ // tm, tiles_per_lat)
    has_prev = (tile_in_seq > 0).astype(F32)
    has_next = (tile_in_seq < tiles_in_seq - 1).astype(F32)
    row = lax.broadcasted_iota(jnp.int32, h.shape, 0)
    prev = jnp.where(row == 0, hp * has_prev, pltpu.roll(h, 1, 0))
    nxt = jnp.where(row == tm - 1, hn * has_next, pltpu.roll(h, tm - 1, 0))
    cs = 0.5 * (prev + nxt) - h
    for p in range(o_ref.shape[0]):
        o_ref[p] = (h + cs * mu_ref[p:p + 1, :]).astype(BF16)


def _rwkv_mix(x, mods3, g_row, mu):
    tm = 256
    nb8 = tm // 8
    last = N_TOK // 8 - 1
    return pl.pallas_call(
        _rwkv_mix_kernel,
        grid=(N_TOK // tm,),
        in_specs=[
            pl.BlockSpec((tm, D_MODEL), lambda i: (i, 0)),
            pl.BlockSpec((8, D_MODEL), lambda i: (jnp.maximum(i * nb8 - 1, 0), 0)),
            pl.BlockSpec((8, D_MODEL), lambda i: (jnp.minimum((i + 1) * nb8, last), 0)),
            pl.BlockSpec((1, 1, D_MODEL), lambda i: (_mod_row(i, tm), 0, 3)),
            pl.BlockSpec((1, 1, D_MODEL), lambda i: (_mod_row(i, tm), 0, 4)),
            pl.BlockSpec((1, D_MODEL), lambda i: (0, 0)),
            pl.BlockSpec((6, D_MODEL), lambda i: (0, 0)),
        ],
        out_specs=pl.BlockSpec((6, tm, D_MODEL), lambda i: (0, i, 0)),
        out_shape=jax.ShapeDtypeStruct((6, N_TOK, D_MODEL), BF16),
        compiler_params=_cparams(1),
        name="rwkv_mix",
    )(x, x, x, mods3, mods3, g_row, mu)


def _lora_kernel(act, post, x_ref, a_ref, b_ref, bias_ref, o_ref):
    u = _dot(x_ref[0], a_ref[0])
    if act == "tanh":
        u = jnp.tanh(u)
    elif act == "sigmoid":
        u = _sigmoid(u)
    y = _dot(u.astype(BF16), b_ref[0]) + bias_ref[0]
    if post == "log_decay":
        sp = jnp.maximum(-y, 0.0) + jnp.log(1.0 + jnp.exp(-jnp.abs(y)))
        y = -jnp.exp(-sp - 0.5)
    elif post == "sigmoid":
        y = _sigmoid(y)
    o_ref[0] = y


def _lora(xs, p, wa, wb, bias, act, post, name):
    tm = 512
    nz, _, r = wa.shape
    return pl.pallas_call(
        functools.partial(_lora_kernel, act, post),
        grid=(nz, N_TOK // tm),
        in_specs=[
            pl.BlockSpec((1, tm, D_MODEL), lambda z, i: (p, i, 0)),
            pl.BlockSpec((1, D_MODEL, r), lambda z, i: (z, 0, 0)),
            pl.BlockSpec((1, r, D_MODEL), lambda z, i: (z, 0, 0)),
            pl.BlockSpec((1, 1, D_MODEL), lambda z, i: (z, 0, 0)),
        ],
        out_specs=pl.BlockSpec((1, tm, D_MODEL), lambda z, i: (z, i, 0)),
        out_shape=jax.ShapeDtypeStruct((nz, N_TOK, D_MODEL), F32),
        compiler_params=_cparams(2),
        name=name,
    )(xs, wa, wb, bias)


def _rkv_kernel(x_ref, w_ref, o_ref):
    x = x_ref[0]
    for c in range(o_ref.shape[2] // MM_COLS):
        cols = slice(c * MM_COLS, (c + 1) * MM_COLS)
        o_ref[0, :, cols] = _dot(x, w_ref[0, :, cols])


def _rkv(xs, w_rkv):
    tm = 512
    return pl.pallas_call(
        _rkv_kernel,
        grid=(3, N_TOK // tm),
        in_specs=[pl.BlockSpec((1, tm, D_MODEL), lambda p, i: (p, i, 0)),
                  pl.BlockSpec((1, D_MODEL, D_MODEL), lambda p, i: (p, 0, 0))],
        out_specs=pl.BlockSpec((1, tm, D_MODEL), lambda p, i: (p, i, 0)),
        out_shape=jax.ShapeDtypeStruct((3, N_TOK, D_MODEL), F32),
        compiler_params=_cparams(2),
        name="rwkv_rkv",
    )(xs, w_rkv)


def _head_sum(x, first_head):
    s0 = jnp.sum(jnp.where(first_head, x, 0.0), axis=-1, keepdims=True)
    s1 = jnp.sum(jnp.where(first_head, 0.0, x), axis=-1, keepdims=True)
    return jnp.where(first_head, s0, s1)


RWKV_PP = 4


def _rwkv_kernel(has_init, emit_state, *refs):
    refs = list(refs)
    r_ref, k_ref, v_ref, lw_ref, a_ref, g_ref, kk_w_ref, ka_ref, rk_ref, lng_ref, lnb_ref = refs[:11]
    refs = refs[11:]
    if has_init:
        s0_ref, _y_ctx = refs[:2]
        refs = refs[2:]
    y_ref = refs[0]
    refs = refs[1:]
    if emit_state:
        so_ref = refs[0]
        refs = refs[1:]
    kk_scr, yf_scr, yb_scr = refs

    L = RWKV_CHUNK
    N = RWKV_HEAD
    t = r_ref.shape[1]
    nc = t // L
    row = lax.broadcasted_iota(jnp.int32, (L, L), 0)
    col = lax.broadcasted_iota(jnp.int32, (L, L), 1)
    first_head = lax.broadcasted_iota(jnp.int32, (L, LANES), 1) < N
    sq_row = lax.broadcasted_iota(jnp.int32, (LANES, LANES), 0)
    sq_col = lax.broadcasted_iota(jnp.int32, (LANES, LANES), 1)
    same_head = (sq_row < N) == (sq_col < N)

    first_head_t = lax.broadcasted_iota(jnp.int32, (t, LANES), 1) < N

    for pp in range(RWKV_PP):
        lanes = slice(pp * LANES, (pp + 1) * LANES)
        kkx = k_ref[0, :, lanes] * kk_w_ref[:, lanes]
        kk_scr[:, lanes] = kkx / jnp.maximum(jnp.sqrt(_head_sum(kkx * kkx, first_head_t)), 1e-12)

    def chunk_step(chains):
        heads = (first_head, jnp.logical_not(first_head))
        pick = lambda x0, x1: jnp.where(first_head, x0, x1)
        st = []
        for z, pp, c, S in chains:
            lanes = slice(pp * LANES, (pp + 1) * LANES)
            sl = pl.ds(pl.multiple_of(c * L, L), L)
            r_, k_, v_ = r_ref[0, sl, lanes], k_ref[0, sl, lanes], v_ref[0, sl, lanes]
            lw, a_ = lw_ref[z, sl, lanes], a_ref[z, sl, lanes]
            kk = kk_scr[sl, lanes]
            kd = k_ * (1.0 + (a_ - 1.0) * ka_ref[:, lanes])
            b_ = kk * a_
            cum = _cumsum_rows(lw, reverse=(z == 1))
            cum_end = cum[L - 1:L, :] if z == 0 else cum[0:1, :]
            kp = (kk * jnp.exp(cum - lw)).astype(BF16)
            rp = (r_ * jnp.exp(cum)).astype(BF16)
            inv = jnp.exp(-cum)
            to_end = jnp.exp(cum_end - cum)
            zero = jnp.zeros_like(kp)
            st.append(dict(
                z=z, pp=pp, sl=sl, lanes=lanes, S=S, kp=kp, rp=rp, cum_end=cum_end,
                kt=(kd * inv).astype(BF16), bt=(b_ * inv).astype(BF16),
                kb_end=jnp.concatenate([kd * to_end, -(b_ * to_end)], axis=0).astype(BF16),
                vb=v_.astype(BF16), Sb=S.astype(BF16),
                probe=[jnp.concatenate([jnp.where(hm, kp, zero), jnp.where(hm, rp, zero)], axis=0) for hm in heads],
                strict=(col < row) if z == 0 else (col > row),
                incl=(col <= row) if z == 0 else (col >= row)))
        for d in st:
            d["gk"] = [_dot_nt(pr, d["kt"]) for pr in d["probe"]]
            d["gb"] = [_dot_nt(pr, d["bt"]) for pr in d["probe"]]
        for d in st:
            d["g1"] = [jnp.where(d["strict"], g[:L], 0.0).astype(BF16) for g in d["gk"]]
            d["n"] = [jnp.where(d["strict"], g[:L], 0.0).astype(BF16) for g in d["gb"]]
            d["g3"] = [jnp.where(d["incl"], g[L:], 0.0).astype(BF16) for g in d["gk"]]
            d["g4"] = [jnp.where(d["incl"], g[L:], 0.0).astype(BF16) for g in d["gb"]]
        for d in st:
            d["g1v"] = [_dot(g, d["vb"]) for g in d["g1"]]
            d["kps"] = _dot_nt(d["kp"], d["Sb"])
            d["rps"] = _dot_nt(d["rp"], d["Sb"])
        for d in st:
            d["u"] = d["kps"] + pick(*d["g1v"])
        sign = -1.0
        p = 1
        while p < L:
            for d in st:
                ub = d["u"].astype(BF16)
                d["nu"] = [_dot(m, ub) for m in d["n"]]
                if 2 * p < L:
                    d["n"] = [_dot(m, m).astype(BF16) for m in d["n"]]
            for d in st:
                d["u"] = d["u"] + sign * pick(*d["nu"])
            sign = 1.0
            p *= 2
        for d in st:
            ub = d["u"].astype(BF16)
            d["y3"] = [_dot(g, d["vb"]) for g in d["g3"]]
            d["y4"] = [_dot(g, ub) for g in d["g4"]]
            d["upd"] = _dot_tn(jnp.concatenate([d["vb"], ub], axis=0), d["kb_end"])
        out = []
        for d in st:
            y = d["rps"] + pick(d["y3"][0] - d["y4"][0], d["y3"][1] - d["y4"][1])
            (yf_scr if d["z"] == 0 else yb_scr)[d["sl"], d["lanes"]] = y
            out.append(d["S"] * jnp.exp(d["cum_end"]) + jnp.where(same_head, d["upd"], 0.0))
        return out

    def body(ci, carry):
        chains = []
        for pp in range(RWKV_PP):
            chains.append((0, pp, ci, carry[2 * pp]))
            chains.append((1, pp, nc - 1 - ci, carry[2 * pp + 1]))
        return tuple(chunk_step(chains))

    init = []
    for pp in range(RWKV_PP):
        for z in range(2):
            if has_init:
                s = s0_ref[0, z, pp]
                init.append(jnp.where(sq_row < N, s, pltpu.roll(s, N, 1)))
            else:
                init.append(jnp.zeros((LANES, LANES), F32))
    final = lax.fori_loop(0, nc, body, tuple(init))

    for pp in range(RWKV_PP):
        lanes = slice(pp * LANES, (pp + 1) * LANES)
        r_, k_, v_ = r_ref[0, :, lanes], k_ref[0, :, lanes], v_ref[0, :, lanes]
        y = yf_scr[:, lanes] + yb_scr[:, lanes]
        mean = _head_sum(y, first_head_t) * (1.0 / N)
        yc = y - mean
        var = _head_sum(yc * yc, first_head_t) * (1.0 / N)
        yn = yc * lax.rsqrt(var + RWKV_GN_EPS) * lng_ref[:, lanes] + lnb_ref[:, lanes]
        ka = ka_ref[:, lanes]
        kd_sum = k_ * (1.0 + (a_ref[0, :, lanes] - 1.0) * ka) + k_ * (1.0 + (a_ref[1, :, lanes] - 1.0) * ka)
        bonus = _head_sum(r_ * kd_sum * rk_ref[:, lanes], first_head_t) * v_
        y_ref[:, lanes] = ((yn + bonus) * g_ref[:, lanes]).astype(BF16)
        if emit_state:
            for z in range(2):
                S = final[2 * pp + z]
                so_ref[0, z, pp] = (S + pltpu.roll(S, N, 1))[:, :N]


def _rwkv(rkv, lw, a, g, k_k, k_a, r_k, ln_g, ln_b, s0, y_ctx, n_seq, t, row_off, emit_state):
    has_init = s0 is not None
    w = RWKV_PP * LANES
    tok = lambda lead, first: pl.BlockSpec((lead, t, w), lambda b, p: (first, b + row_off, p))
    par = pl.BlockSpec((1, w), lambda b, p: (0, p))
    in_specs = [tok(1, 0), tok(1, 1), tok(1, 2), tok(2, 0), tok(2, 0),
                pl.BlockSpec((t, w), lambda b, p: (b + row_off, p)),
                par, par, par, par, par]
    args = [rkv, rkv, rkv, lw, a, g, k_k[None, :], k_a[None, :], r_k.reshape(1, D_MODEL), ln_g[None, :], ln_b[None, :]]
    aliases = {}
    if has_init:
        in_specs += [pl.BlockSpec((1, 2, RWKV_PP, LANES, LANES), lambda b, p: (b, 0, p, 0, 0)),
                     pl.BlockSpec(memory_space=pl.ANY)]
        args += [s0, y_ctx]
        aliases = {len(args) - 1: 0}
    out_specs = [pl.BlockSpec((t, w), lambda b, p: (b + row_off, p))]
    out_shape = [jax.ShapeDtypeStruct((N_TOK, D_MODEL), BF16)]
    if emit_state:
        out_specs.append(pl.BlockSpec((1, 2, RWKV_PP, LANES, RWKV_HEAD), lambda b, p: (b, 0, p, 0, 0)))
        out_shape.append(jax.ShapeDtypeStruct((n_seq, 2, RWKV_PAIRS, LANES, RWKV_HEAD), F32))
    return pl.pallas_call(
        functools.partial(_rwkv_kernel, has_init, emit_state),
        grid=(n_seq, RWKV_PAIRS // RWKV_PP),
        in_specs=in_specs,
        out_specs=out_specs,
        out_shape=out_shape,
        input_output_aliases=aliases,
        scratch_shapes=[pltpu.VMEM((t, w), F32)] * 3,
        compiler_params=_cparams(2),
        name="rwkv",
    )(*args)


def _attn_layer(x, mods3, g_row, cache_k, cache_v, slot, w_qkv, q_norm, k_norm, sink, w_o):
    qkv = _mm_mod(x, mods3, 3, g_row, w_qkv.astype(BF16), 512, "attn_qkv")
    y, k_ctx, v_ctx = _attn_ctx(qkv, sink, q_norm, k_norm, BATCH, SEQ)
    y = _attn_lat(qkv, cache_k, cache_v, slot, sink, q_norm, k_norm, y, DEC_BATCH, DEC_SEQ, N_CTX_TOK // DEC_SEQ)
    x = _mm_res(y, w_o.astype(BF16), x, mods3, 5, "attn_out")
    shape = (BATCH, SEQ, ATTN_KV_HEADS, ATTN_HEAD_DIM)
    return x, k_ctx.reshape(shape), v_ctx.reshape(shape)


def _mlstm_layer(x, mods3, g_row, state_lat, w_in, w_gate, b_gate, out_norm, w_o):
    h_ = MLSTM_HEADS
    proj = _mm_mod(x, mods3, 3, g_row, w_in.astype(BF16), 256, "mlstm_in")
    wg = jnp.transpose(w_gate.reshape(D_MODEL, 4, h_), (0, 2, 1))
    wg = jnp.pad(wg, ((0, 0), (0, 0), (0, LANES - 4))).reshape(D_MODEL, h_ * LANES).astype(BF16)
    bg = jnp.pad(b_gate.reshape(4, h_).T, ((0, 0), (0, LANES - 4)))
    gates = _mm_mod(x, mods3, 3, g_row, wg, 512, "mlstm_gates")
    y, C, n, m = _mlstm(proj, gates, bg, out_norm, None, None, BATCH, SEQ, 0, True)
    (y,) = _mlstm(proj, gates, bg, out_norm, state_lat, y, DEC_BATCH, DEC_SEQ, N_CTX_TOK // DEC_SEQ, False)
    x = _mm_res(y, w_o.astype(BF16), x, mods3, 5, "mlstm_out")
    return x, C, n[:, :, :, 0, :], m[:, :, :, 0, 0]


def _rwkv_layer(x, mods3, g_row, s0_lat, mu, w_rkv, w0, wA, wB, a0, aA, aB, gA, gB, k_k, k_a, r_k, ln_g, ln_b, w_o):
    xs = _rwkv_mix(x, mods3, g_row, mu)
    rkv = _rkv(xs, w_rkv.astype(BF16))
    pad_r = lambda w, axis: jnp.pad(w, [(0, 0) if ax != axis else (0, LANES - w.shape[axis]) for ax in range(w.ndim)])
    lw = _lora(xs, 3, pad_r(wA, 2).astype(BF16), pad_r(wB, 1).astype(BF16), w0[:, None, :], "tanh", "log_decay", "rwkv_w")
    a = _lora(xs, 4, pad_r(aA, 2).astype(BF16), pad_r(aB, 1).astype(BF16), a0[:, None, :], "none", "sigmoid", "rwkv_a")
    g = _lora(xs, 5, gA[None].astype(BF16), gB[None].astype(BF16), jnp.zeros((1, 1, D_MODEL), F32), "sigmoid", "none", "rwkv_g")[0]
    s0 = jnp.pad(s0_lat.reshape(DEC_BATCH, 2, RWKV_PAIRS, LANES, RWKV_HEAD), ((0, 0),) * 4 + ((0, LANES - RWKV_HEAD),))
    y, S = _rwkv(rkv, lw, a, g, k_k, k_a, r_k, ln_g, ln_b, None, None, BATCH, SEQ, 0, True)
    (y,) = _rwkv(rkv, lw, a, g, k_k, k_a, r_k, ln_g, ln_b, s0, y, DEC_BATCH, DEC_SEQ, N_CTX_TOK // DEC_SEQ, False)
    x = _mm_res(y, w_o.astype(BF16), x, mods3, 5, "rwkv_out")
    return x, S.reshape(BATCH, 2, RWKV_HEADS, RWKV_HEAD, RWKV_HEAD)


def kernel(x_prompt, x_sample, cache_k, cache_v, state_mlstm_C, state_mlstm_n, state_mlstm_m, state_rwkv, c, c_ctx, mod_w, mod_b, norm_g, ffn_w_in, ffn_w_out, attn_w_qkv, attn_q_norm, attn_k_norm, attn_sink, attn_w_o, mlstm_w_in, mlstm_w_gate, mlstm_b_gate, mlstm_out_norm, mlstm_w_o, rwkv_mu, rwkv_w_rkv, rwkv_w0, rwkv_wA, rwkv_wB, rwkv_a0, rwkv_aA, rwkv_aB, rwkv_gA, rwkv_gB, rwkv_k_k, rwkv_k_a, rwkv_r_k, rwkv_ln_g, rwkv_ln_b, rwkv_w_o):
    x = jnp.concatenate([x_prompt.reshape(N_CTX_TOK, D_MODEL), x_sample.reshape(N_LAT_TOK, D_MODEL)], axis=0)
    cond8 = jnp.concatenate([c_ctx[None, :], c, jnp.zeros((MOD_ROWS - 1 - DEC_BATCH, D_MODEL), F32)], axis=0)
    mods = _adaln_all(cond8, mod_w, mod_b)
    ffn_w_in_b = ffn_w_in.astype(BF16)
    ffn_w_out_b = ffn_w_out.astype(BF16)

    new_k, new_v, new_C, new_n, new_m, new_S = [], [], [], [], [], []
    for i in range(DEPTH):
        kind, slot = i % N_MIXERS, i // N_MIXERS
        mods3 = mods[i].reshape(MOD_ROWS, 1, N_MOD * D_MODEL)
        x = _ffn(x, mods3, 0, norm_g[i, 0][None, :], ffn_w_in_b[i, 0], ffn_w_out_b[i, 0])
        g_row = norm_g[i, 1][None, :]
        if kind == 0:
            x, k_ctx, v_ctx = _attn_layer(x, mods3, g_row, cache_k, cache_v, slot, attn_w_qkv[slot], attn_q_norm[slot],
                                          attn_k_norm[slot], attn_sink[slot], attn_w_o[slot])
            new_k.append(k_ctx)
            new_v.append(v_ctx)
        elif kind == 1:
            state_lat = (state_mlstm_C[:, slot], state_mlstm_n[:, slot], state_mlstm_m[:, slot])
            x, C, n, m = _mlstm_layer(x, mods3, g_row, state_lat, mlstm_w_in[slot], mlstm_w_gate[slot], mlstm_b_gate[slot],
                                      mlstm_out_norm[slot], mlstm_w_o[slot])
            new_C.append(C)
            new_n.append(n)
            new_m.append(m)
        else:
            x, S = _rwkv_layer(x, mods3, g_row, state_rwkv[:, slot], rwkv_mu[slot], rwkv_w_rkv[slot], rwkv_w0[slot],
                               rwkv_wA[slot], rwkv_wB[slot], rwkv_a0[slot], rwkv_aA[slot], rwkv_aB[slot], rwkv_gA[slot],
                               rwkv_gB[slot], rwkv_k_k[slot], rwkv_k_a[slot], rwkv_r_k[slot], rwkv_ln_g[slot],
                               rwkv_ln_b[slot], rwkv_w_o[slot])
            new_S.append(S)
        x = _ffn(x, mods3, 6, norm_g[i, 2][None, :], ffn_w_in_b[i, 1], ffn_w_out_b[i, 1])

    y_prompt = x[:N_CTX_TOK].reshape(BATCH, SEQ, D_MODEL)
    y_sample = x[N_CTX_TOK:].reshape(DEC_BATCH, DEC_SEQ, D_MODEL)
    return (y_prompt, y_sample, jnp.stack(new_k, axis=1), jnp.stack(new_v, axis=1),
            jnp.stack(new_C, axis=1), jnp.stack(new_n, axis=1), jnp.stack(new_m, axis=1), jnp.stack(new_S, axis=1))
```

```python
import functools

import jax
import jax.numpy as jnp
from jax import lax
from jax.experimental import pallas as pl
from jax.experimental.pallas import tpu as pltpu

D_MODEL = 2048
BATCH = 32
SEQ = 256
DEPTH = 4
DEC_BATCH = 4
DEC_SEQ = 1024
PAST_LEN = 256
GRID_W = 64
N_MIXERS = 3
N_MOD = 9
D_FF = 5632
EPS = 1e-6

ATTN_HEADS = 16
ATTN_KV_HEADS = 4
ATTN_GROUP = ATTN_HEADS // ATTN_KV_HEADS
ATTN_HEAD_DIM = D_MODEL // ATTN_HEADS
ATTN_WINDOW = 128
ATTN_BLOCK = 128
ROPE_THETA = 10000.0
QKV_DIM = (ATTN_HEADS + 2 * ATTN_KV_HEADS) * ATTN_HEAD_DIM

MLSTM_HEADS = 8
MLSTM_DV = D_MODEL // MLSTM_HEADS
MLSTM_DK = MLSTM_DV // 2
MLSTM_CHUNK = 64
MLSTM_IN_DIM = 2 * MLSTM_HEADS * MLSTM_DK + 2 * D_MODEL

RWKV_HEAD = 64
RWKV_HEADS = D_MODEL // RWKV_HEAD
RWKV_PAIRS = RWKV_HEADS // 2
RWKV_CHUNK = 64
RWKV_GN_EPS = 64e-5

N_CTX_TOK = BATCH * SEQ
N_LAT_TOK = DEC_BATCH * DEC_SEQ
N_TOK = N_CTX_TOK + N_LAT_TOK
MOD_ROWS = 8
LANES = 128

VMEM_LIMIT = 56 * 1024 * 1024
BF16 = jnp.bfloat16
F32 = jnp.float32
NEG_INF = float("-inf")
MM_COLS = 512


def _cparams(n_axes):
    return pltpu.CompilerParams(dimension_semantics=("arbitrary",) * n_axes, vmem_limit_bytes=VMEM_LIMIT)


def _mod_row(i, tm):
    n_ctx_tiles = N_CTX_TOK // tm
    tiles_per_lat = DEC_SEQ // tm
    return jnp.where(i < n_ctx_tiles, 0, 1 + (i - n_ctx_tiles) // tiles_per_lat)


def _mod_spec(slot, tm):
    return pl.BlockSpec((1, 1, D_MODEL), lambda i, j, s=slot: (_mod_row(i, tm), 0, s))


def _dot(a, b):
    return jnp.dot(a, b, preferred_element_type=F32)


def _dot_nt(a, b):
    return lax.dot_general(a, b, (((1,), (1,)), ((), ())), preferred_element_type=F32)


def _dot_tn(a, b):
    return lax.dot_general(a, b, (((0,), (0,)), ((), ())), preferred_element_type=F32)


def _sigmoid(x):
    return jax.nn.sigmoid(x)


def _modulated(x, g, shift, scale):
    ms = jnp.mean(x * x, axis=-1, keepdims=True)
    y = x * lax.rsqrt(ms + EPS) * g
    return y * (1.0 + scale) + shift


def _adaln_kernel(cond_ref, w_ref, b_ref, o_ref):
    c = cond_ref[...]
    a = (c * _sigmoid(c)).astype(BF16)
    o_ref[0] = _dot(a, w_ref[0].astype(BF16)) + b_ref[0]


def _adaln_all(cond8, mod_w, mod_b):
    tn = 1024
    n_out = N_MOD * D_MODEL
    return pl.pallas_call(
        _adaln_kernel,
        grid=(DEPTH, n_out // tn),
        in_specs=[
            pl.BlockSpec((MOD_ROWS, D_MODEL), lambda l, j: (0, 0)),
            pl.BlockSpec((1, D_MODEL, tn), lambda l, j: (l, 0, j)),
            pl.BlockSpec((1, 1, tn), lambda l, j: (l, 0, j)),
        ],
        out_specs=pl.BlockSpec((1, MOD_ROWS, tn), lambda l, j: (l, 0, j)),
        out_shape=jax.ShapeDtypeStruct((DEPTH, MOD_ROWS, n_out), F32),
        compiler_params=_cparams(2),
        name="adaln",
    )(cond8, mod_w, mod_b.reshape(DEPTH, 1, n_out))


FFN_TM = 1024
FFN_TF = 512
ROW_CHUNK = 32


def _ffn_kernel(x_ref, shift_ref, scale_ref, gate_ref, g_ref, wg_ref, wu_ref, wo_ref, o_ref, h_scr):
    f = pl.program_id(1)

    @pl.when(f == 0)
    def _():
        def rows(c, carry):
            r = pl.ds(pl.multiple_of(c * ROW_CHUNK, ROW_CHUNK), ROW_CHUNK)
            h_scr[r, :] = _modulated(x_ref[r, :], g_ref[...], shift_ref[0], scale_ref[0]).astype(BF16)
            return carry
        lax.fori_loop(0, x_ref.shape[0] // ROW_CHUNK, rows, 0, unroll=8)

    h = h_scr[...]
    a = _dot(h, wg_ref[0, 0])
    b = _dot(h, wu_ref[0, 0])
    act = ((a * _sigmoid(a)) * b).astype(BF16)
    for c in range(o_ref.shape[1] // MM_COLS):
        cols = slice(c * MM_COLS, (c + 1) * MM_COLS)
        o_ref[:, cols] = jnp.where(f == 0, 0.0, o_ref[:, cols]) + _dot(act, wo_ref[0, 0, :, cols])

    @pl.when(f == pl.num_programs(1) - 1)
    def _():
        o_ref[...] = x_ref[...] + (0.5 * gate_ref[0]) * o_ref[...]


def _ffn(x, mods3, slot, g_row, w_in, w_out, layer, which):
    tm, tf = FFN_TM, FFN_TF
    nf = D_FF // tf
    return pl.pallas_call(
        _ffn_kernel,
        grid=(N_TOK // tm, nf),
        in_specs=[
            pl.BlockSpec((tm, D_MODEL), lambda i, f: (i, 0)),
            _mod_spec(slot, tm), _mod_spec(slot + 1, tm), _mod_spec(slot + 2, tm),
            pl.BlockSpec((1, D_MODEL), lambda i, f: (0, 0)),
            pl.BlockSpec((1, 1, D_MODEL, tf), lambda i, f: (layer, which, 0, f)),
            pl.BlockSpec((1, 1, D_MODEL, tf), lambda i, f: (layer, which, 0, f + nf)),
            pl.BlockSpec((1, 1, tf, D_MODEL), lambda i, f: (layer, which, f, 0)),
        ],
        out_specs=pl.BlockSpec((tm, D_MODEL), lambda i, f: (i, 0)),
        out_shape=jax.ShapeDtypeStruct((N_TOK, D_MODEL), F32),
        scratch_shapes=[pltpu.VMEM((tm, D_MODEL), BF16)],
        compiler_params=_cparams(2),
        name="ffn",
    )(x, mods3, mods3, mods3, g_row, w_in, w_in, w_out)


def _resident(shape):
    return pl.BlockSpec(shape, lambda *_: (0,) * len(shape), pipeline_mode=pl.Buffered(1))


def _mm_mod_kernel(x_ref, shift_ref, scale_ref, g_ref, w_ref, o_ref):
    h = _modulated(x_ref[...], g_ref[...], shift_ref[0], scale_ref[0]).astype(BF16)
    for c in range(o_ref.shape[1] // MM_COLS):
        cols = slice(c * MM_COLS, (c + 1) * MM_COLS)
        o_ref[:, cols] = _dot(h, w_ref[:, cols])


def _mm_mod(x, mods3, slot, g_row, w, tm, name):
    n = w.shape[1]
    mod = lambda s: pl.BlockSpec((1, 1, D_MODEL), lambda i: (_mod_row(i, tm), 0, s))
    return pl.pallas_call(
        _mm_mod_kernel,
        grid=(N_TOK // tm,),
        in_specs=[
            pl.BlockSpec((tm, D_MODEL), lambda i: (i, 0)),
            mod(slot), mod(slot + 1),
            _resident((1, D_MODEL)),
            _resident((D_MODEL, n)),
        ],
        out_specs=pl.BlockSpec((tm, n), lambda i: (i, 0)),
        out_shape=jax.ShapeDtypeStruct((N_TOK, n), F32),
        compiler_params=_cparams(1),
        name=name,
    )(x, mods3, mods3, g_row, w)


def _mm_res_kernel(y_ref, w_ref, x_ref, gate_ref, o_ref):
    y = y_ref[...]
    for c in range(o_ref.shape[1] // MM_COLS):
        cols = slice(c * MM_COLS, (c + 1) * MM_COLS)
        o_ref[:, cols] = x_ref[:, cols] + gate_ref[0][:, cols] * _dot(y, w_ref[:, cols])


def _mm_res(y, w, x, mods3, slot, name):
    tm = 512
    k = y.shape[1]
    return pl.pallas_call(
        _mm_res_kernel,
        grid=(N_TOK // tm,),
        in_specs=[
            pl.BlockSpec((tm, k), lambda i: (i, 0)),
            _resident((k, D_MODEL)),
            pl.BlockSpec((tm, D_MODEL), lambda i: (i, 0)),
            pl.BlockSpec((1, 1, D_MODEL), lambda i: (_mod_row(i, tm), 0, slot)),
        ],
        out_specs=pl.BlockSpec((tm, D_MODEL), lambda i: (i, 0)),
        out_shape=jax.ShapeDtypeStruct((N_TOK, D_MODEL), F32),
        compiler_params=_cparams(1),
        name=name,
    )(y, w, x, mods3)


def _rms_w(x, w):
    return x * lax.rsqrt(jnp.mean(x * x, axis=-1, keepdims=True) + EPS) * w


def _softmax_parts(scores, sink):
    m = sink
    for s in scores:
        m = jnp.maximum(jnp.max(s, axis=-1, keepdims=True), m)
    es = [jnp.exp(s - m) for s in scores]
    den = jnp.exp(sink - m)
    for e in es:
        den = den + jnp.sum(e, axis=-1, keepdims=True)
    return es, den


def _attn_ctx_kernel(sink_ref, q_ref, k_ref, v_ref, qn_ref, kn_ref, o_ref, ko_ref, vo_ref):
    kvh = pl.program_id(1)
    hd = ATTN_HEAD_DIM
    k = _rms_w(k_ref[...], kn_ref[...])
    v = v_ref[...]
    ko_ref[...] = k
    vo_ref[...] = v
    kb = k.astype(BF16)
    vb = v.astype(BF16)
    qs = [_rms_w(q_ref[:, g * hd:(g + 1) * hd], qn_ref[...]).astype(BF16) for g in range(ATTN_GROUP)]
    ss = [_dot_nt(q, kb) * hd ** -0.5 for q in qs]
    ps = []
    for g, s in enumerate(ss):
        (e,), den = _softmax_parts([s], sink_ref[kvh * ATTN_GROUP + g])
        ps.append((e / den).astype(BF16))
    os_ = [_dot(p, vb) for p in ps]
    for g, o in enumerate(os_):
        o_ref[:, g * hd:(g + 1) * hd] = o.astype(BF16)


def _attn_ctx(qkv, sink, q_norm, k_norm, n_seq, t):
    hd = ATTN_HEAD_DIM
    gw = ATTN_GROUP * hd
    kcol = ATTN_HEADS * hd // hd
    vcol = kcol + ATTN_KV_HEADS
    n_tok = n_seq * t
    return pl.pallas_call(
        _attn_ctx_kernel,
        grid=(n_seq, ATTN_KV_HEADS),
        in_specs=[
            pl.BlockSpec(memory_space=pltpu.SMEM),
            pl.BlockSpec((t, gw), lambda b, h: (b, h)),
            pl.BlockSpec((t, hd), lambda b, h: (b, kcol + h)),
            pl.BlockSpec((t, hd), lambda b, h: (b, vcol + h)),
            pl.BlockSpec((1, hd), lambda b, h: (0, 0)),
            pl.BlockSpec((1, hd), lambda b, h: (0, 0)),
        ],
        out_specs=[
            pl.BlockSpec((t, gw), lambda b, h: (b, h)),
            pl.BlockSpec((t, hd), lambda b, h: (b, h)),
            pl.BlockSpec((t, hd), lambda b, h: (b, h)),
        ],
        out_shape=[
            jax.ShapeDtypeStruct((N_TOK, ATTN_HEADS * hd), BF16),
            jax.ShapeDtypeStruct((n_tok, ATTN_KV_HEADS * hd), F32),
            jax.ShapeDtypeStruct((n_tok, ATTN_KV_HEADS * hd), F32),
        ],
        compiler_params=_cparams(2),
        name="attn_ctx",
    )(sink, qkv, qkv, qkv, q_norm[None, :], k_norm[None, :])


def _rope(x, cos, sin_a, sin_b):
    quarter = ATTN_HEAD_DIM // 4
    return x * cos + pltpu.roll(x, ATTN_HEAD_DIM - quarter, 1) * sin_a + pltpu.roll(x, quarter, 1) * sin_b


def _rope_tables(t):
    hd = ATTN_HEAD_DIM
    quarter = hd // 4
    pos = jnp.arange(t)
    row = (pos // GRID_W).astype(F32)
    col = (pos % GRID_W).astype(F32)
    inv_freq = ROPE_THETA ** (-jnp.arange(quarter, dtype=F32) / quarter)
    lane = jnp.arange(hd)
    ang = jnp.where(lane[None, :] < hd // 2, row[:, None], col[:, None]) * inv_freq[lane % quarter][None, :]
    first = (lane % (hd // 2)) < quarter
    sin = jnp.sin(ang)
    return jnp.cos(ang), jnp.where(first[None, :], -sin, 0.0), jnp.where(first[None, :], 0.0, sin)


ATTN_LAT_NQB = 2


def _attn_lat_kernel(sink_ref, q_ref, k_ref, v_ref, ck_ref, cv_ref, qn_ref, kn_ref, cos_ref, sa_ref, sb_ref, _y_ctx,
                     o_ref, kb_scr, vb_scr):
    kvh = pl.program_id(1)
    hd = ATTN_HEAD_DIM
    t = q_ref.shape[0]
    span = ATTN_BLOCK + 2 * ATTN_WINDOW
    k = _rope(_rms_w(k_ref[...], kn_ref[...]), cos_ref[...], sa_ref[...], sb_ref[...])
    kb_scr[...] = k.astype(BF16)
    vb_scr[...] = v_ref[...].astype(BF16)
    ckb = ck_ref[0, 0].astype(BF16)
    cvb = cv_ref[0, 0].astype(BF16)
    q_iota = lax.broadcasted_iota(jnp.int32, (ATTN_BLOCK, span), 0)
    k_iota = lax.broadcasted_iota(jnp.int32, (ATTN_BLOCK, span), 1)

    def blocks(j, carry):
        ch = []
        for sub in range(ATTN_LAT_NQB):
            q0 = pl.multiple_of((j * ATTN_LAT_NQB + sub) * ATTN_BLOCK, ATTN_BLOCK)
            ws = pl.multiple_of(jnp.clip(q0 - ATTN_WINDOW, 0, t - span), ATTN_BLOCK)
            rows = pl.ds(q0, ATTN_BLOCK)
            valid = jnp.abs((k_iota + ws) - (q_iota + q0)) <= ATTN_WINDOW
            cos, sa, sb = cos_ref[rows, :], sa_ref[rows, :], sb_ref[rows, :]
            for g in range(ATTN_GROUP):
                qb = _rope(_rms_w(q_ref[rows, g * hd:(g + 1) * hd], qn_ref[...]), cos, sa, sb).astype(BF16)
                ch.append(dict(g=g, rows=rows, qb=qb, valid=valid,
                               kw=kb_scr[pl.ds(ws, span), :], vw=vb_scr[pl.ds(ws, span), :]))
        for d in ch:
            d["s_win"] = jnp.where(d["valid"], _dot_nt(d["qb"], d["kw"]) * hd ** -0.5, NEG_INF)
            d["s_ctx"] = _dot_nt(d["qb"], ckb) * hd ** -0.5
        for d in ch:
            (e_win, e_ctx), den = _softmax_parts([d["s_win"], d["s_ctx"]], sink_ref[kvh * ATTN_GROUP + d["g"]])
            d["p"] = ((e_win / den).astype(BF16), (e_ctx / den).astype(BF16))
        for d in ch:
            d["o"] = _dot(d["p"][0], d["vw"]) + _dot(d["p"][1], cvb)
        for d in ch:
            o_ref[d["rows"], d["g"] * hd:(d["g"] + 1) * hd] = d["o"].astype(BF16)
        return carry

    lax.fori_loop(0, t // (ATTN_BLOCK * ATTN_LAT_NQB), blocks, 0)


def _attn_lat(qkv, cache_k, cache_v, slot, sink, q_norm, k_norm, y_ctx, n_seq, t, row_off):
    hd = ATTN_HEAD_DIM
    gw = ATTN_GROUP * hd
    kcol = ATTN_HEADS
    vcol = kcol + ATTN_KV_HEADS
    n_past = cache_k.shape[2]
    ck = cache_k.reshape(cache_k.shape[0], cache_k.shape[1], n_past, ATTN_KV_HEADS * hd)
    cv = cache_v.reshape(ck.shape)
    cos, sin_a, sin_b = _rope_tables(t)
    tab = pl.BlockSpec((t, hd), lambda b, h: (0, 0))
    return pl.pallas_call(
        _attn_lat_kernel,
        grid=(n_seq, ATTN_KV_HEADS),
        in_specs=[
            pl.BlockSpec(memory_space=pltpu.SMEM),
            pl.BlockSpec((t, gw), lambda b, h: (b + row_off, h)),
            pl.BlockSpec((t, hd), lambda b, h: (b + row_off, kcol + h)),
            pl.BlockSpec((t, hd), lambda b, h: (b + row_off, vcol + h)),
            pl.BlockSpec((1, 1, n_past, hd), lambda b, h: (b, slot, 0, h)),
            pl.BlockSpec((1, 1, n_past, hd), lambda b, h: (b, slot, 0, h)),
            pl.BlockSpec((1, hd), lambda b, h: (0, 0)),
            pl.BlockSpec((1, hd), lambda b, h: (0, 0)),
            tab, tab, tab,
            pl.BlockSpec(memory_space=pl.ANY),
        ],
        out_specs=pl.BlockSpec((t, gw), lambda b, h: (b + row_off, h)),
        out_shape=jax.ShapeDtypeStruct(y_ctx.shape, BF16),
        input_output_aliases={11: 0},
        scratch_shapes=[pltpu.VMEM((t, hd), BF16), pltpu.VMEM((t, hd), BF16)],
        compiler_params=_cparams(2),
        name="attn_lat",
    )(sink, qkv, qkv, qkv, ck, cv, q_norm[None, :], k_norm[None, :], cos, sin_a, sin_b, y_ctx)


def _cumsum_rows(x, reverse):
    n = x.shape[0]
    row = lax.broadcasted_iota(jnp.int32, x.shape, 0)
    sh = 1
    while sh < n:
        if reverse:
            x = x + jnp.where(row < n - sh, pltpu.roll(x, n - sh, 0), 0.0)
        else:
            x = x + jnp.where(row >= sh, pltpu.roll(x, sh, 0), 0.0)
        sh *= 2
    return x


def _log_sigmoid(x):
    return jnp.minimum(x, 0.0) - jnp.log(1.0 + jnp.exp(-jnp.abs(x)))


MLSTM_HP = 4


def _mlstm_kernel(has_init, emit_state, *refs):
    refs = list(refs)
    q_ref, k_ref, v_ref, og_ref, gt_ref, bg_ref, on_ref = refs[:7]
    refs = refs[7:]
    if has_init:
        c0_ref, n0_ref, m0_ref, _y_ctx = refs[:4]
        refs = refs[4:]
    y_ref = refs[0]
    refs = refs[1:]
    if emit_state:
        co_ref, no_ref, mo_ref = refs[:3]
        refs = refs[3:]
    hf_scr, hb_scr = refs

    L = MLSTM_CHUNK
    t = q_ref.shape[0]
    nc = t // L
    dk, dv = MLSTM_DK, MLSTM_DV
    row = lax.broadcasted_iota(jnp.int32, (L, L), 0)
    col = lax.broadcasted_iota(jnp.int32, (L, L), 1)
    lane = lax.broadcasted_iota(jnp.int32, (L, LANES), 1)

    def chunk_step(chains):
        st = []
        for z, hh, c, C, n, m in chains:
            rows = pl.ds(pl.multiple_of(c * L, L), L)
            qc = q_ref[rows, hh * dk:(hh + 1) * dk] * dk ** -0.5
            kc = k_ref[rows, hh * dk:(hh + 1) * dk]
            vc = v_ref[rows, hh * dv:(hh + 1) * dv]
            G = gt_ref[rows, hh * LANES:(hh + 1) * LANES] + bg_ref[hh]
            CUM = _cumsum_rows(_log_sigmoid(G), reverse=(z == 1))
            ci, cf = 2 * z, 2 * z + 1
            ZT = jnp.where(lane == cf, CUM, G).T
            i_row, cum_row = ZT[ci:ci + 1, :L], ZT[cf:cf + 1, :L]
            i_col, cum_col = G[:, ci:ci + 1], CUM[:, cf:cf + 1]
            mask = (col <= row) if z == 0 else (col >= row)
            dmat = jnp.where(mask, cum_col - cum_row + i_row, NEG_INF)
            inter = cum_col + m
            m_c = jnp.maximum(inter, jnp.max(dmat, axis=-1, keepdims=True))
            bl = cum_row[:, L - 1:L] if z == 0 else cum_row[:, 0:1]
            m_new = jnp.maximum(bl + m, jnp.max(bl - cum_row + i_row, axis=-1, keepdims=True))
            kw = kc * jnp.exp(bl - cum_col + i_col - m_new)
            st.append(dict(
                z=z, hh=hh, rows=rows, C=C, n=n, qc=qc, m_c=m_c, m_new=m_new,
                qb=qc.astype(BF16), kb=kc.astype(BF16), vb=vc.astype(BF16), Cb=C.astype(BF16),
                w=jnp.exp(dmat - m_c), a=jnp.exp(inter - m_c), decay=jnp.exp(bl + m - m_new),
                kw=kw, kwb=kw.astype(BF16)))
        for d in st:
            d["qk"] = _dot_nt(d["qb"], d["kb"])
            d["qC"] = _dot(d["qb"], d["Cb"])
            d["kv"] = _dot_tn(d["kwb"], d["vb"])
        for d in st:
            d["s"] = d["qk"] * d["w"]
        for d in st:
            d["sv"] = _dot(d["s"].astype(BF16), d["vb"])
        out = []
        for d in st:
            num = d["sv"] + d["a"] * d["qC"]
            den = jnp.sum(d["s"], axis=-1, keepdims=True) + d["a"] * jnp.sum(d["qc"] * d["n"], axis=-1, keepdims=True)
            scr = hf_scr if d["z"] == 0 else hb_scr
            scr[d["rows"], d["hh"] * dv:(d["hh"] + 1) * dv] = num / jnp.maximum(jnp.abs(den), jnp.exp(-d["m_c"]))
            out += [d["decay"] * d["C"] + d["kv"],
                    d["decay"] * d["n"] + jnp.sum(d["kw"], axis=0, keepdims=True),
                    d["m_new"]]
        return out

    def body(ci, carry):
        chains = []
        for hh in range(MLSTM_HP):
            Cf, nf, mf, Cb, nb, mb = carry[6 * hh:6 * hh + 6]
            chains.append((0, hh, ci, Cf, nf, mf))
            chains.append((1, hh, nc - 1 - ci, Cb, nb, mb))
        return tuple(chunk_step(chains))

    init = []
    for hh in range(MLSTM_HP):
        for z in range(2):
            if has_init:
                init += [c0_ref[0, z, hh], n0_ref[0, z, hh], m0_ref[0, z, hh][:, 0:1]]
            else:
                init += [jnp.zeros((dk, dv), F32), jnp.zeros((1, dk), F32), jnp.zeros((1, 1), F32)]
    final = lax.fori_loop(0, nc, body, tuple(init))

    for hh in range(MLSTM_HP):
        cols = slice(hh * dv, (hh + 1) * dv)
        hs = hf_scr[:, cols] + hb_scr[:, cols]
        y_ref[:, cols] = (_rms_w(hs, on_ref[:, cols]) * _sigmoid(og_ref[:, cols])).astype(BF16)
        if emit_state:
            for z in range(2):
                C, n, m = final[6 * hh + 3 * z:6 * hh + 3 * z + 3]
                co_ref[0, z, hh] = C
                no_ref[0, z, hh] = n
                mo_ref[0, z, hh] = jnp.broadcast_to(m, (1, LANES))


def _mlstm(proj, gates, b_gate_h, out_norm, state0, y_ctx, n_seq, t, row_off, emit_state):
    h_, dk, dv, hp = MLSTM_HEADS, MLSTM_DK, MLSTM_DV, MLSTM_HP
    nhb = h_ // hp
    has_init = state0 is not None
    in_specs = [
        pl.BlockSpec((t, hp * dk), lambda b, h: (b + row_off, h)),
        pl.BlockSpec((t, hp * dk), lambda b, h: (b + row_off, nhb + h)),
        pl.BlockSpec((t, hp * dv), lambda b, h: (b + row_off, nhb + h)),
        pl.BlockSpec((t, hp * dv), lambda b, h: (b + row_off, 2 * nhb + h)),
        pl.BlockSpec((t, hp * LANES), lambda b, h: (b + row_off, h)),
        pl.BlockSpec((hp, 1, LANES), lambda b, h: (h, 0, 0)),
        pl.BlockSpec((1, hp * dv), lambda b, h: (0, h)),
    ]
    args = [proj, proj, proj, proj, gates, b_gate_h[:, None, :], out_norm[None, :]]
    aliases = {}
    if has_init:
        c0, n0, m0 = state0
        in_specs += [
            pl.BlockSpec((1, 2, hp, dk, dv), lambda b, h: (b, 0, h, 0, 0)),
            pl.BlockSpec((1, 2, hp, 1, dk), lambda b, h: (b, 0, h, 0, 0)),
            pl.BlockSpec((1, 2, hp, 1, LANES), lambda b, h: (b, 0, h, 0, 0)),
            pl.BlockSpec(memory_space=pl.ANY),
        ]
        args += [c0, n0.reshape(n_seq, 2, h_, 1, dk),
                 jnp.broadcast_to(m0[..., None, None], (n_seq, 2, h_, 1, LANES)), y_ctx]
        aliases = {len(args) - 1: 0}
    out_specs = [pl.BlockSpec((t, hp * dv), lambda b, h: (b + row_off, h))]
    out_shape = [jax.ShapeDtypeStruct((N_TOK, D_MODEL), BF16)]
    if emit_state:
        out_specs += [
            pl.BlockSpec((1, 2, hp, dk, dv), lambda b, h: (b, 0, h, 0, 0)),
            pl.BlockSpec((1, 2, hp, 1, dk), lambda b, h: (b, 0, h, 0, 0)),
            pl.BlockSpec((1, 2, hp, 1, LANES), lambda b, h: (b, 0, h, 0, 0)),
        ]
        out_shape += [
            jax.ShapeDtypeStruct((n_seq, 2, h_, dk, dv), F32),
            jax.ShapeDtypeStruct((n_seq, 2, h_, 1, dk), F32),
            jax.ShapeDtypeStruct((n_seq, 2, h_, 1, LANES), F32),
        ]
    return pl.pallas_call(
        functools.partial(_mlstm_kernel, has_init, emit_state),
        grid=(n_seq, nhb),
        in_specs=in_specs,
        out_specs=out_specs,
        out_shape=out_shape,
        input_output_aliases=aliases,
        scratch_shapes=[pltpu.VMEM((t, hp * dv), F32), pltpu.VMEM((t, hp * dv), F32)],
        compiler_params=_cparams(2),
        name="mlstm",
    )(*args)


def _rwkv_mix_kernel(x_ref, xp_ref, xn_ref, shift_ref, scale_ref, g_ref, mu_ref, o_ref):
    i = pl.program_id(0)
    tm = x_ref.shape[0]
    g, sh, sc = g_ref[...], shift_ref[0], scale_ref[0]
    h = _modulated(x_ref[...], g, sh, sc)
    hp = _modulated(xp_ref[...], g, sh, sc)[7:8, :]
    hn = _modulated(xn_ref[...], g, sh, sc)[0:1, :]
    n_ctx_tiles = N_CTX_TOK // tm
    tiles_per_lat = DEC_SEQ // tm
    tile_in_seq = jnp.where(i < n_ctx_tiles, i % (SEQ // tm), (i - n_ctx_tiles) % tiles_per_lat)
    tiles_in_seq = jnp.where(i < n_ctx_tiles, SEQ // tm, tiles_per_lat)
    has_prev = (tile_in_seq > 0).astype(F32)
    has_next = (tile_in_seq < tiles_in_seq - 1).astype(F32)
    row = lax.broadcasted_iota(jnp.int32, h.shape, 0)
    prev = jnp.where(row == 0, hp * has_prev, pltpu.roll(h, 1, 0))
    nxt = jnp.where(row == tm - 1, hn * has_next, pltpu.roll(h, tm - 1, 0))
    cs = 0.5 * (prev + nxt) - h
    for p in range(o_ref.shape[0]):
        o_ref[p] = (h + cs * mu_ref[p:p + 1, :]).astype(BF16)


def _rwkv_mix(x, mods3, g_row, mu):
    tm = 256
    nb8 = tm // 8
    last = N_TOK // 8 - 1
    return pl.pallas_call(
        _rwkv_mix_kernel,
        grid=(N_TOK // tm,),
        in_specs=[
            pl.BlockSpec((tm, D_MODEL), lambda i: (i, 0)),
            pl.BlockSpec((8, D_MODEL), lambda i: (jnp.maximum(i * nb8 - 1, 0), 0)),
            pl.BlockSpec((8, D_MODEL), lambda i: (jnp.minimum((i + 1) * nb8, last), 0)),
            pl.BlockSpec((1, 1, D_MODEL), lambda i: (_mod_row(i, tm), 0, 3)),
            pl.BlockSpec((1, 1, D_MODEL), lambda i: (_mod_row(i, tm), 0, 4)),
            pl.BlockSpec((1, D_MODEL), lambda i: (0, 0)),
            pl.BlockSpec((6, D_MODEL), lambda i: (0, 0)),
        ],
        out_specs=pl.BlockSpec((6, tm, D_MODEL), lambda i: (0, i, 0)),
        out_shape=jax.ShapeDtypeStruct((6, N_TOK, D_MODEL), BF16),
        compiler_params=_cparams(1),
        name="rwkv_mix",
    )(x, x, x, mods3, mods3, g_row, mu)


def _lora_kernel(act, post, x_ref, a_ref, b_ref, bias_ref, o_ref):
    u = _dot(x_ref[0], a_ref[0])
    if act == "tanh":
        u = jnp.tanh(u)
    elif act == "sigmoid":
        u = _sigmoid(u)
    y = _dot(u.astype(BF16), b_ref[0]) + bias_ref[0]
    if post == "log_decay":
        sp = jnp.maximum(-y, 0.0) + jnp.log(1.0 + jnp.exp(-jnp.abs(y)))
        y = -jnp.exp(-sp - 0.5)
    elif post == "sigmoid":
        y = _sigmoid(y)
    o_ref[0] = y


def _lora(xs, p, wa, wb, bias, act, post, name):
    tm = 512
    nz, _, r = wa.shape
    return pl.pallas_call(
        functools.partial(_lora_kernel, act, post),
        grid=(nz, N_TOK // tm),
        in_specs=[
            pl.BlockSpec((1, tm, D_MODEL), lambda z, i: (p, i, 0)),
            pl.BlockSpec((1, D_MODEL, r), lambda z, i: (z, 0, 0)),
            pl.BlockSpec((1, r, D_MODEL), lambda z, i: (z, 0, 0)),
            pl.BlockSpec((1, 1, D_MODEL), lambda z, i: (z, 0, 0)),
        ],
        out_specs=pl.BlockSpec((1, tm, D_MODEL), lambda z, i: (z, i, 0)),
        out_shape=jax.ShapeDtypeStruct((nz, N_TOK, D_MODEL), F32),
        compiler_params=_cparams(2),
        name=name,
    )(xs, wa, wb, bias)


def _rkv_kernel(x_ref, w_ref, o_ref):
    x = x_ref[0]
    for c in range(o_ref.shape[2] // MM_COLS):
        cols = slice(c * MM_COLS, (c + 1) * MM_COLS)
        o_ref[0, :, cols] = _dot(x, w_ref[0, :, cols])


def _rkv(xs, w_rkv):
    tm = 512
    return pl.pallas_call(
        _rkv_kernel,
        grid=(3, N_TOK // tm),
        in_specs=[pl.BlockSpec((1, tm, D_MODEL), lambda p, i: (p, i, 0)),
                  pl.BlockSpec((1, D_MODEL, D_MODEL), lambda p, i: (p, 0, 0))],
        out_specs=pl.BlockSpec((1, tm, D_MODEL), lambda p, i: (p, i, 0)),
        out_shape=jax.ShapeDtypeStruct((3, N_TOK, D_MODEL), F32),
        compiler_params=_cparams(2),
        name="rwkv_rkv",
    )(xs, w_rkv)


def _head_sum(x, first_head):
    s0 = jnp.sum(jnp.where(first_head, x, 0.0), axis=-1, keepdims=True)
    s1 = jnp.sum(jnp.where(first_head, 0.0, x), axis=-1, keepdims=True)
    return jnp.where(first_head, s0, s1)


RWKV_PP = 4


def _rwkv_kernel(has_init, emit_state, *refs):
    refs = list(refs)
    r_ref, k_ref, v_ref, lw_ref, a_ref, g_ref, kk_w_ref, ka_ref, rk_ref, lng_ref, lnb_ref = refs[:11]
    refs = refs[11:]
    if has_init:
        s0_ref, _y_ctx = refs[:2]
        refs = refs[2:]
    y_ref = refs[0]
    refs = refs[1:]
    if emit_state:
        so_ref = refs[0]
        refs = refs[1:]
    kk_scr, yf_scr, yb_scr = refs

    L = RWKV_CHUNK
    N = RWKV_HEAD
    t = r_ref.shape[1]
    nc = t // L
    row = lax.broadcasted_iota(jnp.int32, (L, L), 0)
    col = lax.broadcasted_iota(jnp.int32, (L, L), 1)
    first_head = lax.broadcasted_iota(jnp.int32, (L, LANES), 1) < N
    sq_row = lax.broadcasted_iota(jnp.int32, (LANES, LANES), 0)
    sq_col = lax.broadcasted_iota(jnp.int32, (LANES, LANES), 1)
    same_head = (sq_row < N) == (sq_col < N)

    first_head_t = lax.broadcasted_iota(jnp.int32, (t, LANES), 1) < N

    for pp in range(RWKV_PP):
        lanes = slice(pp * LANES, (pp + 1) * LANES)
        kkx = k_ref[0, :, lanes] * kk_w_ref[:, lanes]
        kk_scr[:, lanes] = kkx / jnp.maximum(jnp.sqrt(_head_sum(kkx * kkx, first_head_t)), 1e-12)

    def chunk_step(chains):
        heads = (first_head, jnp.logical_not(first_head))
        pick = lambda x0, x1: jnp.where(first_head, x0, x1)
        st = []
        for z, pp, c, S in chains:
            lanes = slice(pp * LANES, (pp + 1) * LANES)
            sl = pl.ds(pl.multiple_of(c * L, L), L)
            r_, k_, v_ = r_ref[0, sl, lanes], k_ref[0, sl, lanes], v_ref[0, sl, lanes]
            lw, a_ = lw_ref[z, sl, lanes], a_ref[z, sl, lanes]
            kk = kk_scr[sl, lanes]
            kd = k_ * (1.0 + (a_ - 1.0) * ka_ref[:, lanes])
            b_ = kk * a_
            cum = _cumsum_rows(lw, reverse=(z == 1))
            cum_end = cum[L - 1:L, :] if z == 0 else cum[0:1, :]
            kp = (kk * jnp.exp(cum - lw)).astype(BF16)
            rp = (r_ * jnp.exp(cum)).astype(BF16)
            inv = jnp.exp(-cum)
            to_end = jnp.exp(cum_end - cum)
            zero = jnp.zeros_like(kp)
            st.append(dict(
                z=z, pp=pp, sl=sl, lanes=lanes, S=S, kp=kp, rp=rp, cum_end=cum_end,
                kt=(kd * inv).astype(BF16), bt=(b_ * inv).astype(BF16),
                kb_end=jnp.concatenate([kd * to_end, -(b_ * to_end)], axis=0).astype(BF16),
                vb=v_.astype(BF16), Sb=S.astype(BF16),
                probe=[jnp.concatenate([jnp.where(hm, kp, zero), jnp.where(hm, rp, zero)], axis=0) for hm in heads],
                strict=(col < row) if z == 0 else (col > row),
                incl=(col <= row) if z == 0 else (col >= row)))
        for d in st:
            d["gk"] = [_dot_nt(pr, d["kt"]) for pr in d["probe"]]
            d["gb"] = [_dot_nt(pr, d["bt"]) for pr in d["probe"]]
        for d in st:
            d["g1"] = [jnp.where(d["strict"], g[:L], 0.0).astype(BF16) for g in d["gk"]]
            d["n"] = [jnp.where(d["strict"], g[:L], 0.0).astype(BF16) for g in d["gb"]]
            d["g3"] = [jnp.where(d["incl"], g[L:], 0.0).astype(BF16) for g in d["gk"]]
            d["g4"] = [jnp.where(d["incl"], g[L:], 0.0).astype(BF16) for g in d["gb"]]
        for d in st:
            d["g1v"] = [_dot(g, d["vb"]) for g in d["g1"]]
            d["kps"] = _dot_nt(d["kp"], d["Sb"])
            d["rps"] = _dot_nt(d["rp"], d["Sb"])
        for d in st:
            d["u"] = d["kps"] + pick(*d["g1v"])
        sign = -1.0
        p = 1
        while p < L:
            for d in st:
                ub = d["u"].astype(BF16)
                d["nu"] = [_dot(m, ub) for m in d["n"]]
                if 2 * p < L:
                    d["n"] = [_dot(m, m).astype(BF16) for m in d["n"]]
            for d in st:
                d["u"] = d["u"] + sign * pick(*d["nu"])
            sign = 1.0
            p *= 2
        for d in st:
            ub = d["u"].astype(BF16)
            d["y3"] = [_dot(g, d["vb"]) for g in d["g3"]]
            d["y4"] = [_dot(g, ub) for g in d["g4"]]
            d["upd"] = _dot_tn(jnp.concatenate([d["vb"], ub], axis=0), d["kb_end"])
        out = []
        for d in st:
            y = d["rps"] + pick(d["y3"][0] - d["y4"][0], d["y3"][1] - d["y4"][1])
            (yf_scr if d["z"] == 0 else yb_scr)[d["sl"], d["lanes"]] = y
            out.append(d["S"] * jnp.exp(d["cum_end"]) + jnp.where(same_head, d["upd"], 0.0))
        return out

    def body(ci, carry):
        chains = []
        for pp in range(RWKV_PP):
            chains.append((0, pp, ci, carry[2 * pp]))
            chains.append((1, pp, nc - 1 - ci, carry[2 * pp + 1]))
        return tuple(chunk_step(chains))

    init = []
    for pp in range(RWKV_PP):
        for z in range(2):
            if has_init:
                s = s0_ref[0, z, pp]
                init.append(jnp.where(sq_row < N, s, pltpu.roll(s, N, 1)))
            else:
                init.append(jnp.zeros((LANES, LANES), F32))
    final = lax.fori_loop(0, nc, body, tuple(init))

    for pp in range(RWKV_PP):
        lanes = slice(pp * LANES, (pp + 1) * LANES)
        r_, k_, v_ = r_ref[0, :, lanes], k_ref[0, :, lanes], v_ref[0, :, lanes]
        y = yf_scr[:, lanes] + yb_scr[:, lanes]
        mean = _head_sum(y, first_head_t) * (1.0 / N)
        yc = y - mean
        var = _head_sum(yc * yc, first_head_t) * (1.0 / N)
        yn = yc * lax.rsqrt(var + RWKV_GN_EPS) * lng_ref[:, lanes] + lnb_ref[:, lanes]
        ka = ka_ref[:, lanes]
        kd_sum = k_ * (1.0 + (a_ref[0, :, lanes] - 1.0) * ka) + k_ * (1.0 + (a_ref[1, :, lanes] - 1.0) * ka)
        bonus = _head_sum(r_ * kd_sum * rk_ref[:, lanes], first_head_t) * v_
        y_ref[:, lanes] = ((yn + bonus) * g_ref[:, lanes]).astype(BF16)
        if emit_state:
            for z in range(2):
                S = final[2 * pp + z]
                so_ref[0, z, pp] = (S + pltpu.roll(S, N, 1))[:, :N]


def _rwkv(rkv, lw, a, g, k_k, k_a, r_k, ln_g, ln_b, s0, y_ctx, n_seq, t, row_off, emit_state):
    has_init = s0 is not None
    w = RWKV_PP * LANES
    tok = lambda lead, first: pl.BlockSpec((lead, t, w), lambda b, p: (first, b + row_off, p))
    par = pl.BlockSpec((1, w), lambda b, p: (0, p))
    in_specs = [tok(1, 0), tok(1, 1), tok(1, 2), tok(2, 0), tok(2, 0),
                pl.BlockSpec((t, w), lambda b, p: (b + row_off, p)),
                par, par, par, par, par]
    args = [rkv, rkv, rkv, lw, a, g, k_k[None, :], k_a[None, :], r_k.reshape(1, D_MODEL), ln_g[None, :], ln_b[None, :]]
    aliases = {}
    if has_init:
        in_specs += [pl.BlockSpec((1, 2, RWKV_PP, LANES, LANES), lambda b, p: (b, 0, p, 0, 0)),
                     pl.BlockSpec(memory_space=pl.ANY)]
        args += [s0, y_ctx]
        aliases = {len(args) - 1: 0}
    out_specs = [pl.BlockSpec((t, w), lambda b, p: (b + row_off, p))]
    out_shape = [jax.ShapeDtypeStruct((N_TOK, D_MODEL), BF16)]
    if emit_state:
        out_specs.append(pl.BlockSpec((1, 2, RWKV_PP, LANES, RWKV_HEAD), lambda b, p: (b, 0, p, 0, 0)))
        out_shape.append(jax.ShapeDtypeStruct((n_seq, 2, RWKV_PAIRS, LANES, RWKV_HEAD), F32))
    return pl.pallas_call(
        functools.partial(_rwkv_kernel, has_init, emit_state),
        grid=(n_seq, RWKV_PAIRS // RWKV_PP),
        in_specs=in_specs,
        out_specs=out_specs,
        out_shape=out_shape,
        input_output_aliases=aliases,
        scratch_shapes=[pltpu.VMEM((t, w), F32)] * 3,
        compiler_params=_cparams(2),
        name="rwkv",
    )(*args)


def _attn_layer(x, mods3, g_row, cache_k, cache_v, slot, w_qkv, q_norm, k_norm, sink, w_o):
    qkv = _mm_mod(x, mods3, 3, g_row, w_qkv.astype(BF16), 512, "attn_qkv")
    y, k_ctx, v_ctx = _attn_ctx(qkv, sink, q_norm, k_norm, BATCH, SEQ)
    y = _attn_lat(qkv, cache_k, cache_v, slot, sink, q_norm, k_norm, y, DEC_BATCH, DEC_SEQ, N_CTX_TOK // DEC_SEQ)
    x = _mm_res(y, w_o.astype(BF16), x, mods3, 5, "attn_out")
    shape = (BATCH, SEQ, ATTN_KV_HEADS, ATTN_HEAD_DIM)
    return x, k_ctx.reshape(shape), v_ctx.reshape(shape)


def _mlstm_layer(x, mods3, g_row, state_lat, w_in, w_gate, b_gate, out_norm, w_o):
    h_ = MLSTM_HEADS
    proj = _mm_mod(x, mods3, 3, g_row, w_in.astype(BF16), 256, "mlstm_in")
    wg = jnp.transpose(w_gate.reshape(D_MODEL, 4, h_), (0, 2, 1))
    wg = jnp.pad(wg, ((0, 0), (0, 0), (0, LANES - 4))).reshape(D_MODEL, h_ * LANES).astype(BF16)
    bg = jnp.pad(b_gate.reshape(4, h_).T, ((0, 0), (0, LANES - 4)))
    gates = _mm_mod(x, mods3, 3, g_row, wg, 512, "mlstm_gates")
    y, C, n, m = _mlstm(proj, gates, bg, out_norm, None, None, BATCH, SEQ, 0, True)
    (y,) = _mlstm(proj, gates, bg, out_norm, state_lat, y, DEC_BATCH, DEC_SEQ, N_CTX_TOK // DEC_SEQ, False)
    x = _mm_res(y, w_o.astype(BF16), x, mods3, 5, "mlstm_out")
    return x, C, n[:, :, :, 0, :], m[:, :, :, 0, 0]


def _rwkv_layer(x, mods3, g_row, s0_lat, mu, w_rkv, w0, wA, wB, a0, aA, aB, gA, gB, k_k, k_a, r_k, ln_g, ln_b, w_o):
    xs = _rwkv_mix(x, mods3, g_row, mu)
    rkv = _rkv(xs, w_rkv.astype(BF16))
    pad_r = lambda w, axis: jnp.pad(w, [(0, 0) if ax != axis else (0, LANES - w.shape[axis]) for ax in range(w.ndim)])
    lw = _lora(xs, 3, pad_r(wA, 2).astype(BF16), pad_r(wB, 1).astype(BF16), w0[:, None, :], "tanh", "log_decay", "rwkv_w")
    a = _lora(xs, 4, pad_r(aA, 2).astype(BF16), pad_r(aB, 1).astype(BF16), a0[:, None, :], "none", "sigmoid", "rwkv_a")
    g = _lora(xs, 5, gA[None].astype(BF16), gB[None].astype(BF16), jnp.zeros((1, 1, D_MODEL), F32), "sigmoid", "none", "rwkv_g")[0]
    s0 = jnp.pad(s0_lat.reshape(DEC_BATCH, 2, RWKV_PAIRS, LANES, RWKV_HEAD), ((0, 0),) * 4 + ((0, LANES - RWKV_HEAD),))
    y, S = _rwkv(rkv, lw, a, g, k_k, k_a, r_k, ln_g, ln_b, None, None, BATCH, SEQ, 0, True)
    (y,) = _rwkv(rkv, lw, a, g, k_k, k_a, r_k, ln_g, ln_b, s0, y, DEC_BATCH, DEC_SEQ, N_CTX_TOK // DEC_SEQ, False)
    x = _mm_res(y, w_o.astype(BF16), x, mods3, 5, "rwkv_out")
    return x, S.reshape(BATCH, 2, RWKV_HEADS, RWKV_HEAD, RWKV_HEAD)


def kernel(x_prompt, x_sample, cache_k, cache_v, state_mlstm_C, state_mlstm_n, state_mlstm_m, state_rwkv, c, c_ctx, mod_w, mod_b, norm_g, ffn_w_in, ffn_w_out, attn_w_qkv, attn_q_norm, attn_k_norm, attn_sink, attn_w_o, mlstm_w_in, mlstm_w_gate, mlstm_b_gate, mlstm_out_norm, mlstm_w_o, rwkv_mu, rwkv_w_rkv, rwkv_w0, rwkv_wA, rwkv_wB, rwkv_a0, rwkv_aA, rwkv_aB, rwkv_gA, rwkv_gB, rwkv_k_k, rwkv_k_a, rwkv_r_k, rwkv_ln_g, rwkv_ln_b, rwkv_w_o):
    x = jnp.concatenate([x_prompt.reshape(N_CTX_TOK, D_MODEL), x_sample.reshape(N_LAT_TOK, D_MODEL)], axis=0)
    cond8 = jnp.concatenate([c_ctx[None, :], c, jnp.zeros((MOD_ROWS - 1 - DEC_BATCH, D_MODEL), F32)], axis=0)
    mods = _adaln_all(cond8, mod_w, mod_b)
    ffn_w_in_b = ffn_w_in.astype(BF16)
    ffn_w_out_b = ffn_w_out.astype(BF16)

    new_k, new_v, new_C, new_n, new_m, new_S = [], [], [], [], [], []
    for i in range(DEPTH):
        kind, slot = i % N_MIXERS, i // N_MIXERS
        mods3 = mods[i].reshape(MOD_ROWS, 1, N_MOD * D_MODEL)
        x = _ffn(x, mods3, 0, norm_g[i, 0][None, :], ffn_w_in_b, ffn_w_out_b, i, 0)
        g_row = norm_g[i, 1][None, :]
        if kind == 0:
            x, k_ctx, v_ctx = _attn_layer(x, mods3, g_row, cache_k, cache_v, slot, attn_w_qkv[slot], attn_q_norm[slot],
                                          attn_k_norm[slot], attn_sink[slot], attn_w_o[slot])
            new_k.append(k_ctx)
            new_v.append(v_ctx)
        elif kind == 1:
            state_lat = (state_mlstm_C[:, slot], state_mlstm_n[:, slot], state_mlstm_m[:, slot])
            x, C, n, m = _mlstm_layer(x, mods3, g_row, state_lat, mlstm_w_in[slot], mlstm_w_gate[slot], mlstm_b_gate[slot],
                                      mlstm_out_norm[slot], mlstm_w_o[slot])
            new_C.append(C)
            new_n.append(n)
            new_m.append(m)
        else:
            x, S = _rwkv_layer(x, mods3, g_row, state_rwkv[:, slot], rwkv_mu[slot], rwkv_w_rkv[slot], rwkv_w0[slot],
                               rwkv_wA[slot], rwkv_wB[slot], rwkv_a0[slot], rwkv_aA[slot], rwkv_aB[slot], rwkv_gA[slot],
                               rwkv_gB[slot], rwkv_k_k[slot], rwkv_k_a[slot], rwkv_r_k[slot], rwkv_ln_g[slot],
                               rwkv_ln_b[slot], rwkv_w_o[slot])
            new_S.append(S)
        x = _ffn(x, mods3, 6, norm_g[i, 2][None, :], ffn_w_in_b, ffn_w_out_b, i, 1)

    y_prompt = x[:N_CTX_TOK].reshape(BATCH, SEQ, D_MODEL)
    y_sample = x[N_CTX_TOK:].reshape(DEC_BATCH, DEC_SEQ, D_MODEL)
    return (y_prompt, y_sample, jnp.stack(new_k, axis=1), jnp.stack(new_v, axis=1),
            jnp.stack(new_C, axis=1), jnp.stack(new_n, axis=1), jnp.stack(new_m, axis=1), jnp.stack(new_S, axis=1))
```

```python
import functools

import jax
import jax.numpy as jnp
from jax import lax
from jax.experimental import pallas as pl
from jax.experimental.pallas import tpu as pltpu

D_MODEL = 2048
BATCH = 32
SEQ = 256
DEPTH = 4
DEC_BATCH = 4
DEC_SEQ = 1024
PAST_LEN = 256
GRID_W = 64
N_MIXERS = 3
N_MOD = 9
D_FF = 5632
EPS = 1e-6

ATTN_HEADS = 16
ATTN_KV_HEADS = 4
ATTN_GROUP = ATTN_HEADS // ATTN_KV_HEADS
ATTN_HEAD_DIM = D_MODEL // ATTN_HEADS
ATTN_WINDOW = 128
ATTN_BLOCK = 128
ROPE_THETA = 10000.0
QKV_DIM = (ATTN_HEADS + 2 * ATTN_KV_HEADS) * ATTN_HEAD_DIM

MLSTM_HEADS = 8
MLSTM_DV = D_MODEL // MLSTM_HEADS
MLSTM_DK = MLSTM_DV // 2
MLSTM_CHUNK = 64
MLSTM_IN_DIM = 2 * MLSTM_HEADS * MLSTM_DK + 2 * D_MODEL

RWKV_HEAD = 64
RWKV_HEADS = D_MODEL // RWKV_HEAD
RWKV_PAIRS = RWKV_HEADS // 2
RWKV_CHUNK = 64
RWKV_GN_EPS = 64e-5

N_CTX_TOK = BATCH * SEQ
N_LAT_TOK = DEC_BATCH * DEC_SEQ
N_TOK = N_CTX_TOK + N_LAT_TOK
MOD_ROWS = 8
LANES = 128

VMEM_LIMIT = 60 * 1024 * 1024
BF16 = jnp.bfloat16
F32 = jnp.float32
NEG_INF = float("-inf")
MM_COLS = 512


def _cparams(n_axes):
    return pltpu.CompilerParams(dimension_semantics=("arbitrary",) * n_axes, vmem_limit_bytes=VMEM_LIMIT)


def _mod_row(i, tm):
    n_ctx_tiles = N_CTX_TOK // tm
    tiles_per_lat = DEC_SEQ // tm
    return jnp.where(i < n_ctx_tiles, 0, 1 + (i - n_ctx_tiles) // tiles_per_lat)


def _mod_spec(slot, tm):
    return pl.BlockSpec((1, 1, D_MODEL), lambda i, j, s=slot: (_mod_row(i, tm), 0, s))


def _dot(a, b):
    return jnp.dot(a, b, preferred_element_type=F32)


def _dot_nt(a, b):
    return lax.dot_general(a, b, (((1,), (1,)), ((), ())), preferred_element_type=F32)


def _dot_tn(a, b):
    return lax.dot_general(a, b, (((0,), (0,)), ((), ())), preferred_element_type=F32)


def _sigmoid(x):
    return jax.nn.sigmoid(x)


def _modulated(x, g, shift, scale):
    ms = jnp.mean(x * x, axis=-1, keepdims=True)
    y = x * lax.rsqrt(ms + EPS) * g
    return y * (1.0 + scale) + shift


def _adaln_kernel(cond_ref, w_ref, b_ref, o_ref):
    c = cond_ref[...]
    a = (c * _sigmoid(c)).astype(BF16)
    o_ref[0] = _dot(a, w_ref[0].astype(BF16)) + b_ref[0]


def _adaln_all(cond8, mod_w, mod_b):
    tn = 1024
    n_out = N_MOD * D_MODEL
    return pl.pallas_call(
        _adaln_kernel,
        grid=(DEPTH, n_out // tn),
        in_specs=[
            pl.BlockSpec((MOD_ROWS, D_MODEL), lambda l, j: (0, 0)),
            pl.BlockSpec((1, D_MODEL, tn), lambda l, j: (l, 0, j)),
            pl.BlockSpec((1, 1, tn), lambda l, j: (l, 0, j)),
        ],
        out_specs=pl.BlockSpec((1, MOD_ROWS, tn), lambda l, j: (l, 0, j)),
        out_shape=jax.ShapeDtypeStruct((DEPTH, MOD_ROWS, n_out), F32),
        compiler_params=_cparams(2),
        name="adaln",
    )(cond8, mod_w, mod_b.reshape(DEPTH, 1, n_out))


FFN_TM = 1024
FFN_TF = 512
ROW_CHUNK = 32


def _cast_rows(total, n_steps):
    r = 16
    while total % r or total // r > n_steps:
        r *= 2
    return r


def _ffn_kernel(has_next, *refs):
    x_ref, shift_ref, scale_ref, gate_ref, g_ref, wg_ref, wu_ref, wo_ref = refs[:8]
    refs = refs[8:]
    if has_next:
        nin_ref, nout_ref, o_ref, cin_ref, cout_ref, h_scr = refs
    else:
        o_ref, h_scr = refs
    f = pl.program_id(1)

    @pl.when(f == 0)
    def _():
        def rows(c, carry):
            r = pl.ds(pl.multiple_of(c * ROW_CHUNK, ROW_CHUNK), ROW_CHUNK)
            h_scr[r, :] = _modulated(x_ref[r, :], g_ref[...], shift_ref[0], scale_ref[0]).astype(BF16)
            return carry
        lax.fori_loop(0, x_ref.shape[0] // ROW_CHUNK, rows, 0, unroll=8)

    h = h_scr[...]
    a = _dot(h, wg_ref[...])
    b = _dot(h, wu_ref[...])
    act = ((a * _sigmoid(a)) * b).astype(BF16)
    if has_next:
        cin_ref[...] = nin_ref[0, 0].astype(BF16)
        cout_ref[...] = nout_ref[0, 0].astype(BF16)
    for c in range(o_ref.shape[1] // MM_COLS):
        cols = slice(c * MM_COLS, (c + 1) * MM_COLS)
        o_ref[:, cols] = jnp.where(f == 0, 0.0, o_ref[:, cols]) + _dot(act, wo_ref[:, cols])

    @pl.when(f == pl.num_programs(1) - 1)
    def _():
        o_ref[...] = x_ref[...] + (0.5 * gate_ref[0]) * o_ref[...]


def _ffn(x, mods3, slot, g_row, w_in, w_out, nxt):
    tm, tf = FFN_TM, FFN_TF
    nf = D_FF // tf
    in_specs = [
        pl.BlockSpec((tm, D_MODEL), lambda i, f: (i, 0)),
        _mod_spec(slot, tm), _mod_spec(slot + 1, tm), _mod_spec(slot + 2, tm),
        pl.BlockSpec((1, D_MODEL), lambda i, f: (0, 0)),
        pl.BlockSpec((D_MODEL, tf), lambda i, f: (0, f)),
        pl.BlockSpec((D_MODEL, tf), lambda i, f: (0, f + nf)),
        pl.BlockSpec((tf, D_MODEL), lambda i, f: (f, 0)),
    ]
    args = [x, mods3, mods3, mods3, g_row, w_in, w_in, w_out]
    out_specs = [pl.BlockSpec((tm, D_MODEL), lambda i, f: (i, 0))]
    out_shape = [jax.ShapeDtypeStruct((N_TOK, D_MODEL), F32)]
    if nxt is not None:
        n_in, n_out, layer, which = nxt
        n_steps = (N_TOK // tm) * nf
        rows_in, rows_out = _cast_rows(D_MODEL, n_steps), _cast_rows(D_FF, n_steps)
        last_in, last_out = D_MODEL // rows_in - 1, D_FF // rows_out - 1
        blk_in = lambda i, f: jnp.minimum(i * nf + f, last_in)
        blk_out = lambda i, f: jnp.minimum(i * nf + f, last_out)
        in_specs += [
            pl.BlockSpec((1, 1, rows_in, 2 * D_FF), lambda i, f: (layer, which, blk_in(i, f), 0)),
            pl.BlockSpec((1, 1, rows_out, D_MODEL), lambda i, f: (layer, which, blk_out(i, f), 0)),
        ]
        args += [n_in, n_out]
        out_specs += [
            pl.BlockSpec((rows_in, 2 * D_FF), lambda i, f: (blk_in(i, f), 0)),
            pl.BlockSpec((rows_out, D_MODEL), lambda i, f: (blk_out(i, f), 0)),
        ]
        out_shape += [jax.ShapeDtypeStruct((D_MODEL, 2 * D_FF), BF16), jax.ShapeDtypeStruct((D_FF, D_MODEL), BF16)]
    return pl.pallas_call(
        functools.partial(_ffn_kernel, nxt is not None),
        grid=(N_TOK // tm, nf),
        in_specs=in_specs,
        out_specs=out_specs,
        out_shape=out_shape,
        scratch_shapes=[pltpu.VMEM((tm, D_MODEL), BF16)],
        compiler_params=_cparams(2),
        name="ffn",
    )(*args)


def _resident(shape):
    return pl.BlockSpec(shape, lambda *_: (0,) * len(shape), pipeline_mode=pl.Buffered(1))


def _mm_mod_kernel(x_ref, shift_ref, scale_ref, g_ref, w_ref, o_ref):
    h = _modulated(x_ref[...], g_ref[...], shift_ref[0], scale_ref[0]).astype(BF16)
    for c in range(o_ref.shape[1] // MM_COLS):
        cols = slice(c * MM_COLS, (c + 1) * MM_COLS)
        o_ref[:, cols] = _dot(h, w_ref[:, cols])


def _mm_mod(x, mods3, slot, g_row, w, tm, name):
    n = w.shape[1]
    mod = lambda s: pl.BlockSpec((1, 1, D_MODEL), lambda i: (_mod_row(i, tm), 0, s))
    return pl.pallas_call(
        _mm_mod_kernel,
        grid=(N_TOK // tm,),
        in_specs=[
            pl.BlockSpec((tm, D_MODEL), lambda i: (i, 0)),
            mod(slot), mod(slot + 1),
            _resident((1, D_MODEL)),
            _resident((D_MODEL, n)),
        ],
        out_specs=pl.BlockSpec((tm, n), lambda i: (i, 0)),
        out_shape=jax.ShapeDtypeStruct((N_TOK, n), F32),
        compiler_params=_cparams(1),
        name=name,
    )(x, mods3, mods3, g_row, w)


def _mm_res_kernel(y_ref, w_ref, x_ref, gate_ref, o_ref):
    y = y_ref[...]
    for c in range(o_ref.shape[1] // MM_COLS):
        cols = slice(c * MM_COLS, (c + 1) * MM_COLS)
        o_ref[:, cols] = x_ref[:, cols] + gate_ref[0][:, cols] * _dot(y, w_ref[:, cols])


def _mm_res(y, w, x, mods3, slot, name):
    tm = 512
    k = y.shape[1]
    return pl.pallas_call(
        _mm_res_kernel,
        grid=(N_TOK // tm,),
        in_specs=[
            pl.BlockSpec((tm, k), lambda i: (i, 0)),
            _resident((k, D_MODEL)),
            pl.BlockSpec((tm, D_MODEL), lambda i: (i, 0)),
            pl.BlockSpec((1, 1, D_MODEL), lambda i: (_mod_row(i, tm), 0, slot)),
        ],
        out_specs=pl.BlockSpec((tm, D_MODEL), lambda i: (i, 0)),
        out_shape=jax.ShapeDtypeStruct((N_TOK, D_MODEL), F32),
        compiler_params=_cparams(1),
        name=name,
    )(y, w, x, mods3)


def _token_zeros():
    return jnp.zeros((N_TOK, D_MODEL), BF16)


def _rms_w(x, w):
    return x * lax.rsqrt(jnp.mean(x * x, axis=-1, keepdims=True) + EPS) * w


def _softmax_parts(scores, sink):
    m = sink
    for s in scores:
        m = jnp.maximum(jnp.max(s, axis=-1, keepdims=True), m)
    es = [jnp.exp(s - m) for s in scores]
    den = jnp.exp(sink - m)
    for e in es:
        den = den + jnp.sum(e, axis=-1, keepdims=True)
    return es, den


def _attn_ctx_kernel(sink_ref, q_ref, k_ref, v_ref, qn_ref, kn_ref, _y_init, o_ref, ko_ref, vo_ref):
    kvh = pl.program_id(1)
    hd = ATTN_HEAD_DIM
    k = _rms_w(k_ref[...], kn_ref[...])
    v = v_ref[...]
    ko_ref[...] = k
    vo_ref[...] = v
    kb = k.astype(BF16)
    vb = v.astype(BF16)
    qs = [_rms_w(q_ref[:, g * hd:(g + 1) * hd], qn_ref[...]).astype(BF16) for g in range(ATTN_GROUP)]
    ss = [_dot_nt(q, kb) * hd ** -0.5 for q in qs]
    ps = []
    for g, s in enumerate(ss):
        (e,), den = _softmax_parts([s], sink_ref[kvh * ATTN_GROUP + g])
        ps.append((e / den).astype(BF16))
    os_ = [_dot(p, vb) for p in ps]
    for g, o in enumerate(os_):
        o_ref[:, g * hd:(g + 1) * hd] = o.astype(BF16)


def _attn_ctx(qkv, sink, q_norm, k_norm, n_seq, t):
    hd = ATTN_HEAD_DIM
    gw = ATTN_GROUP * hd
    kcol = ATTN_HEADS * hd // hd
    vcol = kcol + ATTN_KV_HEADS
    n_tok = n_seq * t
    return pl.pallas_call(
        _attn_ctx_kernel,
        grid=(n_seq, ATTN_KV_HEADS),
        in_specs=[
            pl.BlockSpec(memory_space=pltpu.SMEM),
            pl.BlockSpec((t, gw), lambda b, h: (b, h)),
            pl.BlockSpec((t, hd), lambda b, h: (b, kcol + h)),
            pl.BlockSpec((t, hd), lambda b, h: (b, vcol + h)),
            pl.BlockSpec((1, hd), lambda b, h: (0, 0)),
            pl.BlockSpec((1, hd), lambda b, h: (0, 0)),
            pl.BlockSpec(memory_space=pl.ANY),
        ],
        out_specs=[
            pl.BlockSpec((t, gw), lambda b, h: (b, h)),
            pl.BlockSpec((t, hd), lambda b, h: (b, h)),
            pl.BlockSpec((t, hd), lambda b, h: (b, h)),
        ],
        input_output_aliases={6: 0},
        out_shape=[
            jax.ShapeDtypeStruct((N_TOK, ATTN_HEADS * hd), BF16),
            jax.ShapeDtypeStruct((n_tok, ATTN_KV_HEADS * hd), F32),
            jax.ShapeDtypeStruct((n_tok, ATTN_KV_HEADS * hd), F32),
        ],
        compiler_params=_cparams(2),
        name="attn_ctx",
    )(sink, qkv, qkv, qkv, q_norm[None, :], k_norm[None, :], _token_zeros())


def _rope(x, cos, sin_a, sin_b):
    quarter = ATTN_HEAD_DIM // 4
    return x * cos + pltpu.roll(x, ATTN_HEAD_DIM - quarter, 1) * sin_a + pltpu.roll(x, quarter, 1) * sin_b


def _rope_tables(t):
    hd = ATTN_HEAD_DIM
    quarter = hd // 4
    pos = jnp.arange(t)
    row = (pos // GRID_W).astype(F32)
    col = (pos % GRID_W).astype(F32)
    inv_freq = ROPE_THETA ** (-jnp.arange(quarter, dtype=F32) / quarter)
    lane = jnp.arange(hd)
    ang = jnp.where(lane[None, :] < hd // 2, row[:, None], col[:, None]) * inv_freq[lane % quarter][None, :]
    first = (lane % (hd // 2)) < quarter
    sin = jnp.sin(ang)
    return jnp.cos(ang), jnp.where(first[None, :], -sin, 0.0), jnp.where(first[None, :], 0.0, sin)


ATTN_LAT_NQB = 2


def _attn_lat_kernel(sink_ref, q_ref, k_ref, v_ref, ck_ref, cv_ref, qn_ref, kn_ref, cos_ref, sa_ref, sb_ref, _y_ctx,
                     o_ref, kb_scr, vb_scr):
    kvh = pl.program_id(1)
    hd = ATTN_HEAD_DIM
    t = q_ref.shape[0]
    span = ATTN_BLOCK + 2 * ATTN_WINDOW
    k = _rope(_rms_w(k_ref[...], kn_ref[...]), cos_ref[...], sa_ref[...], sb_ref[...])
    kb_scr[...] = k.astype(BF16)
    vb_scr[...] = v_ref[...].astype(BF16)
    ckb = ck_ref[0, 0].astype(BF16)
    cvb = cv_ref[0, 0].astype(BF16)
    q_iota = lax.broadcasted_iota(jnp.int32, (ATTN_BLOCK, span), 0)
    k_iota = lax.broadcasted_iota(jnp.int32, (ATTN_BLOCK, span), 1)

    def blocks(j, carry):
        ch = []
        for sub in range(ATTN_LAT_NQB):
            q0 = pl.multiple_of((j * ATTN_LAT_NQB + sub) * ATTN_BLOCK, ATTN_BLOCK)
            ws = pl.multiple_of(jnp.clip(q0 - ATTN_WINDOW, 0, t - span), ATTN_BLOCK)
            rows = pl.ds(q0, ATTN_BLOCK)
            valid = jnp.abs((k_iota + ws) - (q_iota + q0)) <= ATTN_WINDOW
            cos, sa, sb = cos_ref[rows, :], sa_ref[rows, :], sb_ref[rows, :]
            for g in range(ATTN_GROUP):
                qb = _rope(_rms_w(q_ref[rows, g * hd:(g + 1) * hd], qn_ref[...]), cos, sa, sb).astype(BF16)
                ch.append(dict(g=g, rows=rows, qb=qb, valid=valid,
                               kw=kb_scr[pl.ds(ws, span), :], vw=vb_scr[pl.ds(ws, span), :]))
        for d in ch:
            d["s_win"] = jnp.where(d["valid"], _dot_nt(d["qb"], d["kw"]) * hd ** -0.5, NEG_INF)
            d["s_ctx"] = _dot_nt(d["qb"], ckb) * hd ** -0.5
        for d in ch:
            (e_win, e_ctx), den = _softmax_parts([d["s_win"], d["s_ctx"]], sink_ref[kvh * ATTN_GROUP + d["g"]])
            d["p"] = ((e_win / den).astype(BF16), (e_ctx / den).astype(BF16))
        for d in ch:
            d["o"] = _dot(d["p"][0], d["vw"]) + _dot(d["p"][1], cvb)
        for d in ch:
            o_ref[d["rows"], d["g"] * hd:(d["g"] + 1) * hd] = d["o"].astype(BF16)
        return carry

    lax.fori_loop(0, t // (ATTN_BLOCK * ATTN_LAT_NQB), blocks, 0)


def _attn_lat(qkv, cache_k, cache_v, slot, sink, q_norm, k_norm, y_ctx, n_seq, t, row_off):
    hd = ATTN_HEAD_DIM
    gw = ATTN_GROUP * hd
    kcol = ATTN_HEADS
    vcol = kcol + ATTN_KV_HEADS
    n_past = cache_k.shape[2]
    ck = cache_k.reshape(cache_k.shape[0], cache_k.shape[1], n_past, ATTN_KV_HEADS * hd)
    cv = cache_v.reshape(ck.shape)
    cos, sin_a, sin_b = _rope_tables(t)
    tab = pl.BlockSpec((t, hd), lambda b, h: (0, 0))
    return pl.pallas_call(
        _attn_lat_kernel,
        grid=(n_seq, ATTN_KV_HEADS),
        in_specs=[
            pl.BlockSpec(memory_space=pltpu.SMEM),
            pl.BlockSpec((t, gw), lambda b, h: (b + row_off, h)),
            pl.BlockSpec((t, hd), lambda b, h: (b + row_off, kcol + h)),
            pl.BlockSpec((t, hd), lambda b, h: (b + row_off, vcol + h)),
            pl.BlockSpec((1, 1, n_past, hd), lambda b, h: (b, slot, 0, h)),
            pl.BlockSpec((1, 1, n_past, hd), lambda b, h: (b, slot, 0, h)),
            pl.BlockSpec((1, hd), lambda b, h: (0, 0)),
            pl.BlockSpec((1, hd), lambda b, h: (0, 0)),
            tab, tab, tab,
            pl.BlockSpec(memory_space=pl.ANY),
        ],
        out_specs=pl.BlockSpec((t, gw), lambda b, h: (b + row_off, h)),
        out_shape=jax.ShapeDtypeStruct(y_ctx.shape, BF16),
        input_output_aliases={11: 0},
        scratch_shapes=[pltpu.VMEM((t, hd), BF16), pltpu.VMEM((t, hd), BF16)],
        compiler_params=_cparams(2),
        name="attn_lat",
    )(sink, qkv, qkv, qkv, ck, cv, q_norm[None, :], k_norm[None, :], cos, sin_a, sin_b, y_ctx)


def _cumsum_rows(x, reverse):
    n = x.shape[0]
    row = lax.broadcasted_iota(jnp.int32, x.shape, 0)
    sh = 1
    while sh < n:
        if reverse:
            x = x + jnp.where(row < n - sh, pltpu.roll(x, n - sh, 0), 0.0)
        else:
            x = x + jnp.where(row >= sh, pltpu.roll(x, sh, 0), 0.0)
        sh *= 2
    return x


def _log_sigmoid(x):
    return jnp.minimum(x, 0.0) - jnp.log(1.0 + jnp.exp(-jnp.abs(x)))


MLSTM_HP = 4


def _mlstm_kernel(has_init, emit_state, *refs):
    refs = list(refs)
    q_ref, k_ref, v_ref, og_ref, gt_ref, bg_ref, on_ref = refs[:7]
    refs = refs[7:]
    if has_init:
        c0_ref, n0_ref, m0_ref = refs[:3]
        refs = refs[3:]
    _y_prev, y_ref = refs[:2]
    refs = refs[2:]
    if emit_state:
        co_ref, no_ref, mo_ref = refs[:3]
        refs = refs[3:]
    hf_scr, hb_scr = refs

    L = MLSTM_CHUNK
    t = q_ref.shape[0]
    nc = t // L
    dk, dv = MLSTM_DK, MLSTM_DV
    row = lax.broadcasted_iota(jnp.int32, (L, L), 0)
    col = lax.broadcasted_iota(jnp.int32, (L, L), 1)
    lane = lax.broadcasted_iota(jnp.int32, (L, LANES), 1)

    def chunk_step(chains):
        st = []
        for z, hh, c, C, n, m in chains:
            rows = pl.ds(pl.multiple_of(c * L, L), L)
            qc = q_ref[rows, hh * dk:(hh + 1) * dk] * dk ** -0.5
            kc = k_ref[rows, hh * dk:(hh + 1) * dk]
            vc = v_ref[rows, hh * dv:(hh + 1) * dv]
            G = gt_ref[rows, hh * LANES:(hh + 1) * LANES] + bg_ref[hh]
            CUM = _cumsum_rows(_log_sigmoid(G), reverse=(z == 1))
            ci, cf = 2 * z, 2 * z + 1
            ZT = jnp.where(lane == cf, CUM, G).T
            i_row, cum_row = ZT[ci:ci + 1, :L], ZT[cf:cf + 1, :L]
            i_col, cum_col = G[:, ci:ci + 1], CUM[:, cf:cf + 1]
            mask = (col <= row) if z == 0 else (col >= row)
            dmat = jnp.where(mask, cum_col - cum_row + i_row, NEG_INF)
            inter = cum_col + m
            m_c = jnp.maximum(inter, jnp.max(dmat, axis=-1, keepdims=True))
            bl = cum_row[:, L - 1:L] if z == 0 else cum_row[:, 0:1]
            m_new = jnp.maximum(bl + m, jnp.max(bl - cum_row + i_row, axis=-1, keepdims=True))
            kw = kc * jnp.exp(bl - cum_col + i_col - m_new)
            st.append(dict(
                z=z, hh=hh, rows=rows, C=C, n=n, qc=qc, m_c=m_c, m_new=m_new,
                qb=qc.astype(BF16), kb=kc.astype(BF16), vb=vc.astype(BF16), Cb=C.astype(BF16),
                w=jnp.exp(dmat - m_c), a=jnp.exp(inter - m_c), decay=jnp.exp(bl + m - m_new),
                kw=kw, kwb=kw.astype(BF16)))
        for d in st:
            d["qk"] = _dot_nt(d["qb"], d["kb"])
            d["qC"] = _dot(d["qb"], d["Cb"])
            d["kv"] = _dot_tn(d["kwb"], d["vb"])
        for d in st:
            d["s"] = d["qk"] * d["w"]
        for d in st:
            d["sv"] = _dot(d["s"].astype(BF16), d["vb"])
        out = []
        for d in st:
            num = d["sv"] + d["a"] * d["qC"]
            den = jnp.sum(d["s"], axis=-1, keepdims=True) + d["a"] * jnp.sum(d["qc"] * d["n"], axis=-1, keepdims=True)
            scr = hf_scr if d["z"] == 0 else hb_scr
            scr[d["rows"], d["hh"] * dv:(d["hh"] + 1) * dv] = num / jnp.maximum(jnp.abs(den), jnp.exp(-d["m_c"]))
            out += [d["decay"] * d["C"] + d["kv"],
                    d["decay"] * d["n"] + jnp.sum(d["kw"], axis=0, keepdims=True),
                    d["m_new"]]
        return out

    def body(ci, carry):
        chains = []
        for hh in range(MLSTM_HP):
            Cf, nf, mf, Cb, nb, mb = carry[6 * hh:6 * hh + 6]
            chains.append((0, hh, ci, Cf, nf, mf))
            chains.append((1, hh, nc - 1 - ci, Cb, nb, mb))
        return tuple(chunk_step(chains))

    init = []
    for hh in range(MLSTM_HP):
        for z in range(2):
            if has_init:
                init += [c0_ref[0, z, hh], n0_ref[0, z, hh], m0_ref[0, z, hh][:, 0:1]]
            else:
                init += [jnp.zeros((dk, dv), F32), jnp.zeros((1, dk), F32), jnp.zeros((1, 1), F32)]
    final = lax.fori_loop(0, nc, body, tuple(init))

    for hh in range(MLSTM_HP):
        cols = slice(hh * dv, (hh + 1) * dv)
        hs = hf_scr[:, cols] + hb_scr[:, cols]
        y_ref[:, cols] = (_rms_w(hs, on_ref[:, cols]) * _sigmoid(og_ref[:, cols])).astype(BF16)
        if emit_state:
            for z in range(2):
                C, n, m = final[6 * hh + 3 * z:6 * hh + 3 * z + 3]
                co_ref[0, z, hh] = C
                no_ref[0, z, hh] = n
                mo_ref[0, z, hh] = jnp.broadcast_to(m, (1, LANES))


def _mlstm(proj, gates, b_gate_h, out_norm, state0, y_prev, n_seq, t, row_off, emit_state):
    h_, dk, dv, hp = MLSTM_HEADS, MLSTM_DK, MLSTM_DV, MLSTM_HP
    nhb = h_ // hp
    has_init = state0 is not None
    in_specs = [
        pl.BlockSpec((t, hp * dk), lambda b, h: (b + row_off, h)),
        pl.BlockSpec((t, hp * dk), lambda b, h: (b + row_off, nhb + h)),
        pl.BlockSpec((t, hp * dv), lambda b, h: (b + row_off, nhb + h)),
        pl.BlockSpec((t, hp * dv), lambda b, h: (b + row_off, 2 * nhb + h)),
        pl.BlockSpec((t, hp * LANES), lambda b, h: (b + row_off, h)),
        pl.BlockSpec((hp, 1, LANES), lambda b, h: (h, 0, 0)),
        pl.BlockSpec((1, hp * dv), lambda b, h: (0, h)),
    ]
    args = [proj, proj, proj, proj, gates, b_gate_h[:, None, :], out_norm[None, :]]
    if has_init:
        c0, n0, m0 = state0
        in_specs += [
            pl.BlockSpec((1, 2, hp, dk, dv), lambda b, h: (b, 0, h, 0, 0)),
            pl.BlockSpec((1, 2, hp, 1, dk), lambda b, h: (b, 0, h, 0, 0)),
            pl.BlockSpec((1, 2, hp, 1, LANES), lambda b, h: (b, 0, h, 0, 0)),
        ]
        args += [c0, n0.reshape(n_seq, 2, h_, 1, dk),
                 jnp.broadcast_to(m0[..., None, None], (n_seq, 2, h_, 1, LANES))]
    in_specs.append(pl.BlockSpec(memory_space=pl.ANY))
    args.append(y_prev)
    aliases = {len(args) - 1: 0}
    out_specs = [pl.BlockSpec((t, hp * dv), lambda b, h: (b + row_off, h))]
    out_shape = [jax.ShapeDtypeStruct((N_TOK, D_MODEL), BF16)]
    if emit_state:
        out_specs += [
            pl.BlockSpec((1, 2, hp, dk, dv), lambda b, h: (b, 0, h, 0, 0)),
            pl.BlockSpec((1, 2, hp, 1, dk), lambda b, h: (b, 0, h, 0, 0)),
            pl.BlockSpec((1, 2, hp, 1, LANES), lambda b, h: (b, 0, h, 0, 0)),
        ]
        out_shape += [
            jax.ShapeDtypeStruct((n_seq, 2, h_, dk, dv), F32),
            jax.ShapeDtypeStruct((n_seq, 2, h_, 1, dk), F32),
            jax.ShapeDtypeStruct((n_seq, 2, h_, 1, LANES), F32),
        ]
    return pl.pallas_call(
        functools.partial(_mlstm_kernel, has_init, emit_state),
        grid=(n_seq, nhb),
        in_specs=in_specs,
        out_specs=out_specs,
        out_shape=out_shape,
        input_output_aliases=aliases,
        scratch_shapes=[pltpu.VMEM((t, hp * dv), F32), pltpu.VMEM((t, hp * dv), F32)],
        compiler_params=_cparams(2),
        name="mlstm",
    )(*args)


def _rwkv_mix_kernel(x_ref, xp_ref, xn_ref, shift_ref, scale_ref, g_ref, mu_ref, o_ref, h_scr):
    i = pl.program_id(0)
    tm = x_ref.shape[0]
    g, sh, sc = g_ref[...], shift_ref[0], scale_ref[0]
    n_ctx_tiles = N_CTX_TOK // tm
    tiles_per_lat = DEC_SEQ // tm
    tile_in_seq = jnp.where(i < n_ctx_tiles, i % (SEQ // tm), (i - n_ctx_tiles) % tiles_per_lat)
    tiles_in_seq = jnp.where(i < n_ctx_tiles, SEQ // tm, tiles_per_lat)
    has_prev = (tile_in_seq > 0).astype(F32)
    has_next = (tile_in_seq < tiles_in_seq - 1).astype(F32)
    h_scr[0:8, :] = _modulated(xp_ref[...], g, sh, sc) * has_prev
    h_scr[8 + tm:16 + tm, :] = _modulated(xn_ref[...], g, sh, sc) * has_next

    def fill(c, carry):
        r0 = pl.multiple_of(c * ROW_CHUNK, ROW_CHUNK)
        h_scr[pl.ds(8 + r0, ROW_CHUNK), :] = _modulated(x_ref[pl.ds(r0, ROW_CHUNK), :], g, sh, sc)
        return carry

    lax.fori_loop(0, tm // ROW_CHUNK, fill, 0, unroll=4)

    def mix(c, carry):
        r0 = pl.multiple_of(c * ROW_CHUNK, ROW_CHUNK)
        ext = h_scr[pl.ds(r0, ROW_CHUNK + 16), :]
        h = ext[8:8 + ROW_CHUNK]
        prev = pltpu.roll(ext, 1, 0)[8:8 + ROW_CHUNK]
        nxt = pltpu.roll(ext, ROW_CHUNK + 15, 0)[8:8 + ROW_CHUNK]
        cs = 0.5 * (prev + nxt) - h
        for p in range(o_ref.shape[0]):
            o_ref[p, pl.ds(r0, ROW_CHUNK), :] = (h + cs * mu_ref[p:p + 1, :]).astype(BF16)
        return carry

    lax.fori_loop(0, tm // ROW_CHUNK, mix, 0, unroll=2)


def _rwkv_mix(x, mods3, g_row, mu):
    tm = 256
    nb8 = tm // 8
    last = N_TOK // 8 - 1
    return pl.pallas_call(
        _rwkv_mix_kernel,
        grid=(N_TOK // tm,),
        in_specs=[
            pl.BlockSpec((tm, D_MODEL), lambda i: (i, 0)),
            pl.BlockSpec((8, D_MODEL), lambda i: (jnp.maximum(i * nb8 - 1, 0), 0)),
            pl.BlockSpec((8, D_MODEL), lambda i: (jnp.minimum((i + 1) * nb8, last), 0)),
            pl.BlockSpec((1, 1, D_MODEL), lambda i: (_mod_row(i, tm), 0, 3)),
            pl.BlockSpec((1, 1, D_MODEL), lambda i: (_mod_row(i, tm), 0, 4)),
            pl.BlockSpec((1, D_MODEL), lambda i: (0, 0)),
            pl.BlockSpec((6, D_MODEL), lambda i: (0, 0)),
        ],
        out_specs=pl.BlockSpec((6, tm, D_MODEL), lambda i: (0, i, 0)),
        out_shape=jax.ShapeDtypeStruct((6, N_TOK, D_MODEL), BF16),
        scratch_shapes=[pltpu.VMEM((tm + 16, D_MODEL), F32)],
        compiler_params=_cparams(1),
        name="rwkv_mix",
    )(x, x, x, mods3, mods3, g_row, mu)


def _lora_kernel(act, post, x_ref, a_ref, b_ref, bias_ref, o_ref):
    u = _dot(x_ref[0], a_ref[0])
    if act == "tanh":
        u = jnp.tanh(u)
    elif act == "sigmoid":
        u = _sigmoid(u)
    y = _dot(u.astype(BF16), b_ref[0]) + bias_ref[0]
    if post == "log_decay":
        y = -(_sigmoid(y) * jnp.exp(jnp.float32(-0.5)))
    elif post == "sigmoid":
        y = _sigmoid(y)
    o_ref[0] = y


def _lora(xs, p, wa, wb, bias, act, post, name):
    tm = 512
    nz, _, r = wa.shape
    return pl.pallas_call(
        functools.partial(_lora_kernel, act, post),
        grid=(nz, N_TOK // tm),
        in_specs=[
            pl.BlockSpec((1, tm, D_MODEL), lambda z, i: (p, i, 0)),
            pl.BlockSpec((1, D_MODEL, r), lambda z, i: (z, 0, 0)),
            pl.BlockSpec((1, r, D_MODEL), lambda z, i: (z, 0, 0)),
            pl.BlockSpec((1, 1, D_MODEL), lambda z, i: (z, 0, 0)),
        ],
        out_specs=pl.BlockSpec((1, tm, D_MODEL), lambda z, i: (z, i, 0)),
        out_shape=jax.ShapeDtypeStruct((nz, N_TOK, D_MODEL), F32),
        compiler_params=_cparams(2),
        name=name,
    )(xs, wa, wb, bias)


def _rkv_kernel(x_ref, w_ref, o_ref):
    x = x_ref[0]
    for c in range(o_ref.shape[2] // MM_COLS):
        cols = slice(c * MM_COLS, (c + 1) * MM_COLS)
        o_ref[0, :, cols] = _dot(x, w_ref[0, :, cols])


def _rkv(xs, w_rkv):
    tm = 512
    return pl.pallas_call(
        _rkv_kernel,
        grid=(3, N_TOK // tm),
        in_specs=[pl.BlockSpec((1, tm, D_MODEL), lambda p, i: (p, i, 0)),
                  pl.BlockSpec((1, D_MODEL, D_MODEL), lambda p, i: (p, 0, 0))],
        out_specs=pl.BlockSpec((1, tm, D_MODEL), lambda p, i: (p, i, 0)),
        out_shape=jax.ShapeDtypeStruct((3, N_TOK, D_MODEL), F32),
        compiler_params=_cparams(2),
        name="rwkv_rkv",
    )(xs, w_rkv)


def _head_sum(x, first_head):
    s0 = jnp.sum(jnp.where(first_head, x, 0.0), axis=-1, keepdims=True)
    s1 = jnp.sum(jnp.where(first_head, 0.0, x), axis=-1, keepdims=True)
    return jnp.where(first_head, s0, s1)


RWKV_PP = 4


def _rwkv_kernel(has_init, emit_state, *refs):
    refs = list(refs)
    r_ref, k_ref, v_ref, lw_ref, a_ref, g_ref, kk_w_ref, ka_ref, rk_ref, lng_ref, lnb_ref = refs[:11]
    refs = refs[11:]
    if has_init:
        s0_ref = refs[0]
        refs = refs[1:]
    _y_prev, y_ref = refs[:2]
    refs = refs[2:]
    if emit_state:
        so_ref = refs[0]
        refs = refs[1:]
    kk_scr, yf_scr, yb_scr = refs

    L = RWKV_CHUNK
    N = RWKV_HEAD
    t = r_ref.shape[1]
    nc = t // L
    row = lax.broadcasted_iota(jnp.int32, (L, L), 0)
    col = lax.broadcasted_iota(jnp.int32, (L, L), 1)
    first_head = lax.broadcasted_iota(jnp.int32, (L, LANES), 1) < N
    sq_row = lax.broadcasted_iota(jnp.int32, (LANES, LANES), 0)
    sq_col = lax.broadcasted_iota(jnp.int32, (LANES, LANES), 1)
    same_head = (sq_row < N) == (sq_col < N)

    first_head_t = lax.broadcasted_iota(jnp.int32, (t, LANES), 1) < N

    for pp in range(RWKV_PP):
        lanes = slice(pp * LANES, (pp + 1) * LANES)
        kkx = k_ref[0, :, lanes] * kk_w_ref[:, lanes]
        kk_scr[:, lanes] = kkx / jnp.maximum(jnp.sqrt(_head_sum(kkx * kkx, first_head_t)), 1e-12)

    def chunk_step(chains):
        heads = (first_head, jnp.logical_not(first_head))
        pick = lambda x0, x1: jnp.where(first_head, x0, x1)
        st = []
        for z, pp, c, S in chains:
            lanes = slice(pp * LANES, (pp + 1) * LANES)
            sl = pl.ds(pl.multiple_of(c * L, L), L)
            r_, k_, v_ = r_ref[0, sl, lanes], k_ref[0, sl, lanes], v_ref[0, sl, lanes]
            lw, a_ = lw_ref[z, sl, lanes], a_ref[z, sl, lanes]
            kk = kk_scr[sl, lanes]
            kd = k_ * (1.0 + (a_ - 1.0) * ka_ref[:, lanes])
            b_ = kk * a_
            cum = _cumsum_rows(lw, reverse=(z == 1))
            cum_end = cum[L - 1:L, :] if z == 0 else cum[0:1, :]
            kp = (kk * jnp.exp(cum - lw)).astype(BF16)
            rp = (r_ * jnp.exp(cum)).astype(BF16)
            inv = jnp.exp(-cum)
            to_end = jnp.exp(cum_end - cum)
            zero = jnp.zeros_like(kp)
            st.append(dict(
                z=z, pp=pp, sl=sl, lanes=lanes, S=S, kp=kp, rp=rp, cum_end=cum_end,
                kt=(kd * inv).astype(BF16), bt=(b_ * inv).astype(BF16),
                kb_end=jnp.concatenate([kd * to_end, -(b_ * to_end)], axis=0).astype(BF16),
                vb=v_.astype(BF16), Sb=S.astype(BF16),
                probe=[jnp.concatenate([jnp.where(hm, kp, zero), jnp.where(hm, rp, zero)], axis=0) for hm in heads],
                strict=(col < row) if z == 0 else (col > row),
                incl=(col <= row) if z == 0 else (col >= row)))
        for d in st:
            d["gk"] = [_dot_nt(pr, d["kt"]) for pr in d["probe"]]
            d["gb"] = [_dot_nt(pr, d["bt"]) for pr in d["probe"]]
        for d in st:
            d["g1"] = [jnp.where(d["strict"], g[:L], 0.0).astype(BF16) for g in d["gk"]]
            d["n"] = [jnp.where(d["strict"], g[:L], 0.0).astype(BF16) for g in d["gb"]]
            d["g3"] = [jnp.where(d["incl"], g[L:], 0.0).astype(BF16) for g in d["gk"]]
            d["g4"] = [jnp.where(d["incl"], g[L:], 0.0).astype(BF16) for g in d["gb"]]
        for d in st:
            d["g1v"] = [_dot(g, d["vb"]) for g in d["g1"]]
            d["kps"] = _dot_nt(d["kp"], d["Sb"])
            d["rps"] = _dot_nt(d["rp"], d["Sb"])
        for d in st:
            d["u"] = d["kps"] + pick(*d["g1v"])
        sign = -1.0
        p = 1
        while p < L:
            for d in st:
                ub = d["u"].astype(BF16)
                d["nu"] = [_dot(m, ub) for m in d["n"]]
                if 2 * p < L:
                    d["n"] = [_dot(m, m).astype(BF16) for m in d["n"]]
            for d in st:
                d["u"] = d["u"] + sign * pick(*d["nu"])
            sign = 1.0
            p *= 2
        for d in st:
            ub = d["u"].astype(BF16)
            d["y3"] = [_dot(g, d["vb"]) for g in d["g3"]]
            d["y4"] = [_dot(g, ub) for g in d["g4"]]
            d["upd"] = _dot_tn(jnp.concatenate([d["vb"], ub], axis=0), d["kb_end"])
        out = []
        for d in st:
            y = d["rps"] + pick(d["y3"][0] - d["y4"][0], d["y3"][1] - d["y4"][1])
            (yf_scr if d["z"] == 0 else yb_scr)[d["sl"], d["lanes"]] = y
            out.append(d["S"] * jnp.exp(d["cum_end"]) + jnp.where(same_head, d["upd"], 0.0))
        return out

    def body(ci, carry):
        chains = []
        for pp in range(RWKV_PP):
            chains.append((0, pp, ci, carry[2 * pp]))
            chains.append((1, pp, nc - 1 - ci, carry[2 * pp + 1]))
        return tuple(chunk_step(chains))

    init = []
    for pp in range(RWKV_PP):
        for z in range(2):
            if has_init:
                s = s0_ref[0, z, pp]
                init.append(jnp.where(sq_row < N, s, pltpu.roll(s, N, 1)))
            else:
                init.append(jnp.zeros((LANES, LANES), F32))
    final = lax.fori_loop(0, nc, body, tuple(init))

    for pp in range(RWKV_PP):
        lanes = slice(pp * LANES, (pp + 1) * LANES)
        r_, k_, v_ = r_ref[0, :, lanes], k_ref[0, :, lanes], v_ref[0, :, lanes]
        y = yf_scr[:, lanes] + yb_scr[:, lanes]
        mean = _head_sum(y, first_head_t) * (1.0 / N)
        yc = y - mean
        var = _head_sum(yc * yc, first_head_t) * (1.0 / N)
        yn = yc * lax.rsqrt(var + RWKV_GN_EPS) * lng_ref[:, lanes] + lnb_ref[:, lanes]
        ka = ka_ref[:, lanes]
        kd_sum = k_ * (1.0 + (a_ref[0, :, lanes] - 1.0) * ka) + k_ * (1.0 + (a_ref[1, :, lanes] - 1.0) * ka)
        bonus = _head_sum(r_ * kd_sum * rk_ref[:, lanes], first_head_t) * v_
        y_ref[:, lanes] = ((yn + bonus) * g_ref[:, lanes]).astype(BF16)
        if emit_state:
            for z in range(2):
                S = final[2 * pp + z]
                so_ref[0, z, pp] = (S + pltpu.roll(S, N, 1))[:, :N]


def _rwkv(rkv, lw, a, g, k_k, k_a, r_k, ln_g, ln_b, s0, y_prev, n_seq, t, row_off, emit_state):
    has_init = s0 is not None
    w = RWKV_PP * LANES
    tok = lambda lead, first: pl.BlockSpec((lead, t, w), lambda b, p: (first, b + row_off, p))
    par = pl.BlockSpec((1, w), lambda b, p: (0, p))
    in_specs = [tok(1, 0), tok(1, 1), tok(1, 2), tok(2, 0), tok(2, 0),
                pl.BlockSpec((t, w), lambda b, p: (b + row_off, p)),
                par, par, par, par, par]
    args = [rkv, rkv, rkv, lw, a, g, k_k[None, :], k_a[None, :], r_k.reshape(1, D_MODEL), ln_g[None, :], ln_b[None, :]]
    if has_init:
        in_specs.append(pl.BlockSpec((1, 2, RWKV_PP, LANES, LANES), lambda b, p: (b, 0, p, 0, 0)))
        args.append(s0)
    in_specs.append(pl.BlockSpec(memory_space=pl.ANY))
    args.append(y_prev)
    aliases = {len(args) - 1: 0}
    out_specs = [pl.BlockSpec((t, w), lambda b, p: (b + row_off, p))]
    out_shape = [jax.ShapeDtypeStruct((N_TOK, D_MODEL), BF16)]
    if emit_state:
        out_specs.append(pl.BlockSpec((1, 2, RWKV_PP, LANES, RWKV_HEAD), lambda b, p: (b, 0, p, 0, 0)))
        out_shape.append(jax.ShapeDtypeStruct((n_seq, 2, RWKV_PAIRS, LANES, RWKV_HEAD), F32))
    return pl.pallas_call(
        functools.partial(_rwkv_kernel, has_init, emit_state),
        grid=(n_seq, RWKV_PAIRS // RWKV_PP),
        in_specs=in_specs,
        out_specs=out_specs,
        out_shape=out_shape,
        input_output_aliases=aliases,
        scratch_shapes=[pltpu.VMEM((t, w), F32)] * 3,
        compiler_params=_cparams(2),
        name="rwkv",
    )(*args)


def _attn_layer(x, mods3, g_row, cache_k, cache_v, slot, w_qkv, q_norm, k_norm, sink, w_o):
    qkv = _mm_mod(x, mods3, 3, g_row, w_qkv.astype(BF16), 512, "attn_qkv")
    y, k_ctx, v_ctx = _attn_ctx(qkv, sink, q_norm, k_norm, BATCH, SEQ)
    y = _attn_lat(qkv, cache_k, cache_v, slot, sink, q_norm, k_norm, y, DEC_BATCH, DEC_SEQ, N_CTX_TOK // DEC_SEQ)
    x = _mm_res(y, w_o.astype(BF16), x, mods3, 5, "attn_out")
    shape = (BATCH, SEQ, ATTN_KV_HEADS, ATTN_HEAD_DIM)
    return x, k_ctx.reshape(shape), v_ctx.reshape(shape)


def _mlstm_layer(x, mods3, g_row, state_lat, w_in, w_gate, b_gate, out_norm, w_o):
    h_ = MLSTM_HEADS
    proj = _mm_mod(x, mods3, 3, g_row, w_in.astype(BF16), 256, "mlstm_in")
    wg = jnp.transpose(w_gate.reshape(D_MODEL, 4, h_), (0, 2, 1))
    wg = jnp.pad(wg, ((0, 0), (0, 0), (0, LANES - 4))).reshape(D_MODEL, h_ * LANES).astype(BF16)
    bg = jnp.pad(b_gate.reshape(4, h_).T, ((0, 0), (0, LANES - 4)))
    gates = _mm_mod(x, mods3, 3, g_row, wg, 512, "mlstm_gates")
    y, C, n, m = _mlstm(proj, gates, bg, out_norm, None, _token_zeros(), BATCH, SEQ, 0, True)
    (y,) = _mlstm(proj, gates, bg, out_norm, state_lat, y, DEC_BATCH, DEC_SEQ, N_CTX_TOK // DEC_SEQ, False)
    x = _mm_res(y, w_o.astype(BF16), x, mods3, 5, "mlstm_out")
    return x, C, n[:, :, :, 0, :], m[:, :, :, 0, 0]


def _rwkv_layer(x, mods3, g_row, s0_lat, mu, w_rkv, w0, wA, wB, a0, aA, aB, gA, gB, k_k, k_a, r_k, ln_g, ln_b, w_o):
    xs = _rwkv_mix(x, mods3, g_row, mu)
    rkv = _rkv(xs, w_rkv.astype(BF16))
    pad_r = lambda w, axis: jnp.pad(w, [(0, 0) if ax != axis else (0, LANES - w.shape[axis]) for ax in range(w.ndim)])
    lw = _lora(xs, 3, pad_r(wA, 2).astype(BF16), pad_r(wB, 1).astype(BF16), w0[:, None, :], "tanh", "log_decay", "rwkv_w")
    a = _lora(xs, 4, pad_r(aA, 2).astype(BF16), pad_r(aB, 1).astype(BF16), a0[:, None, :], "none", "sigmoid", "rwkv_a")
    g = _lora(xs, 5, gA[None].astype(BF16), gB[None].astype(BF16), jnp.zeros((1, 1, D_MODEL), F32), "sigmoid", "none", "rwkv_g")[0]
    s0 = jnp.pad(s0_lat.reshape(DEC_BATCH, 2, RWKV_PAIRS, LANES, RWKV_HEAD), ((0, 0),) * 4 + ((0, LANES - RWKV_HEAD),))
    y, S = _rwkv(rkv, lw, a, g, k_k, k_a, r_k, ln_g, ln_b, None, _token_zeros(), BATCH, SEQ, 0, True)
    (y,) = _rwkv(rkv, lw, a, g, k_k, k_a, r_k, ln_g, ln_b, s0, y, DEC_BATCH, DEC_SEQ, N_CTX_TOK // DEC_SEQ, False)
    x = _mm_res(y, w_o.astype(BF16), x, mods3, 5, "rwkv_out")
    return x, S.reshape(BATCH, 2, RWKV_HEADS, RWKV_HEAD, RWKV_HEAD)


def kernel(x_prompt, x_sample, cache_k, cache_v, state_mlstm_C, state_mlstm_n, state_mlstm_m, state_rwkv, c, c_ctx, mod_w, mod_b, norm_g, ffn_w_in, ffn_w_out, attn_w_qkv, attn_q_norm, attn_k_norm, attn_sink, attn_w_o, mlstm_w_in, mlstm_w_gate, mlstm_b_gate, mlstm_out_norm, mlstm_w_o, rwkv_mu, rwkv_w_rkv, rwkv_w0, rwkv_wA, rwkv_wB, rwkv_a0, rwkv_aA, rwkv_aB, rwkv_gA, rwkv_gB, rwkv_k_k, rwkv_k_a, rwkv_r_k, rwkv_ln_g, rwkv_ln_b, rwkv_w_o):
    x = jnp.concatenate([x_prompt.reshape(N_CTX_TOK, D_MODEL), x_sample.reshape(N_LAT_TOK, D_MODEL)], axis=0)
    cond8 = jnp.concatenate([c_ctx[None, :], c, jnp.zeros((MOD_ROWS - 1 - DEC_BATCH, D_MODEL), F32)], axis=0)
    mods = _adaln_all(cond8, mod_w, mod_b)
    ffn_order = [(i, j) for i in range(DEPTH) for j in range(2)]
    w_in_b = ffn_w_in[0, 0].astype(BF16)
    w_out_b = ffn_w_out[0, 0].astype(BF16)

    def ffn(x, mods3, slot, g_row, k, w_in_b, w_out_b):
        nxt = (ffn_w_in, ffn_w_out) + ffn_order[k + 1] if k + 1 < len(ffn_order) else None
        res = _ffn(x, mods3, slot, g_row, w_in_b, w_out_b, nxt)
        return res if nxt is not None else (res[0], None, None)

    new_k, new_v, new_C, new_n, new_m, new_S = [], [], [], [], [], []
    for i in range(DEPTH):
        kind, slot = i % N_MIXERS, i // N_MIXERS
        mods3 = mods[i].reshape(MOD_ROWS, 1, N_MOD * D_MODEL)
        x, w_in_b, w_out_b = ffn(x, mods3, 0, norm_g[i, 0][None, :], 2 * i, w_in_b, w_out_b)
        g_row = norm_g[i, 1][None, :]
        if kind == 0:
            x, k_ctx, v_ctx = _attn_layer(x, mods3, g_row, cache_k, cache_v, slot, attn_w_qkv[slot], attn_q_norm[slot],
                                          attn_k_norm[slot], attn_sink[slot], attn_w_o[slot])
            new_k.append(k_ctx)
            new_v.append(v_ctx)
        elif kind == 1:
            state_lat = (state_mlstm_C[:, slot], state_mlstm_n[:, slot], state_mlstm_m[:, slot])
            x, C, n, m = _mlstm_layer(x, mods3, g_row, state_lat, mlstm_w_in[slot], mlstm_w_gate[slot], mlstm_b_gate[slot],
                                      mlstm_out_norm[slot], mlstm_w_o[slot])
            new_C.append(C)
            new_n.append(n)
            new_m.append(m)
        else:
            x, S = _rwkv_layer(x, mods3, g_row, state_rwkv[:, slot], rwkv_mu[slot], rwkv_w_rkv[slot], rwkv_w0[slot],
                               rwkv_wA[slot], rwkv_wB[slot], rwkv_a0[slot], rwkv_aA[slot], rwkv_aB[slot], rwkv_gA[slot],
                               rwkv_gB[slot], rwkv_k_k[slot], rwkv_k_a[slot], rwkv_r_k[slot], rwkv_ln_g[slot],
                               rwkv_ln_b[slot], rwkv_w_o[slot])
            new_S.append(S)
        x, w_in_b, w_out_b = ffn(x, mods3, 6, norm_g[i, 2][None, :], 2 * i + 1, w_in_b, w_out_b)

    y_prompt = x[:N_CTX_TOK].reshape(BATCH, SEQ, D_MODEL)
    y_sample = x[N_CTX_TOK:].reshape(DEC_BATCH, DEC_SEQ, D_MODEL)
    return (y_prompt, y_sample, jnp.stack(new_k, axis=1), jnp.stack(new_v, axis=1),
            jnp.stack(new_C, axis=1), jnp.stack(new_n, axis=1), jnp.stack(new_m, axis=1), jnp.stack(new_S, axis=1))
```

```python
import functools

import jax
import jax.numpy as jnp
from jax import lax
from jax.experimental import pallas as pl
from jax.experimental.pallas import tpu as pltpu

D_MODEL = 2048
BATCH = 32
SEQ = 256
DEPTH = 4
DEC_BATCH = 4
DEC_SEQ = 1024
GRID_W = 64
N_MIXERS = 3
N_MOD = 9
D_FF = 5632
EPS = 1e-6

ATTN_HEADS = 16
ATTN_KV_HEADS = 4
ATTN_GROUP = ATTN_HEADS // ATTN_KV_HEADS
ATTN_HEAD_DIM = D_MODEL // ATTN_HEADS
ATTN_WINDOW = 128
ATTN_BLOCK = 128
ROPE_THETA = 10000.0

MLSTM_HEADS = 8
MLSTM_DV = D_MODEL // MLSTM_HEADS
MLSTM_DK = MLSTM_DV // 2
MLSTM_CHUNK = 64

RWKV_HEAD = 64
RWKV_HEADS = D_MODEL // RWKV_HEAD
RWKV_PAIRS = RWKV_HEADS // 2
RWKV_CHUNK = 64
RWKV_GN_EPS = 64e-5

N_CTX_TOK = BATCH * SEQ
N_LAT_TOK = DEC_BATCH * DEC_SEQ
N_TOK = N_CTX_TOK + N_LAT_TOK
MOD_ROWS = 8
LANES = 128

VMEM_LIMIT = 60 * 1024 * 1024
BF16 = jnp.bfloat16
F32 = jnp.float32
NEG_INF = float("-inf")
MM_COLS = 512


def _cparams(n_axes):
    return pltpu.CompilerParams(dimension_semantics=("arbitrary",) * n_axes, vmem_limit_bytes=VMEM_LIMIT)


def _mod_row(i, tm):
    n_ctx_tiles = N_CTX_TOK // tm
    tiles_per_lat = DEC_SEQ // tm
    return jnp.where(i < n_ctx_tiles, 0, 1 + (i - n_ctx_tiles) // tiles_per_lat)


def _mod_spec(slot, tm):
    return pl.BlockSpec((1, 1, D_MODEL), lambda i, j, s=slot: (_mod_row(i, tm), 0, s))


def _dot(a, b):
    return jnp.dot(a, b, preferred_element_type=F32)


def _dot_nt(a, b):
    return lax.dot_general(a, b, (((1,), (1,)), ((), ())), preferred_element_type=F32)


def _dot_tn(a, b):
    return lax.dot_general(a, b, (((0,), (0,)), ((), ())), preferred_element_type=F32)


def _sigmoid(x):
    return jax.nn.sigmoid(x)


def _modulated(x, g, shift, scale):
    ms = jnp.mean(x * x, axis=-1, keepdims=True)
    y = x * lax.rsqrt(ms + EPS) * g
    return y * (1.0 + scale) + shift


def _adaln_kernel(cond_ref, w_ref, b_ref, o_ref):
    c = cond_ref[...]
    a = (c * _sigmoid(c)).astype(BF16)
    o_ref[0] = _dot(a, w_ref[0].astype(BF16)) + b_ref[0]


def _adaln_all(cond8, mod_w, mod_b):
    tn = 1024
    n_out = N_MOD * D_MODEL
    return pl.pallas_call(
        _adaln_kernel,
        grid=(DEPTH, n_out // tn),
        in_specs=[
            pl.BlockSpec((MOD_ROWS, D_MODEL), lambda l, j: (0, 0)),
            pl.BlockSpec((1, D_MODEL, tn), lambda l, j: (l, 0, j)),
            pl.BlockSpec((1, 1, tn), lambda l, j: (l, 0, j)),
        ],
        out_specs=pl.BlockSpec((1, MOD_ROWS, tn), lambda l, j: (l, 0, j)),
        out_shape=jax.ShapeDtypeStruct((DEPTH, MOD_ROWS, n_out), F32),
        compiler_params=_cparams(2),
        name="adaln",
    )(cond8, mod_w, mod_b.reshape(DEPTH, 1, n_out))


FFN_TM = 1024
FFN_TF = 512
ROW_CHUNK = 32


def _cast_rows(total, n_steps):
    r = 16
    while total % r or total // r > n_steps:
        r *= 2
    return r


def _ffn_kernel(has_next, *refs):
    x_ref, shift_ref, scale_ref, gate_ref, g_ref, wg_ref, wu_ref, wo_ref = refs[:8]
    refs = refs[8:]
    if has_next:
        nin_ref, nout_ref, o_ref, cin_ref, cout_ref, h_scr = refs
    else:
        o_ref, h_scr = refs
    f = pl.program_id(1)

    @pl.when(f == 0)
    def _():
        def rows(c, carry):
            r = pl.ds(pl.multiple_of(c * ROW_CHUNK, ROW_CHUNK), ROW_CHUNK)
            h_scr[r, :] = _modulated(x_ref[r, :], g_ref[...], shift_ref[0], scale_ref[0]).astype(BF16)
            return carry
        lax.fori_loop(0, x_ref.shape[0] // ROW_CHUNK, rows, 0, unroll=8)

    h = h_scr[...]
    a = _dot(h, wg_ref[...])
    b = _dot(h, wu_ref[...])
    act = ((a * _sigmoid(a)) * b).astype(BF16)
    if has_next:
        cin_ref[...] = nin_ref[0, 0].astype(BF16)
        cout_ref[...] = nout_ref[0, 0].astype(BF16)
    for c in range(o_ref.shape[1] // MM_COLS):
        cols = slice(c * MM_COLS, (c + 1) * MM_COLS)
        o_ref[:, cols] = jnp.where(f == 0, 0.0, o_ref[:, cols]) + _dot(act, wo_ref[:, cols])

    @pl.when(f == pl.num_programs(1) - 1)
    def _():
        o_ref[...] = x_ref[...] + (0.5 * gate_ref[0]) * o_ref[...]


def _ffn(x, mods3, slot, g_row, w_in, w_out, nxt):
    tm, tf = FFN_TM, FFN_TF
    nf = D_FF // tf
    in_specs = [
        pl.BlockSpec((tm, D_MODEL), lambda i, f: (i, 0)),
        _mod_spec(slot, tm), _mod_spec(slot + 1, tm), _mod_spec(slot + 2, tm),
        pl.BlockSpec((1, D_MODEL), lambda i, f: (0, 0)),
        pl.BlockSpec((D_MODEL, tf), lambda i, f: (0, f)),
        pl.BlockSpec((D_MODEL, tf), lambda i, f: (0, f + nf)),
        pl.BlockSpec((tf, D_MODEL), lambda i, f: (f, 0)),
    ]
    args = [x, mods3, mods3, mods3, g_row, w_in, w_in, w_out]
    out_specs = [pl.BlockSpec((tm, D_MODEL), lambda i, f: (i, 0))]
    out_shape = [jax.ShapeDtypeStruct((N_TOK, D_MODEL), F32)]
    if nxt is not None:
        n_in, n_out, layer, which = nxt
        n_steps = (N_TOK // tm) * nf
        rows_in, rows_out = _cast_rows(D_MODEL, n_steps), _cast_rows(D_FF, n_steps)
        last_in, last_out = D_MODEL // rows_in - 1, D_FF // rows_out - 1
        blk_in = lambda i, f: jnp.minimum(i * nf + f, last_in)
        blk_out = lambda i, f: jnp.minimum(i * nf + f, last_out)
        in_specs += [
            pl.BlockSpec((1, 1, rows_in, 2 * D_FF), lambda i, f: (layer, which, blk_in(i, f), 0)),
            pl.BlockSpec((1, 1, rows_out, D_MODEL), lambda i, f: (layer, which, blk_out(i, f), 0)),
        ]
        args += [n_in, n_out]
        out_specs += [
            pl.BlockSpec((rows_in, 2 * D_FF), lambda i, f: (blk_in(i, f), 0)),
            pl.BlockSpec((rows_out, D_MODEL), lambda i, f: (blk_out(i, f), 0)),
        ]
        out_shape += [jax.ShapeDtypeStruct((D_MODEL, 2 * D_FF), BF16), jax.ShapeDtypeStruct((D_FF, D_MODEL), BF16)]
    return pl.pallas_call(
        functools.partial(_ffn_kernel, nxt is not None),
        grid=(N_TOK // tm, nf),
        in_specs=in_specs,
        out_specs=out_specs,
        out_shape=out_shape,
        scratch_shapes=[pltpu.VMEM((tm, D_MODEL), BF16)],
        compiler_params=_cparams(2),
        name="ffn",
    )(*args)


def _resident(shape):
    return pl.BlockSpec(shape, lambda *_: (0,) * len(shape), pipeline_mode=pl.Buffered(1))


def _mm_mod_kernel(x_ref, shift_ref, scale_ref, g_ref, w_ref, o_ref):
    h = _modulated(x_ref[...], g_ref[...], shift_ref[0], scale_ref[0]).astype(BF16)
    for c in range(o_ref.shape[1] // MM_COLS):
        cols = slice(c * MM_COLS, (c + 1) * MM_COLS)
        o_ref[:, cols] = _dot(h, w_ref[:, cols])


def _mm_mod(x, mods3, slot, g_row, w, tm, name):
    n = w.shape[1]
    mod = lambda s: pl.BlockSpec((1, 1, D_MODEL), lambda i: (_mod_row(i, tm), 0, s))
    return pl.pallas_call(
        _mm_mod_kernel,
        grid=(N_TOK // tm,),
        in_specs=[
            pl.BlockSpec((tm, D_MODEL), lambda i: (i, 0)),
            mod(slot), mod(slot + 1),
            _resident((1, D_MODEL)),
            _resident((D_MODEL, n)),
        ],
        out_specs=pl.BlockSpec((tm, n), lambda i: (i, 0)),
        out_shape=jax.ShapeDtypeStruct((N_TOK, n), F32),
        compiler_params=_cparams(1),
        name=name,
    )(x, mods3, mods3, g_row, w)


def _mm_res_kernel(y_ref, w_ref, x_ref, gate_ref, o_ref):
    y = y_ref[...]
    for c in range(o_ref.shape[1] // MM_COLS):
        cols = slice(c * MM_COLS, (c + 1) * MM_COLS)
        o_ref[:, cols] = x_ref[:, cols] + gate_ref[0][:, cols] * _dot(y, w_ref[:, cols])


def _mm_res(y, w, x, mods3, slot, name):
    tm = 512
    k = y.shape[1]
    return pl.pallas_call(
        _mm_res_kernel,
        grid=(N_TOK // tm,),
        in_specs=[
            pl.BlockSpec((tm, k), lambda i: (i, 0)),
            _resident((k, D_MODEL)),
            pl.BlockSpec((tm, D_MODEL), lambda i: (i, 0)),
            pl.BlockSpec((1, 1, D_MODEL), lambda i: (_mod_row(i, tm), 0, slot)),
        ],
        out_specs=pl.BlockSpec((tm, D_MODEL), lambda i: (i, 0)),
        out_shape=jax.ShapeDtypeStruct((N_TOK, D_MODEL), F32),
        compiler_params=_cparams(1),
        name=name,
    )(y, w, x, mods3)


def _token_zeros():
    return jnp.zeros((N_TOK, D_MODEL), BF16)


def _rms_w(x, w):
    return x * lax.rsqrt(jnp.mean(x * x, axis=-1, keepdims=True) + EPS) * w


def _softmax_parts(scores, sink):
    m = sink
    for s in scores:
        m = jnp.maximum(jnp.max(s, axis=-1, keepdims=True), m)
    es = [jnp.exp(s - m) for s in scores]
    den = jnp.exp(sink - m)
    for e in es:
        den = den + jnp.sum(e, axis=-1, keepdims=True)
    return es, den


ATTN_CTX_KVH = 4


def _attn_ctx_kernel(sink_ref, q_ref, k_ref, v_ref, qn_ref, kn_ref, _y_init, o_ref, ko_ref, vo_ref):
    h2 = pl.program_id(1)
    hd = ATTN_HEAD_DIM
    chains = []
    for kv in range(ATTN_CTX_KVH):
        cols = slice(kv * hd, (kv + 1) * hd)
        k = _rms_w(k_ref[:, cols], kn_ref[...])
        v = v_ref[:, cols]
        ko_ref[:, cols] = k
        vo_ref[:, cols] = v
        kb, vb = k.astype(BF16), v.astype(BF16)
        for g in range(ATTN_GROUP):
            head = kv * ATTN_GROUP + g
            q = _rms_w(q_ref[:, head * hd:(head + 1) * hd], qn_ref[...]).astype(BF16)
            chains.append(dict(head=head, q=q, kb=kb, vb=vb))
    for d in chains:
        d["s"] = _dot_nt(d["q"], d["kb"]) * hd ** -0.5
    for d in chains:
        (e,), den = _softmax_parts([d["s"]], sink_ref[h2 * ATTN_CTX_KVH * ATTN_GROUP + d["head"]])
        d["p"] = (e / den).astype(BF16)
    for d in chains:
        d["o"] = _dot(d["p"], d["vb"])
    for d in chains:
        o_ref[:, d["head"] * hd:(d["head"] + 1) * hd] = d["o"].astype(BF16)


def _attn_ctx(qkv, sink, q_norm, k_norm, n_seq, t):
    hd = ATTN_HEAD_DIM
    nkv = ATTN_CTX_KVH
    gw = nkv * ATTN_GROUP * hd
    kcol = ATTN_HEADS // nkv
    vcol = kcol + ATTN_KV_HEADS // nkv
    n_tok = n_seq * t
    return pl.pallas_call(
        _attn_ctx_kernel,
        grid=(n_seq, ATTN_KV_HEADS // nkv),
        in_specs=[
            pl.BlockSpec(memory_space=pltpu.SMEM),
            pl.BlockSpec((t, gw), lambda b, h: (b, h)),
            pl.BlockSpec((t, nkv * hd), lambda b, h: (b, kcol + h)),
            pl.BlockSpec((t, nkv * hd), lambda b, h: (b, vcol + h)),
            pl.BlockSpec((1, hd), lambda b, h: (0, 0)),
            pl.BlockSpec((1, hd), lambda b, h: (0, 0)),
            pl.BlockSpec(memory_space=pl.ANY),
        ],
        out_specs=[
            pl.BlockSpec((t, gw), lambda b, h: (b, h)),
            pl.BlockSpec((t, nkv * hd), lambda b, h: (b, h)),
            pl.BlockSpec((t, nkv * hd), lambda b, h: (b, h)),
        ],
        input_output_aliases={6: 0},
        out_shape=[
            jax.ShapeDtypeStruct((N_TOK, ATTN_HEADS * hd), BF16),
            jax.ShapeDtypeStruct((n_tok, ATTN_KV_HEADS * hd), F32),
            jax.ShapeDtypeStruct((n_tok, ATTN_KV_HEADS * hd), F32),
        ],
        compiler_params=_cparams(2),
        name="attn_ctx",
    )(sink, qkv, qkv, qkv, q_norm[None, :], k_norm[None, :], _token_zeros())


def _rope(x, cos, sin_a, sin_b):
    quarter = ATTN_HEAD_DIM // 4
    return x * cos + pltpu.roll(x, ATTN_HEAD_DIM - quarter, 1) * sin_a + pltpu.roll(x, quarter, 1) * sin_b


def _rope_tables(t):
    hd = ATTN_HEAD_DIM
    quarter = hd // 4
    pos = jnp.arange(t)
    row = (pos // GRID_W).astype(F32)
    col = (pos % GRID_W).astype(F32)
    inv_freq = ROPE_THETA ** (-jnp.arange(quarter, dtype=F32) / quarter)
    lane = jnp.arange(hd)
    ang = jnp.where(lane[None, :] < hd // 2, row[:, None], col[:, None]) * inv_freq[lane % quarter][None, :]
    first = (lane % (hd // 2)) < quarter
    sin = jnp.sin(ang)
    return jnp.cos(ang), jnp.where(first[None, :], -sin, 0.0), jnp.where(first[None, :], 0.0, sin)


ATTN_LAT_NQB = 2


def _attn_lat_kernel(sink_ref, q_ref, k_ref, v_ref, ck_ref, cv_ref, qn_ref, kn_ref, cos_ref, sa_ref, sb_ref, _y_ctx,
                     o_ref, kb_scr, vb_scr):
    kvh = pl.program_id(1)
    hd = ATTN_HEAD_DIM
    t = q_ref.shape[0]
    span = ATTN_BLOCK + 2 * ATTN_WINDOW
    k = _rope(_rms_w(k_ref[...], kn_ref[...]), cos_ref[...], sa_ref[...], sb_ref[...])
    kb_scr[...] = k.astype(BF16)
    vb_scr[...] = v_ref[...].astype(BF16)
    ckb = ck_ref[0, 0].astype(BF16)
    cvb = cv_ref[0, 0].astype(BF16)
    q_iota = lax.broadcasted_iota(jnp.int32, (ATTN_BLOCK, span), 0)
    k_iota = lax.broadcasted_iota(jnp.int32, (ATTN_BLOCK, span), 1)

    def blocks(j, carry):
        ch = []
        for sub in range(ATTN_LAT_NQB):
            q0 = pl.multiple_of((j * ATTN_LAT_NQB + sub) * ATTN_BLOCK, ATTN_BLOCK)
            ws = pl.multiple_of(jnp.clip(q0 - ATTN_WINDOW, 0, t - span), ATTN_BLOCK)
            rows = pl.ds(q0, ATTN_BLOCK)
            valid = jnp.abs((k_iota + ws) - (q_iota + q0)) <= ATTN_WINDOW
            cos, sa, sb = cos_ref[rows, :], sa_ref[rows, :], sb_ref[rows, :]
            for g in range(ATTN_GROUP):
                qb = _rope(_rms_w(q_ref[rows, g * hd:(g + 1) * hd], qn_ref[...]), cos, sa, sb).astype(BF16)
                ch.append(dict(g=g, rows=rows, qb=qb, valid=valid,
                               kw=kb_scr[pl.ds(ws, span), :], vw=vb_scr[pl.ds(ws, span), :]))
        for d in ch:
            d["s_win"] = jnp.where(d["valid"], _dot_nt(d["qb"], d["kw"]) * hd ** -0.5, NEG_INF)
            d["s_ctx"] = _dot_nt(d["qb"], ckb) * hd ** -0.5
        for d in ch:
            (e_win, e_ctx), den = _softmax_parts([d["s_win"], d["s_ctx"]], sink_ref[kvh * ATTN_GROUP + d["g"]])
            d["p"] = ((e_win / den).astype(BF16), (e_ctx / den).astype(BF16))
        for d in ch:
            d["o"] = _dot(d["p"][0], d["vw"]) + _dot(d["p"][1], cvb)
        for d in ch:
            o_ref[d["rows"], d["g"] * hd:(d["g"] + 1) * hd] = d["o"].astype(BF16)
        return carry

    lax.fori_loop(0, t // (ATTN_BLOCK * ATTN_LAT_NQB), blocks, 0)


def _attn_lat(qkv, cache_k, cache_v, slot, sink, q_norm, k_norm, y_ctx, n_seq, t, row_off):
    hd = ATTN_HEAD_DIM
    gw = ATTN_GROUP * hd
    kcol = ATTN_HEADS
    vcol = kcol + ATTN_KV_HEADS
    n_past = cache_k.shape[2]
    ck = cache_k.reshape(cache_k.shape[0], cache_k.shape[1], n_past, ATTN_KV_HEADS * hd)
    cv = cache_v.reshape(ck.shape)
    cos, sin_a, sin_b = _rope_tables(t)
    tab = pl.BlockSpec((t, hd), lambda b, h: (0, 0))
    return pl.pallas_call(
        _attn_lat_kernel,
        grid=(n_seq, ATTN_KV_HEADS),
        in_specs=[
            pl.BlockSpec(memory_space=pltpu.SMEM),
            pl.BlockSpec((t, gw), lambda b, h: (b + row_off, h)),
            pl.BlockSpec((t, hd), lambda b, h: (b + row_off, kcol + h)),
            pl.BlockSpec((t, hd), lambda b, h: (b + row_off, vcol + h)),
            pl.BlockSpec((1, 1, n_past, hd), lambda b, h: (b, slot, 0, h)),
            pl.BlockSpec((1, 1, n_past, hd), lambda b, h: (b, slot, 0, h)),
            pl.BlockSpec((1, hd), lambda b, h: (0, 0)),
            pl.BlockSpec((1, hd), lambda b, h: (0, 0)),
            tab, tab, tab,
            pl.BlockSpec(memory_space=pl.ANY),
        ],
        out_specs=pl.BlockSpec((t, gw), lambda b, h: (b + row_off, h)),
        out_shape=jax.ShapeDtypeStruct(y_ctx.shape, BF16),
        input_output_aliases={11: 0},
        scratch_shapes=[pltpu.VMEM((t, hd), BF16), pltpu.VMEM((t, hd), BF16)],
        compiler_params=_cparams(2),
        name="attn_lat",
    )(sink, qkv, qkv, qkv, ck, cv, q_norm[None, :], k_norm[None, :], cos, sin_a, sin_b, y_ctx)


def _cumsum_rows(x, reverse):
    n = x.shape[0]
    row = lax.broadcasted_iota(jnp.int32, x.shape, 0)
    sh = 1
    while sh < n:
        if reverse:
            x = x + jnp.where(row < n - sh, pltpu.roll(x, n - sh, 0), 0.0)
        else:
            x = x + jnp.where(row >= sh, pltpu.roll(x, sh, 0), 0.0)
        sh *= 2
    return x


def _log_sigmoid(x):
    return jnp.minimum(x, 0.0) - jnp.log(1.0 + jnp.exp(-jnp.abs(x)))


def _mlstm_kernel(has_init, emit_state, hp, *refs):
    refs = list(refs)
    q_ref, k_ref, v_ref, og_ref, gt_ref, bg_ref, on_ref = refs[:7]
    refs = refs[7:]
    if has_init:
        c0_ref, n0_ref, m0_ref = refs[:3]
        refs = refs[3:]
    _y_prev, y_ref = refs[:2]
    refs = refs[2:]
    if emit_state:
        co_ref, no_ref, mo_ref = refs[:3]
        refs = refs[3:]
    hf_scr, hb_scr = refs

    L = MLSTM_CHUNK
    t = q_ref.shape[0]
    nc = t // L
    dk, dv = MLSTM_DK, MLSTM_DV
    row = lax.broadcasted_iota(jnp.int32, (L, L), 0)
    col = lax.broadcasted_iota(jnp.int32, (L, L), 1)
    lane = lax.broadcasted_iota(jnp.int32, (L, LANES), 1)

    def chunk_step(chains):
        st = []
        for z, hh, c, C, n, m in chains:
            rows = pl.ds(pl.multiple_of(c * L, L), L)
            qc = q_ref[rows, hh * dk:(hh + 1) * dk] * dk ** -0.5
            kc = k_ref[rows, hh * dk:(hh + 1) * dk]
            vc = v_ref[rows, hh * dv:(hh + 1) * dv]
            G = gt_ref[rows, hh * LANES:(hh + 1) * LANES] + bg_ref[hh]
            CUM = _cumsum_rows(_log_sigmoid(G), reverse=(z == 1))
            ci, cf = 2 * z, 2 * z + 1
            ZT = jnp.where(lane == cf, CUM, G).T
            i_row, cum_row = ZT[ci:ci + 1, :L], ZT[cf:cf + 1, :L]
            i_col, cum_col = G[:, ci:ci + 1], CUM[:, cf:cf + 1]
            mask = (col <= row) if z == 0 else (col >= row)
            dmat = jnp.where(mask, cum_col - cum_row + i_row, NEG_INF)
            inter = cum_col + m
            m_c = jnp.maximum(inter, jnp.max(dmat, axis=-1, keepdims=True))
            bl = cum_row[:, L - 1:L] if z == 0 else cum_row[:, 0:1]
            m_new = jnp.maximum(bl + m, jnp.max(bl - cum_row + i_row, axis=-1, keepdims=True))
            kw = kc * jnp.exp(bl - cum_col + i_col - m_new)
            st.append(dict(
                z=z, hh=hh, rows=rows, C=C, n=n, qc=qc, m_c=m_c, m_new=m_new,
                qb=qc.astype(BF16), kb=kc.astype(BF16), vb=vc.astype(BF16), Cb=C.astype(BF16),
                w=jnp.exp(dmat - m_c), a=jnp.exp(inter - m_c), decay=jnp.exp(bl + m - m_new),
                kw=kw, kwb=kw.astype(BF16)))
        for d in st:
            d["qk"] = _dot_nt(d["qb"], d["kb"])
            d["qC"] = _dot(d["qb"], d["Cb"])
            d["kv"] = _dot_tn(d["kwb"], d["vb"])
        for d in st:
            d["s"] = d["qk"] * d["w"]
        for d in st:
            d["sv"] = _dot(d["s"].astype(BF16), d["vb"])
        out = []
        for d in st:
            num = d["sv"] + d["a"] * d["qC"]
            den = jnp.sum(d["s"], axis=-1, keepdims=True) + d["a"] * jnp.sum(d["qc"] * d["n"], axis=-1, keepdims=True)
            scr = hf_scr if d["z"] == 0 else hb_scr
            scr[d["rows"], d["hh"] * dv:(d["hh"] + 1) * dv] = num / jnp.maximum(jnp.abs(den), jnp.exp(-d["m_c"]))
            out += [d["decay"] * d["C"] + d["kv"],
                    d["decay"] * d["n"] + jnp.sum(d["kw"], axis=0, keepdims=True),
                    d["m_new"]]
        return out

    def body(ci, carry):
        chains = []
        for hh in range(hp):
            Cf, nf, mf, Cb, nb, mb = carry[6 * hh:6 * hh + 6]
            chains.append((0, hh, ci, Cf, nf, mf))
            chains.append((1, hh, nc - 1 - ci, Cb, nb, mb))
        return tuple(chunk_step(chains))

    init = []
    for hh in range(hp):
        for z in range(2):
            if has_init:
                init += [c0_ref[0, z, hh], n0_ref[0, z, hh], m0_ref[0, z, hh][:, 0:1]]
            else:
                init += [jnp.zeros((dk, dv), F32), jnp.zeros((1, dk), F32), jnp.zeros((1, 1), F32)]
    final = lax.fori_loop(0, nc, body, tuple(init))

    for hh in range(hp):
        cols = slice(hh * dv, (hh + 1) * dv)
        hs = hf_scr[:, cols] + hb_scr[:, cols]
        y_ref[:, cols] = (_rms_w(hs, on_ref[:, cols]) * _sigmoid(og_ref[:, cols])).astype(BF16)
        if emit_state:
            for z in range(2):
                C, n, m = final[6 * hh + 3 * z:6 * hh + 3 * z + 3]
                co_ref[0, z, hh] = C
                no_ref[0, z, hh] = n
                mo_ref[0, z, hh] = jnp.broadcast_to(m, (1, LANES))


def _mlstm(proj, gates, b_gate_h, out_norm, state0, y_prev, n_seq, t, row_off, emit_state, hp):
    h_, dk, dv = MLSTM_HEADS, MLSTM_DK, MLSTM_DV
    nhb = h_ // hp
    has_init = state0 is not None
    in_specs = [
        pl.BlockSpec((t, hp * dk), lambda b, h: (b + row_off, h)),
        pl.BlockSpec((t, hp * dk), lambda b, h: (b + row_off, nhb + h)),
        pl.BlockSpec((t, hp * dv), lambda b, h: (b + row_off, nhb + h)),
        pl.BlockSpec((t, hp * dv), lambda b, h: (b + row_off, 2 * nhb + h)),
        pl.BlockSpec((t, hp * LANES), lambda b, h: (b + row_off, h)),
        pl.BlockSpec((hp, 1, LANES), lambda b, h: (h, 0, 0)),
        pl.BlockSpec((1, hp * dv), lambda b, h: (0, h)),
    ]
    args = [proj, proj, proj, proj, gates, b_gate_h[:, None, :], out_norm[None, :]]
    if has_init:
        c0, n0, m0 = state0
        in_specs += [
            pl.BlockSpec((1, 2, hp, dk, dv), lambda b, h: (b, 0, h, 0, 0)),
            pl.BlockSpec((1, 2, hp, 1, dk), lambda b, h: (b, 0, h, 0, 0)),
            pl.BlockSpec((1, 2, hp, 1, LANES), lambda b, h: (b, 0, h, 0, 0)),
        ]
        args += [c0, n0.reshape(n_seq, 2, h_, 1, dk),
                 jnp.broadcast_to(m0[..., None, None], (n_seq, 2, h_, 1, LANES))]
    in_specs.append(pl.BlockSpec(memory_space=pl.ANY))
    args.append(y_prev)
    aliases = {len(args) - 1: 0}
    out_specs = [pl.BlockSpec((t, hp * dv), lambda b, h: (b + row_off, h))]
    out_shape = [jax.ShapeDtypeStruct((N_TOK, D_MODEL), BF16)]
    if emit_state:
        out_specs += [
            pl.BlockSpec((1, 2, hp, dk, dv), lambda b, h: (b, 0, h, 0, 0)),
            pl.BlockSpec((1, 2, hp, 1, dk), lambda b, h: (b, 0, h, 0, 0)),
            pl.BlockSpec((1, 2, hp, 1, LANES), lambda b, h: (b, 0, h, 0, 0)),
        ]
        out_shape += [
            jax.ShapeDtypeStruct((n_seq, 2, h_, dk, dv), F32),
            jax.ShapeDtypeStruct((n_seq, 2, h_, 1, dk), F32),
            jax.ShapeDtypeStruct((n_seq, 2, h_, 1, LANES), F32),
        ]
    return pl.pallas_call(
        functools.partial(_mlstm_kernel, has_init, emit_state, hp),
        grid=(n_seq, nhb),
        in_specs=in_specs,
        out_specs=out_specs,
        out_shape=out_shape,
        input_output_aliases=aliases,
        scratch_shapes=[pltpu.VMEM((t, hp * dv), F32), pltpu.VMEM((t, hp * dv), F32)],
        compiler_params=_cparams(2),
        name="mlstm",
    )(*args)


def _rwkv_mix_kernel(x_ref, xp_ref, xn_ref, shift_ref, scale_ref, g_ref, mu_ref, o_ref, h_scr):
    i = pl.program_id(0)
    tm = x_ref.shape[0]
    g, sh, sc = g_ref[...], shift_ref[0], scale_ref[0]
    n_ctx_tiles = N_CTX_TOK // tm
    tiles_per_lat = DEC_SEQ // tm
    tile_in_seq = jnp.where(i < n_ctx_tiles, i % (SEQ // tm), (i - n_ctx_tiles) % tiles_per_lat)
    tiles_in_seq = jnp.where(i < n_ctx_tiles, SEQ // tm, tiles_per_lat)
    has_prev = (tile_in_seq > 0).astype(F32)
    has_next = (tile_in_seq < tiles_in_seq - 1).astype(F32)
    h_scr[0:8, :] = _modulated(xp_ref[...], g, sh, sc) * has_prev
    h_scr[8 + tm:16 + tm, :] = _modulated(xn_ref[...], g, sh, sc) * has_next

    def fill(c, carry):
        r0 = pl.multiple_of(c * ROW_CHUNK, ROW_CHUNK)
        h_scr[pl.ds(8 + r0, ROW_CHUNK), :] = _modulated(x_ref[pl.ds(r0, ROW_CHUNK), :], g, sh, sc)
        return carry

    lax.fori_loop(0, tm // ROW_CHUNK, fill, 0, unroll=4)

    def mix(c, carry):
        r0 = pl.multiple_of(c * ROW_CHUNK, ROW_CHUNK)
        ext = h_scr[pl.ds(r0, ROW_CHUNK + 16), :]
        h = ext[8:8 + ROW_CHUNK]
        prev = pltpu.roll(ext, 1, 0)[8:8 + ROW_CHUNK]
        nxt = pltpu.roll(ext, ROW_CHUNK + 15, 0)[8:8 + ROW_CHUNK]
        cs = 0.5 * (prev + nxt) - h
        for p in range(o_ref.shape[0]):
            o_ref[p, pl.ds(r0, ROW_CHUNK), :] = (h + cs * mu_ref[p:p + 1, :]).astype(BF16)
        return carry

    lax.fori_loop(0, tm // ROW_CHUNK, mix, 0, unroll=2)


def _rwkv_mix(x, mods3, g_row, mu):
    tm = 256
    nb8 = tm // 8
    last = N_TOK // 8 - 1
    return pl.pallas_call(
        _rwkv_mix_kernel,
        grid=(N_TOK // tm,),
        in_specs=[
            pl.BlockSpec((tm, D_MODEL), lambda i: (i, 0)),
            pl.BlockSpec((8, D_MODEL), lambda i: (jnp.maximum(i * nb8 - 1, 0), 0)),
            pl.BlockSpec((8, D_MODEL), lambda i: (jnp.minimum((i + 1) * nb8, last), 0)),
            pl.BlockSpec((1, 1, D_MODEL), lambda i: (_mod_row(i, tm), 0, 3)),
            pl.BlockSpec((1, 1, D_MODEL), lambda i: (_mod_row(i, tm), 0, 4)),
            pl.BlockSpec((1, D_MODEL), lambda i: (0, 0)),
            pl.BlockSpec((6, D_MODEL), lambda i: (0, 0)),
        ],
        out_specs=pl.BlockSpec((6, tm, D_MODEL), lambda i: (0, i, 0)),
        out_shape=jax.ShapeDtypeStruct((6, N_TOK, D_MODEL), BF16),
        scratch_shapes=[pltpu.VMEM((tm + 16, D_MODEL), F32)],
        compiler_params=_cparams(1),
        name="rwkv_mix",
    )(x, x, x, mods3, mods3, g_row, mu)


def _lora_kernel(act, post, x_ref, a_ref, b_ref, bias_ref, o_ref):
    u = _dot(x_ref[0], a_ref[0])
    if act == "tanh":
        u = jnp.tanh(u)
    elif act == "sigmoid":
        u = _sigmoid(u)
    y = _dot(u.astype(BF16), b_ref[0]) + bias_ref[0]
    if post == "log_decay":
        y = -(_sigmoid(y) * jnp.exp(jnp.float32(-0.5)))
    elif post == "sigmoid":
        y = _sigmoid(y)
    o_ref[0] = y


def _lora(xs, p, wa, wb, bias, act, post, name):
    tm = 512
    nz, _, r = wa.shape
    return pl.pallas_call(
        functools.partial(_lora_kernel, act, post),
        grid=(nz, N_TOK // tm),
        in_specs=[
            pl.BlockSpec((1, tm, D_MODEL), lambda z, i: (p, i, 0)),
            pl.BlockSpec((1, D_MODEL, r), lambda z, i: (z, 0, 0)),
            pl.BlockSpec((1, r, D_MODEL), lambda z, i: (z, 0, 0)),
            pl.BlockSpec((1, 1, D_MODEL), lambda z, i: (z, 0, 0)),
        ],
        out_specs=pl.BlockSpec((1, tm, D_MODEL), lambda z, i: (z, i, 0)),
        out_shape=jax.ShapeDtypeStruct((nz, N_TOK, D_MODEL), F32),
        compiler_params=_cparams(2),
        name=name,
    )(xs, wa, wb, bias)


def _rkv_kernel(x_ref, w_ref, o_ref):
    x = x_ref[0]
    for c in range(o_ref.shape[2] // MM_COLS):
        cols = slice(c * MM_COLS, (c + 1) * MM_COLS)
        o_ref[0, :, cols] = _dot(x, w_ref[0, :, cols])


def _rkv(xs, w_rkv):
    tm = 512
    return pl.pallas_call(
        _rkv_kernel,
        grid=(3, N_TOK // tm),
        in_specs=[pl.BlockSpec((1, tm, D_MODEL), lambda p, i: (p, i, 0)),
                  pl.BlockSpec((1, D_MODEL, D_MODEL), lambda p, i: (p, 0, 0))],
        out_specs=pl.BlockSpec((1, tm, D_MODEL), lambda p, i: (p, i, 0)),
        out_shape=jax.ShapeDtypeStruct((3, N_TOK, D_MODEL), F32),
        compiler_params=_cparams(2),
        name="rwkv_rkv",
    )(xs, w_rkv)


def _head_sum(x, first_head):
    s0 = jnp.sum(jnp.where(first_head, x, 0.0), axis=-1, keepdims=True)
    s1 = jnp.sum(jnp.where(first_head, 0.0, x), axis=-1, keepdims=True)
    return jnp.where(first_head, s0, s1)


def _rwkv_kernel(has_init, emit_state, npp, *refs):
    refs = list(refs)
    r_ref, k_ref, v_ref, lw_ref, a_ref, g_ref, kk_w_ref, ka_ref, rk_ref, lng_ref, lnb_ref = refs[:11]
    refs = refs[11:]
    if has_init:
        s0_ref = refs[0]
        refs = refs[1:]
    _y_prev, y_ref = refs[:2]
    refs = refs[2:]
    if emit_state:
        so_ref = refs[0]
        refs = refs[1:]
    kk_scr, yf_scr, yb_scr = refs

    L = RWKV_CHUNK
    N = RWKV_HEAD
    t = r_ref.shape[1]
    nc = t // L
    row = lax.broadcasted_iota(jnp.int32, (L, L), 0)
    col = lax.broadcasted_iota(jnp.int32, (L, L), 1)
    first_head = lax.broadcasted_iota(jnp.int32, (L, LANES), 1) < N
    sq_row = lax.broadcasted_iota(jnp.int32, (LANES, LANES), 0)
    sq_col = lax.broadcasted_iota(jnp.int32, (LANES, LANES), 1)
    same_head = (sq_row < N) == (sq_col < N)

    first_head_t = lax.broadcasted_iota(jnp.int32, (t, LANES), 1) < N

    for pp in range(npp):
        lanes = slice(pp * LANES, (pp + 1) * LANES)
        kkx = k_ref[0, :, lanes] * kk_w_ref[:, lanes]
        kk_scr[:, lanes] = kkx / jnp.maximum(jnp.sqrt(_head_sum(kkx * kkx, first_head_t)), 1e-12)

    def chunk_step(chains):
        heads = (first_head, jnp.logical_not(first_head))
        pick = lambda x0, x1: jnp.where(first_head, x0, x1)
        st = []
        for z, pp, c, S in chains:
            lanes = slice(pp * LANES, (pp + 1) * LANES)
            sl = pl.ds(pl.multiple_of(c * L, L), L)
            r_, k_, v_ = r_ref[0, sl, lanes], k_ref[0, sl, lanes], v_ref[0, sl, lanes]
            lw, a_ = lw_ref[z, sl, lanes], a_ref[z, sl, lanes]
            kk = kk_scr[sl, lanes]
            kd = k_ * (1.0 + (a_ - 1.0) * ka_ref[:, lanes])
            b_ = kk * a_
            cum = _cumsum_rows(lw, reverse=(z == 1))
            cum_end = cum[L - 1:L, :] if z == 0 else cum[0:1, :]
            kp = (kk * jnp.exp(cum - lw)).astype(BF16)
            rp = (r_ * jnp.exp(cum)).astype(BF16)
            inv = jnp.exp(-cum)
            to_end = jnp.exp(cum_end - cum)
            zero = jnp.zeros_like(kp)
            st.append(dict(
                z=z, pp=pp, sl=sl, lanes=lanes, S=S, kp=kp, rp=rp, cum_end=cum_end,
                kt=(kd * inv).astype(BF16), bt=(b_ * inv).astype(BF16),
                kb_end=jnp.concatenate([kd * to_end, -(b_ * to_end)], axis=0).astype(BF16),
                vb=v_.astype(BF16), Sb=S.astype(BF16),
                probe=[jnp.concatenate([jnp.where(hm, kp, zero), jnp.where(hm, rp, zero)], axis=0) for hm in heads],
                strict=(col < row) if z == 0 else (col > row),
                incl=(col <= row) if z == 0 else (col >= row)))
        for d in st:
            d["gk"] = [_dot_nt(pr, d["kt"]) for pr in d["probe"]]
            d["gb"] = [_dot_nt(pr, d["bt"]) for pr in d["probe"]]
        for d in st:
            d["g1"] = [jnp.where(d["strict"], g[:L], 0.0).astype(BF16) for g in d["gk"]]
            d["n"] = [jnp.where(d["strict"], g[:L], 0.0).astype(BF16) for g in d["gb"]]
            d["g3"] = [jnp.where(d["incl"], g[L:], 0.0).astype(BF16) for g in d["gk"]]
            d["g4"] = [jnp.where(d["incl"], g[L:], 0.0).astype(BF16) for g in d["gb"]]
        for d in st:
            d["g1v"] = [_dot(g, d["vb"]) for g in d["g1"]]
            d["kps"] = _dot_nt(d["kp"], d["Sb"])
            d["rps"] = _dot_nt(d["rp"], d["Sb"])
        for d in st:
            d["u"] = d["kps"] + pick(*d["g1v"])
        sign = -1.0
        p = 1
        while p < L:
            for d in st:
                ub = d["u"].astype(BF16)
                d["nu"] = [_dot(m, ub) for m in d["n"]]
                if 2 * p < L:
                    d["n"] = [_dot(m, m).astype(BF16) for m in d["n"]]
            for d in st:
                d["u"] = d["u"] + sign * pick(*d["nu"])
            sign = 1.0
            p *= 2
        for d in st:
            ub = d["u"].astype(BF16)
            d["y3"] = [_dot(g, d["vb"]) for g in d["g3"]]
            d["y4"] = [_dot(g, ub) for g in d["g4"]]
            d["upd"] = _dot_tn(jnp.concatenate([d["vb"], ub], axis=0), d["kb_end"])
        out = []
        for d in st:
            y = d["rps"] + pick(d["y3"][0] - d["y4"][0], d["y3"][1] - d["y4"][1])
            (yf_scr if d["z"] == 0 else yb_scr)[d["sl"], d["lanes"]] = y
            out.append(d["S"] * jnp.exp(d["cum_end"]) + jnp.where(same_head, d["upd"], 0.0))
        return out

    def body(ci, carry):
        chains = []
        for pp in range(npp):
            chains.append((0, pp, ci, carry[2 * pp]))
            chains.append((1, pp, nc - 1 - ci, carry[2 * pp + 1]))
        return tuple(chunk_step(chains))

    init = []
    for pp in range(npp):
        for z in range(2):
            if has_init:
                s = s0_ref[0, z, pp]
                init.append(jnp.where(sq_row < N, s, pltpu.roll(s, N, 1)))
            else:
                init.append(jnp.zeros((LANES, LANES), F32))
    final = lax.fori_loop(0, nc, body, tuple(init))

    for pp in range(npp):
        lanes = slice(pp * LANES, (pp + 1) * LANES)
        r_, k_, v_ = r_ref[0, :, lanes], k_ref[0, :, lanes], v_ref[0, :, lanes]
        y = yf_scr[:, lanes] + yb_scr[:, lanes]
        mean = _head_sum(y, first_head_t) * (1.0 / N)
        yc = y - mean
        var = _head_sum(yc * yc, first_head_t) * (1.0 / N)
        yn = yc * lax.rsqrt(var + RWKV_GN_EPS) * lng_ref[:, lanes] + lnb_ref[:, lanes]
        ka = ka_ref[:, lanes]
        kd_sum = k_ * (1.0 + (a_ref[0, :, lanes] - 1.0) * ka) + k_ * (1.0 + (a_ref[1, :, lanes] - 1.0) * ka)
        bonus = _head_sum(r_ * kd_sum * rk_ref[:, lanes], first_head_t) * v_
        y_ref[:, lanes] = ((yn + bonus) * g_ref[:, lanes]).astype(BF16)
        if emit_state:
            for z in range(2):
                S = final[2 * pp + z]
                so_ref[0, z, pp] = (S + pltpu.roll(S, N, 1))[:, :N]


def _rwkv(rkv, lw, a, g, k_k, k_a, r_k, ln_g, ln_b, s0, y_prev, n_seq, t, row_off, emit_state, npp):
    has_init = s0 is not None
    w = npp * LANES
    tok = lambda lead, first: pl.BlockSpec((lead, t, w), lambda b, p: (first, b + row_off, p))
    par = pl.BlockSpec((1, w), lambda b, p: (0, p))
    in_specs = [tok(1, 0), tok(1, 1), tok(1, 2), tok(2, 0), tok(2, 0),
                pl.BlockSpec((t, w), lambda b, p: (b + row_off, p)),
                par, par, par, par, par]
    args = [rkv, rkv, rkv, lw, a, g, k_k[None, :], k_a[None, :], r_k.reshape(1, D_MODEL), ln_g[None, :], ln_b[None, :]]
    if has_init:
        in_specs.append(pl.BlockSpec((1, 2, npp, LANES, LANES), lambda b, p: (b, 0, p, 0, 0)))
        args.append(s0)
    in_specs.append(pl.BlockSpec(memory_space=pl.ANY))
    args.append(y_prev)
    aliases = {len(args) - 1: 0}
    out_specs = [pl.BlockSpec((t, w), lambda b, p: (b + row_off, p))]
    out_shape = [jax.ShapeDtypeStruct((N_TOK, D_MODEL), BF16)]
    if emit_state:
        out_specs.append(pl.BlockSpec((1, 2, npp, LANES, RWKV_HEAD), lambda b, p: (b, 0, p, 0, 0)))
        out_shape.append(jax.ShapeDtypeStruct((n_seq, 2, RWKV_PAIRS, LANES, RWKV_HEAD), F32))
    return pl.pallas_call(
        functools.partial(_rwkv_kernel, has_init, emit_state, npp),
        grid=(n_seq, RWKV_PAIRS // npp),
        in_specs=in_specs,
        out_specs=out_specs,
        out_shape=out_shape,
        input_output_aliases=aliases,
        scratch_shapes=[pltpu.VMEM((t, w), F32)] * 3,
        compiler_params=_cparams(2),
        name="rwkv",
    )(*args)


def _attn_layer(x, mods3, g_row, cache_k, cache_v, slot, w_qkv, q_norm, k_norm, sink, w_o):
    qkv = _mm_mod(x, mods3, 3, g_row, w_qkv.astype(BF16), 512, "attn_qkv")
    y, k_ctx, v_ctx = _attn_ctx(qkv, sink, q_norm, k_norm, BATCH, SEQ)
    y = _attn_lat(qkv, cache_k, cache_v, slot, sink, q_norm, k_norm, y, DEC_BATCH, DEC_SEQ, N_CTX_TOK // DEC_SEQ)
    x = _mm_res(y, w_o.astype(BF16), x, mods3, 5, "attn_out")
    shape = (BATCH, SEQ, ATTN_KV_HEADS, ATTN_HEAD_DIM)
    return x, k_ctx.reshape(shape), v_ctx.reshape(shape)


def _mlstm_layer(x, mods3, g_row, state_lat, w_in, w_gate, b_gate, out_norm, w_o):
    h_ = MLSTM_HEADS
    proj = _mm_mod(x, mods3, 3, g_row, w_in.astype(BF16), 256, "mlstm_in")
    wg = jnp.transpose(w_gate.reshape(D_MODEL, 4, h_), (0, 2, 1))
    wg = jnp.pad(wg, ((0, 0), (0, 0), (0, LANES - 4))).reshape(D_MODEL, h_ * LANES).astype(BF16)
    bg = jnp.pad(b_gate.reshape(4, h_).T, ((0, 0), (0, LANES - 4)))
    gates = _mm_mod(x, mods3, 3, g_row, wg, 512, "mlstm_gates")
    y, C, n, m = _mlstm(proj, gates, bg, out_norm, None, _token_zeros(), BATCH, SEQ, 0, True, MLSTM_HEADS)
    (y,) = _mlstm(proj, gates, bg, out_norm, state_lat, y, DEC_BATCH, DEC_SEQ, N_CTX_TOK // DEC_SEQ, False, MLSTM_HEADS // 2)
    x = _mm_res(y, w_o.astype(BF16), x, mods3, 5, "mlstm_out")
    return x, C, n[:, :, :, 0, :], m[:, :, :, 0, 0]


def _rwkv_layer(x, mods3, g_row, s0_lat, mu, w_rkv, w0, wA, wB, a0, aA, aB, gA, gB, k_k, k_a, r_k, ln_g, ln_b, w_o):
    xs = _rwkv_mix(x, mods3, g_row, mu)
    rkv = _rkv(xs, w_rkv.astype(BF16))
    pad_r = lambda w, axis: jnp.pad(w, [(0, 0) if ax != axis else (0, LANES - w.shape[axis]) for ax in range(w.ndim)])
    lw = _lora(xs, 3, pad_r(wA, 2).astype(BF16), pad_r(wB, 1).astype(BF16), w0[:, None, :], "tanh", "log_decay", "rwkv_w")
    a = _lora(xs, 4, pad_r(aA, 2).astype(BF16), pad_r(aB, 1).astype(BF16), a0[:, None, :], "none", "sigmoid", "rwkv_a")
    g = _lora(xs, 5, gA[None].astype(BF16), gB[None].astype(BF16), jnp.zeros((1, 1, D_MODEL), F32), "sigmoid", "none", "rwkv_g")[0]
    s0 = jnp.pad(s0_lat.reshape(DEC_BATCH, 2, RWKV_PAIRS, LANES, RWKV_HEAD), ((0, 0),) * 4 + ((0, LANES - RWKV_HEAD),))
    y, S = _rwkv(rkv, lw, a, g, k_k, k_a, r_k, ln_g, ln_b, None, _token_zeros(), BATCH, SEQ, 0, True, RWKV_PAIRS // 2)
    (y,) = _rwkv(rkv, lw, a, g, k_k, k_a, r_k, ln_g, ln_b, s0, y, DEC_BATCH, DEC_SEQ, N_CTX_TOK // DEC_SEQ, False, RWKV_PAIRS // 4)
    x = _mm_res(y, w_o.astype(BF16), x, mods3, 5, "rwkv_out")
    return x, S.reshape(BATCH, 2, RWKV_HEADS, RWKV_HEAD, RWKV_HEAD)


def kernel(x_prompt, x_sample, cache_k, cache_v, state_mlstm_C, state_mlstm_n, state_mlstm_m, state_rwkv, c, c_ctx, mod_w, mod_b, norm_g, ffn_w_in, ffn_w_out, attn_w_qkv, attn_q_norm, attn_k_norm, attn_sink, attn_w_o, mlstm_w_in, mlstm_w_gate, mlstm_b_gate, mlstm_out_norm, mlstm_w_o, rwkv_mu, rwkv_w_rkv, rwkv_w0, rwkv_wA, rwkv_wB, rwkv_a0, rwkv_aA, rwkv_aB, rwkv_gA, rwkv_gB, rwkv_k_k, rwkv_k_a, rwkv_r_k, rwkv_ln_g, rwkv_ln_b, rwkv_w_o):
    x = jnp.concatenate([x_prompt.reshape(N_CTX_TOK, D_MODEL), x_sample.reshape(N_LAT_TOK, D_MODEL)], axis=0)
    cond8 = jnp.concatenate([c_ctx[None, :], c, jnp.zeros((MOD_ROWS - 1 - DEC_BATCH, D_MODEL), F32)], axis=0)
    mods = _adaln_all(cond8, mod_w, mod_b)
    ffn_order = [(i, j) for i in range(DEPTH) for j in range(2)]
    w_in_b = ffn_w_in[0, 0].astype(BF16)
    w_out_b = ffn_w_out[0, 0].astype(BF16)

    def ffn(x, mods3, slot, g_row, k, w_in_b, w_out_b):
        nxt = (ffn_w_in, ffn_w_out) + ffn_order[k + 1] if k + 1 < len(ffn_order) else None
        res = _ffn(x, mods3, slot, g_row, w_in_b, w_out_b, nxt)
        return res if nxt is not None else (res[0], None, None)

    new_k, new_v, new_C, new_n, new_m, new_S = [], [], [], [], [], []
    for i in range(DEPTH):
        kind, slot = i % N_MIXERS, i // N_MIXERS
        mods3 = mods[i].reshape(MOD_ROWS, 1, N_MOD * D_MODEL)
        x, w_in_b, w_out_b = ffn(x, mods3, 0, norm_g[i, 0][None, :], 2 * i, w_in_b, w_out_b)
        g_row = norm_g[i, 1][None, :]
        if kind == 0:
            x, k_ctx, v_ctx = _attn_layer(x, mods3, g_row, cache_k, cache_v, slot, attn_w_qkv[slot], attn_q_norm[slot],
                                          attn_k_norm[slot], attn_sink[slot], attn_w_o[slot])
            new_k.append(k_ctx)
            new_v.append(v_ctx)
        elif kind == 1:
            state_lat = (state_mlstm_C[:, slot], state_mlstm_n[:, slot], state_mlstm_m[:, slot])
            x, C, n, m = _mlstm_layer(x, mods3, g_row, state_lat, mlstm_w_in[slot], mlstm_w_gate[slot], mlstm_b_gate[slot],
                                      mlstm_out_norm[slot], mlstm_w_o[slot])
            new_C.append(C)
            new_n.append(n)
            new_m.append(m)
        else:
            x, S = _rwkv_layer(x, mods3, g_row, state_rwkv[:, slot], rwkv_mu[slot], rwkv_w_rkv[slot], rwkv_w0[slot],
                               rwkv_wA[slot], rwkv_wB[slot], rwkv_a0[slot], rwkv_aA[slot], rwkv_aB[slot], rwkv_gA[slot],
                               rwkv_gB[slot], rwkv_k_k[slot], rwkv_k_a[slot], rwkv_r_k[slot], rwkv_ln_g[slot],
                               rwkv_ln_b[slot], rwkv_w_o[slot])
            new_S.append(S)
        x, w_in_b, w_out_b = ffn(x, mods3, 6, norm_g[i, 2][None, :], 2 * i + 1, w_in_b, w_out_b)

    y_prompt = x[:N_CTX_TOK].reshape(BATCH, SEQ, D_MODEL)
    y_sample = x[N_CTX_TOK:].reshape(DEC_BATCH, DEC_SEQ, D_MODEL)
    return (y_prompt, y_sample, jnp.stack(new_k, axis=1), jnp.stack(new_v, axis=1),
            jnp.stack(new_C, axis=1), jnp.stack(new_n, axis=1), jnp.stack(new_m, axis=1), jnp.stack(new_S, axis=1))
```

```python
import functools

import jax
import jax.numpy as jnp
from jax import lax
from jax.experimental import pallas as pl
from jax.experimental.pallas import tpu as pltpu

D_MODEL = 2048
BATCH = 32
SEQ = 256
DEPTH = 4
DEC_BATCH = 4
DEC_SEQ = 1024
GRID_W = 64
N_MIXERS = 3
N_MOD = 9
D_FF = 5632
EPS = 1e-6

ATTN_HEADS = 16
ATTN_KV_HEADS = 4
ATTN_GROUP = ATTN_HEADS // ATTN_KV_HEADS
ATTN_HEAD_DIM = D_MODEL // ATTN_HEADS
ATTN_WINDOW = 128
ATTN_BLOCK = 128
ROPE_THETA = 10000.0

MLSTM_HEADS = 8
MLSTM_DV = D_MODEL // MLSTM_HEADS
MLSTM_DK = MLSTM_DV // 2
MLSTM_CHUNK = 128

RWKV_HEAD = 64
RWKV_HEADS = D_MODEL // RWKV_HEAD
RWKV_PAIRS = RWKV_HEADS // 2
RWKV_CHUNK = 64
RWKV_GN_EPS = 64e-5

N_CTX_TOK = BATCH * SEQ
N_LAT_TOK = DEC_BATCH * DEC_SEQ
N_TOK = N_CTX_TOK + N_LAT_TOK
MOD_ROWS = 8
LANES = 128

VMEM_LIMIT = 60 * 1024 * 1024
BF16 = jnp.bfloat16
F32 = jnp.float32
NEG_INF = float("-inf")
MM_COLS = 512


def _cparams(n_axes):
    return pltpu.CompilerParams(dimension_semantics=("arbitrary",) * n_axes, vmem_limit_bytes=VMEM_LIMIT)


def _mod_row(i, tm):
    n_ctx_tiles = N_CTX_TOK // tm
    tiles_per_lat = DEC_SEQ // tm
    return jnp.where(i < n_ctx_tiles, 0, 1 + (i - n_ctx_tiles) // tiles_per_lat)


def _mod_spec(slot, tm):
    return pl.BlockSpec((1, 1, D_MODEL), lambda i, j, s=slot: (_mod_row(i, tm), 0, s))


def _dot(a, b):
    return jnp.dot(a, b, preferred_element_type=F32)


def _dot_nt(a, b):
    return lax.dot_general(a, b, (((1,), (1,)), ((), ())), preferred_element_type=F32)


def _dot_tn(a, b):
    return lax.dot_general(a, b, (((0,), (0,)), ((), ())), preferred_element_type=F32)


def _sigmoid(x):
    return jax.nn.sigmoid(x)


def _modulated(x, g, shift, scale):
    ms = jnp.mean(x * x, axis=-1, keepdims=True)
    y = x * lax.rsqrt(ms + EPS) * g
    return y * (1.0 + scale) + shift


def _adaln_kernel(cond_ref, w_ref, b_ref, o_ref):
    c = cond_ref[...]
    a = (c * _sigmoid(c)).astype(BF16)
    o_ref[0] = _dot(a, w_ref[0].astype(BF16)) + b_ref[0]


def _adaln_all(cond8, mod_w, mod_b):
    tn = 1024
    n_out = N_MOD * D_MODEL
    return pl.pallas_call(
        _adaln_kernel,
        grid=(DEPTH, n_out // tn),
        in_specs=[
            pl.BlockSpec((MOD_ROWS, D_MODEL), lambda l, j: (0, 0)),
            pl.BlockSpec((1, D_MODEL, tn), lambda l, j: (l, 0, j)),
            pl.BlockSpec((1, 1, tn), lambda l, j: (l, 0, j)),
        ],
        out_specs=pl.BlockSpec((1, MOD_ROWS, tn), lambda l, j: (l, 0, j)),
        out_shape=jax.ShapeDtypeStruct((DEPTH, MOD_ROWS, n_out), F32),
        compiler_params=_cparams(2),
        name="adaln",
    )(cond8, mod_w, mod_b.reshape(DEPTH, 1, n_out))


FFN_TM = 1024
FFN_TF = 512
ROW_CHUNK = 32


def _cast_rows(total, n_steps):
    r = 16
    while total % r or total // r > n_steps:
        r *= 2
    return r


def _ffn_kernel(has_next, *refs):
    x_ref, shift_ref, scale_ref, gate_ref, g_ref, wg_ref, wu_ref, wo_ref = refs[:8]
    refs = refs[8:]
    if has_next:
        nin_ref, nout_ref, o_ref, cin_ref, cout_ref, h_scr = refs
    else:
        o_ref, h_scr = refs
    f = pl.program_id(1)

    @pl.when(f == 0)
    def _():
        def rows(c, carry):
            r = pl.ds(pl.multiple_of(c * ROW_CHUNK, ROW_CHUNK), ROW_CHUNK)
            h_scr[r, :] = _modulated(x_ref[r, :], g_ref[...], shift_ref[0], scale_ref[0]).astype(BF16)
            return carry
        lax.fori_loop(0, x_ref.shape[0] // ROW_CHUNK, rows, 0, unroll=8)

    h = h_scr[...]
    a = _dot(h, wg_ref[...])
    b = _dot(h, wu_ref[...])
    act = ((a * _sigmoid(a)) * b).astype(BF16)
    if has_next:
        cin_ref[...] = nin_ref[0, 0].astype(BF16)
        cout_ref[...] = nout_ref[0, 0].astype(BF16)
    for c in range(o_ref.shape[1] // MM_COLS):
        cols = slice(c * MM_COLS, (c + 1) * MM_COLS)
        o_ref[:, cols] = jnp.where(f == 0, 0.0, o_ref[:, cols]) + _dot(act, wo_ref[:, cols])

    @pl.when(f == pl.num_programs(1) - 1)
    def _():
        o_ref[...] = x_ref[...] + (0.5 * gate_ref[0]) * o_ref[...]


def _ffn(x, mods3, slot, g_row, w_in, w_out, nxt):
    tm, tf = FFN_TM, FFN_TF
    nf = D_FF // tf
    in_specs = [
        pl.BlockSpec((tm, D_MODEL), lambda i, f: (i, 0)),
        _mod_spec(slot, tm), _mod_spec(slot + 1, tm), _mod_spec(slot + 2, tm),
        pl.BlockSpec((1, D_MODEL), lambda i, f: (0, 0)),
        pl.BlockSpec((D_MODEL, tf), lambda i, f: (0, f)),
        pl.BlockSpec((D_MODEL, tf), lambda i, f: (0, f + nf)),
        pl.BlockSpec((tf, D_MODEL), lambda i, f: (f, 0)),
    ]
    args = [x, mods3, mods3, mods3, g_row, w_in, w_in, w_out]
    out_specs = [pl.BlockSpec((tm, D_MODEL), lambda i, f: (i, 0))]
    out_shape = [jax.ShapeDtypeStruct((N_TOK, D_MODEL), F32)]
    if nxt is not None:
        n_in, n_out, layer, which = nxt
        n_steps = (N_TOK // tm) * nf
        rows_in, rows_out = _cast_rows(D_MODEL, n_steps), _cast_rows(D_FF, n_steps)
        last_in, last_out = D_MODEL // rows_in - 1, D_FF // rows_out - 1
        blk_in = lambda i, f: jnp.minimum(i * nf + f, last_in)
        blk_out = lambda i, f: jnp.minimum(i * nf + f, last_out)
        in_specs += [
            pl.BlockSpec((1, 1, rows_in, 2 * D_FF), lambda i, f: (layer, which, blk_in(i, f), 0)),
            pl.BlockSpec((1, 1, rows_out, D_MODEL), lambda i, f: (layer, which, blk_out(i, f), 0)),
        ]
        args += [n_in, n_out]
        out_specs += [
            pl.BlockSpec((rows_in, 2 * D_FF), lambda i, f: (blk_in(i, f), 0)),
            pl.BlockSpec((rows_out, D_MODEL), lambda i, f: (blk_out(i, f), 0)),
        ]
        out_shape += [jax.ShapeDtypeStruct((D_MODEL, 2 * D_FF), BF16), jax.ShapeDtypeStruct((D_FF, D_MODEL), BF16)]
    return pl.pallas_call(
        functools.partial(_ffn_kernel, nxt is not None),
        grid=(N_TOK // tm, nf),
        in_specs=in_specs,
        out_specs=out_specs,
        out_shape=out_shape,
        scratch_shapes=[pltpu.VMEM((tm, D_MODEL), BF16)],
        compiler_params=_cparams(2),
        name="ffn",
    )(*args)


def _resident(shape):
    return pl.BlockSpec(shape, lambda *_: (0,) * len(shape), pipeline_mode=pl.Buffered(1))


def _mm_mod_kernel(x_ref, shift_ref, scale_ref, g_ref, w_ref, o_ref):
    h = _modulated(x_ref[...], g_ref[...], shift_ref[0], scale_ref[0]).astype(BF16)
    for c in range(o_ref.shape[1] // MM_COLS):
        cols = slice(c * MM_COLS, (c + 1) * MM_COLS)
        o_ref[:, cols] = _dot(h, w_ref[:, cols])


def _mm_mod(x, mods3, slot, g_row, w, tm, name):
    n = w.shape[1]
    mod = lambda s: pl.BlockSpec((1, 1, D_MODEL), lambda i: (_mod_row(i, tm), 0, s))
    return pl.pallas_call(
        _mm_mod_kernel,
        grid=(N_TOK // tm,),
        in_specs=[
            pl.BlockSpec((tm, D_MODEL), lambda i: (i, 0)),
            mod(slot), mod(slot + 1),
            _resident((1, D_MODEL)),
            _resident((D_MODEL, n)),
        ],
        out_specs=pl.BlockSpec((tm, n), lambda i: (i, 0)),
        out_shape=jax.ShapeDtypeStruct((N_TOK, n), F32),
        compiler_params=_cparams(1),
        name=name,
    )(x, mods3, mods3, g_row, w)


def _mm_res_kernel(y_ref, w_ref, x_ref, gate_ref, o_ref):
    y = y_ref[...]
    for c in range(o_ref.shape[1] // MM_COLS):
        cols = slice(c * MM_COLS, (c + 1) * MM_COLS)
        o_ref[:, cols] = x_ref[:, cols] + gate_ref[0][:, cols] * _dot(y, w_ref[:, cols])


def _mm_res(y, w, x, mods3, slot, name):
    tm = 512
    k = y.shape[1]
    return pl.pallas_call(
        _mm_res_kernel,
        grid=(N_TOK // tm,),
        in_specs=[
            pl.BlockSpec((tm, k), lambda i: (i, 0)),
            _resident((k, D_MODEL)),
            pl.BlockSpec((tm, D_MODEL), lambda i: (i, 0)),
            pl.BlockSpec((1, 1, D_MODEL), lambda i: (_mod_row(i, tm), 0, slot)),
        ],
        out_specs=pl.BlockSpec((tm, D_MODEL), lambda i: (i, 0)),
        out_shape=jax.ShapeDtypeStruct((N_TOK, D_MODEL), F32),
        compiler_params=_cparams(1),
        name=name,
    )(y, w, x, mods3)


def _token_zeros():
    return jnp.zeros((N_TOK, D_MODEL), BF16)


def _rms_w(x, w):
    return x * lax.rsqrt(jnp.mean(x * x, axis=-1, keepdims=True) + EPS) * w


def _softmax_parts(scores, sink):
    m = sink
    for s in scores:
        m = jnp.maximum(jnp.max(s, axis=-1, keepdims=True), m)
    es = [jnp.exp(s - m) for s in scores]
    den = jnp.exp(sink - m)
    for e in es:
        den = den + jnp.sum(e, axis=-1, keepdims=True)
    return es, den


ATTN_CTX_KVH = 4


def _attn_ctx_kernel(sink_ref, q_ref, k_ref, v_ref, qn_ref, kn_ref, _y_init, o_ref, ko_ref, vo_ref):
    h2 = pl.program_id(1)
    hd = ATTN_HEAD_DIM
    chains = []
    for kv in range(ATTN_CTX_KVH):
        cols = slice(kv * hd, (kv + 1) * hd)
        k = _rms_w(k_ref[:, cols], kn_ref[...])
        v = v_ref[:, cols]
        ko_ref[:, cols] = k
        vo_ref[:, cols] = v
        kb, vb = k.astype(BF16), v.astype(BF16)
        for g in range(ATTN_GROUP):
            head = kv * ATTN_GROUP + g
            q = _rms_w(q_ref[:, head * hd:(head + 1) * hd], qn_ref[...]).astype(BF16)
            chains.append(dict(head=head, q=q, kb=kb, vb=vb))
    for d in chains:
        d["s"] = _dot_nt(d["q"], d["kb"]) * hd ** -0.5
    for d in chains:
        (e,), den = _softmax_parts([d["s"]], sink_ref[h2 * ATTN_CTX_KVH * ATTN_GROUP + d["head"]])
        d["p"] = (e / den).astype(BF16)
    for d in chains:
        d["o"] = _dot(d["p"], d["vb"])
    for d in chains:
        o_ref[:, d["head"] * hd:(d["head"] + 1) * hd] = d["o"].astype(BF16)


def _attn_ctx(qkv, sink, q_norm, k_norm, n_seq, t):
    hd = ATTN_HEAD_DIM
    nkv = ATTN_CTX_KVH
    gw = nkv * ATTN_GROUP * hd
    kcol = ATTN_HEADS // nkv
    vcol = kcol + ATTN_KV_HEADS // nkv
    n_tok = n_seq * t
    return pl.pallas_call(
        _attn_ctx_kernel,
        grid=(n_seq, ATTN_KV_HEADS // nkv),
        in_specs=[
            pl.BlockSpec(memory_space=pltpu.SMEM),
            pl.BlockSpec((t, gw), lambda b, h: (b, h)),
            pl.BlockSpec((t, nkv * hd), lambda b, h: (b, kcol + h)),
            pl.BlockSpec((t, nkv * hd), lambda b, h: (b, vcol + h)),
            pl.BlockSpec((1, hd), lambda b, h: (0, 0)),
            pl.BlockSpec((1, hd), lambda b, h: (0, 0)),
            pl.BlockSpec(memory_space=pl.ANY),
        ],
        out_specs=[
            pl.BlockSpec((t, gw), lambda b, h: (b, h)),
            pl.BlockSpec((t, nkv * hd), lambda b, h: (b, h)),
            pl.BlockSpec((t, nkv * hd), lambda b, h: (b, h)),
        ],
        input_output_aliases={6: 0},
        out_shape=[
            jax.ShapeDtypeStruct((N_TOK, ATTN_HEADS * hd), BF16),
            jax.ShapeDtypeStruct((n_tok, ATTN_KV_HEADS * hd), F32),
            jax.ShapeDtypeStruct((n_tok, ATTN_KV_HEADS * hd), F32),
        ],
        compiler_params=_cparams(2),
        name="attn_ctx",
    )(sink, qkv, qkv, qkv, q_norm[None, :], k_norm[None, :], _token_zeros())


def _rope(x, cos, sin_a, sin_b):
    quarter = ATTN_HEAD_DIM // 4
    return x * cos + pltpu.roll(x, ATTN_HEAD_DIM - quarter, 1) * sin_a + pltpu.roll(x, quarter, 1) * sin_b


def _rope_tables(t):
    hd = ATTN_HEAD_DIM
    quarter = hd // 4
    pos = jnp.arange(t)
    row = (pos // GRID_W).astype(F32)
    col = (pos % GRID_W).astype(F32)
    inv_freq = ROPE_THETA ** (-jnp.arange(quarter, dtype=F32) / quarter)
    lane = jnp.arange(hd)
    ang = jnp.where(lane[None, :] < hd // 2, row[:, None], col[:, None]) * inv_freq[lane % quarter][None, :]
    first = (lane % (hd // 2)) < quarter
    sin = jnp.sin(ang)
    return jnp.cos(ang), jnp.where(first[None, :], -sin, 0.0), jnp.where(first[None, :], 0.0, sin)


ATTN_LAT_NQB = 2


def _attn_lat_kernel(sink_ref, q_ref, k_ref, v_ref, ck_ref, cv_ref, qn_ref, kn_ref, cos_ref, sa_ref, sb_ref, _y_ctx,
                     o_ref, kb_scr, vb_scr):
    kvh = pl.program_id(1)
    hd = ATTN_HEAD_DIM
    t = q_ref.shape[0]
    span = ATTN_BLOCK + 2 * ATTN_WINDOW
    k = _rope(_rms_w(k_ref[...], kn_ref[...]), cos_ref[...], sa_ref[...], sb_ref[...])
    kb_scr[...] = k.astype(BF16)
    vb_scr[...] = v_ref[...].astype(BF16)
    ckb = ck_ref[0, 0].astype(BF16)
    cvb = cv_ref[0, 0].astype(BF16)
    q_iota = lax.broadcasted_iota(jnp.int32, (ATTN_BLOCK, span), 0)
    k_iota = lax.broadcasted_iota(jnp.int32, (ATTN_BLOCK, span), 1)

    def blocks(j, carry):
        ch = []
        for sub in range(ATTN_LAT_NQB):
            q0 = pl.multiple_of((j * ATTN_LAT_NQB + sub) * ATTN_BLOCK, ATTN_BLOCK)
            ws = pl.multiple_of(jnp.clip(q0 - ATTN_WINDOW, 0, t - span), ATTN_BLOCK)
            rows = pl.ds(q0, ATTN_BLOCK)
            valid = jnp.abs((k_iota + ws) - (q_iota + q0)) <= ATTN_WINDOW
            cos, sa, sb = cos_ref[rows, :], sa_ref[rows, :], sb_ref[rows, :]
            for g in range(ATTN_GROUP):
                qb = _rope(_rms_w(q_ref[rows, g * hd:(g + 1) * hd], qn_ref[...]), cos, sa, sb).astype(BF16)
                ch.append(dict(g=g, rows=rows, qb=qb, valid=valid,
                               kw=kb_scr[pl.ds(ws, span), :], vw=vb_scr[pl.ds(ws, span), :]))
        for d in ch:
            d["s_win"] = jnp.where(d["valid"], _dot_nt(d["qb"], d["kw"]) * hd ** -0.5, NEG_INF)
            d["s_ctx"] = _dot_nt(d["qb"], ckb) * hd ** -0.5
        for d in ch:
            (e_win, e_ctx), den = _softmax_parts([d["s_win"], d["s_ctx"]], sink_ref[kvh * ATTN_GROUP + d["g"]])
            d["p"] = ((e_win / den).astype(BF16), (e_ctx / den).astype(BF16))
        for d in ch:
            d["o"] = _dot(d["p"][0], d["vw"]) + _dot(d["p"][1], cvb)
        for d in ch:
            o_ref[d["rows"], d["g"] * hd:(d["g"] + 1) * hd] = d["o"].astype(BF16)
        return carry

    lax.fori_loop(0, t // (ATTN_BLOCK * ATTN_LAT_NQB), blocks, 0)


def _attn_lat(qkv, cache_k, cache_v, slot, sink, q_norm, k_norm, y_ctx, n_seq, t, row_off):
    hd = ATTN_HEAD_DIM
    gw = ATTN_GROUP * hd
    kcol = ATTN_HEADS
    vcol = kcol + ATTN_KV_HEADS
    n_past = cache_k.shape[2]
    ck = cache_k.reshape(cache_k.shape[0], cache_k.shape[1], n_past, ATTN_KV_HEADS * hd)
    cv = cache_v.reshape(ck.shape)
    cos, sin_a, sin_b = _rope_tables(t)
    tab = pl.BlockSpec((t, hd), lambda b, h: (0, 0))
    return pl.pallas_call(
        _attn_lat_kernel,
        grid=(n_seq, ATTN_KV_HEADS),
        in_specs=[
            pl.BlockSpec(memory_space=pltpu.SMEM),
            pl.BlockSpec((t, gw), lambda b, h: (b + row_off, h)),
            pl.BlockSpec((t, hd), lambda b, h: (b + row_off, kcol + h)),
            pl.BlockSpec((t, hd), lambda b, h: (b + row_off, vcol + h)),
            pl.BlockSpec((1, 1, n_past, hd), lambda b, h: (b, slot, 0, h)),
            pl.BlockSpec((1, 1, n_past, hd), lambda b, h: (b, slot, 0, h)),
            pl.BlockSpec((1, hd), lambda b, h: (0, 0)),
            pl.BlockSpec((1, hd), lambda b, h: (0, 0)),
            tab, tab, tab,
            pl.BlockSpec(memory_space=pl.ANY),
        ],
        out_specs=pl.BlockSpec((t, gw), lambda b, h: (b + row_off, h)),
        out_shape=jax.ShapeDtypeStruct(y_ctx.shape, BF16),
        input_output_aliases={11: 0},
        scratch_shapes=[pltpu.VMEM((t, hd), BF16), pltpu.VMEM((t, hd), BF16)],
        compiler_params=_cparams(2),
        name="attn_lat",
    )(sink, qkv, qkv, qkv, ck, cv, q_norm[None, :], k_norm[None, :], cos, sin_a, sin_b, y_ctx)


def _cumsum_rows(x, reverse):
    n = x.shape[0]
    row = lax.broadcasted_iota(jnp.int32, x.shape, 0)
    sh = 1
    while sh < n:
        if reverse:
            x = x + jnp.where(row < n - sh, pltpu.roll(x, n - sh, 0), 0.0)
        else:
            x = x + jnp.where(row >= sh, pltpu.roll(x, sh, 0), 0.0)
        sh *= 2
    return x


def _log_sigmoid(x):
    return jnp.minimum(x, 0.0) - jnp.log(1.0 + jnp.exp(-jnp.abs(x)))


def _mlstm_kernel(has_init, emit_state, hp, *refs):
    refs = list(refs)
    q_ref, k_ref, v_ref, og_ref, gt_ref, bg_ref, on_ref = refs[:7]
    refs = refs[7:]
    if has_init:
        c0_ref, n0_ref, m0_ref = refs[:3]
        refs = refs[3:]
    _y_prev, y_ref = refs[:2]
    refs = refs[2:]
    if emit_state:
        co_ref, no_ref, mo_ref = refs[:3]
        refs = refs[3:]
    hf_scr, hb_scr = refs

    L = MLSTM_CHUNK
    t = q_ref.shape[0]
    nc = t // L
    dk, dv = MLSTM_DK, MLSTM_DV
    row = lax.broadcasted_iota(jnp.int32, (L, L), 0)
    col = lax.broadcasted_iota(jnp.int32, (L, L), 1)
    lane = lax.broadcasted_iota(jnp.int32, (L, LANES), 1)

    def chunk_step(chains):
        st = []
        for z, hh, c, C, n, m in chains:
            rows = pl.ds(pl.multiple_of(c * L, L), L)
            qc = q_ref[rows, hh * dk:(hh + 1) * dk] * dk ** -0.5
            kc = k_ref[rows, hh * dk:(hh + 1) * dk]
            vc = v_ref[rows, hh * dv:(hh + 1) * dv]
            G = gt_ref[rows, hh * LANES:(hh + 1) * LANES] + bg_ref[hh]
            CUM = _cumsum_rows(_log_sigmoid(G), reverse=(z == 1))
            ci, cf = 2 * z, 2 * z + 1
            ZT = jnp.where(lane == cf, CUM, G).T
            i_row, cum_row = ZT[ci:ci + 1, :L], ZT[cf:cf + 1, :L]
            i_col, cum_col = G[:, ci:ci + 1], CUM[:, cf:cf + 1]
            mask = (col <= row) if z == 0 else (col >= row)
            dmat = jnp.where(mask, cum_col - cum_row + i_row, NEG_INF)
            inter = cum_col + m
            m_c = jnp.maximum(inter, jnp.max(dmat, axis=-1, keepdims=True))
            bl = cum_row[:, L - 1:L] if z == 0 else cum_row[:, 0:1]
            m_new = jnp.maximum(bl + m, jnp.max(bl - cum_row + i_row, axis=-1, keepdims=True))
            kw = kc * jnp.exp(bl - cum_col + i_col - m_new)
            st.append(dict(
                z=z, hh=hh, rows=rows, C=C, n=n, qc=qc, m_c=m_c, m_new=m_new,
                qb=qc.astype(BF16), kb=kc.astype(BF16), vb=vc.astype(BF16), Cb=C.astype(BF16),
                w=jnp.exp(dmat - m_c), a=jnp.exp(inter - m_c), decay=jnp.exp(bl + m - m_new),
                kw=kw, kwb=kw.astype(BF16)))
        for d in st:
            d["qk"] = _dot_nt(d["qb"], d["kb"])
            d["qC"] = _dot(d["qb"], d["Cb"])
            d["kv"] = _dot_tn(d["kwb"], d["vb"])
        for d in st:
            d["s"] = d["qk"] * d["w"]
        for d in st:
            d["sv"] = _dot(d["s"].astype(BF16), d["vb"])
        out = []
        for d in st:
            num = d["sv"] + d["a"] * d["qC"]
            den = jnp.sum(d["s"], axis=-1, keepdims=True) + d["a"] * jnp.sum(d["qc"] * d["n"], axis=-1, keepdims=True)
            scr = hf_scr if d["z"] == 0 else hb_scr
            scr[d["rows"], d["hh"] * dv:(d["hh"] + 1) * dv] = num / jnp.maximum(jnp.abs(den), jnp.exp(-d["m_c"]))
            out += [d["decay"] * d["C"] + d["kv"],
                    d["decay"] * d["n"] + jnp.sum(d["kw"], axis=0, keepdims=True),
                    d["m_new"]]
        return out

    def body(ci, carry):
        chains = []
        for hh in range(hp):
            Cf, nf, mf, Cb, nb, mb = carry[6 * hh:6 * hh + 6]
            chains.append((0, hh, ci, Cf, nf, mf))
            chains.append((1, hh, nc - 1 - ci, Cb, nb, mb))
        return tuple(chunk_step(chains))

    init = []
    for hh in range(hp):
        for z in range(2):
            if has_init:
                init += [c0_ref[0, z, hh], n0_ref[0, z, hh], m0_ref[0, z, hh][:, 0:1]]
            else:
                init += [jnp.zeros((dk, dv), F32), jnp.zeros((1, dk), F32), jnp.zeros((1, 1), F32)]
    final = lax.fori_loop(0, nc, body, tuple(init))

    for hh in range(hp):
        cols = slice(hh * dv, (hh + 1) * dv)
        hs = hf_scr[:, cols] + hb_scr[:, cols]
        y_ref[:, cols] = (_rms_w(hs, on_ref[:, cols]) * _sigmoid(og_ref[:, cols])).astype(BF16)
        if emit_state:
            for z in range(2):
                C, n, m = final[6 * hh + 3 * z:6 * hh + 3 * z + 3]
                co_ref[0, z, hh] = C
                no_ref[0, z, hh] = n
                mo_ref[0, z, hh] = jnp.broadcast_to(m, (1, LANES))


def _mlstm(proj, gates, b_gate_h, out_norm, state0, y_prev, n_seq, t, row_off, emit_state, hp):
    h_, dk, dv = MLSTM_HEADS, MLSTM_DK, MLSTM_DV
    nhb = h_ // hp
    has_init = state0 is not None
    in_specs = [
        pl.BlockSpec((t, hp * dk), lambda b, h: (b + row_off, h)),
        pl.BlockSpec((t, hp * dk), lambda b, h: (b + row_off, nhb + h)),
        pl.BlockSpec((t, hp * dv), lambda b, h: (b + row_off, nhb + h)),
        pl.BlockSpec((t, hp * dv), lambda b, h: (b + row_off, 2 * nhb + h)),
        pl.BlockSpec((t, hp * LANES), lambda b, h: (b + row_off, h)),
        pl.BlockSpec((hp, 1, LANES), lambda b, h: (h, 0, 0)),
        pl.BlockSpec((1, hp * dv), lambda b, h: (0, h)),
    ]
    args = [proj, proj, proj, proj, gates, b_gate_h[:, None, :], out_norm[None, :]]
    if has_init:
        c0, n0, m0 = state0
        in_specs += [
            pl.BlockSpec((1, 2, hp, dk, dv), lambda b, h: (b, 0, h, 0, 0)),
            pl.BlockSpec((1, 2, hp, 1, dk), lambda b, h: (b, 0, h, 0, 0)),
            pl.BlockSpec((1, 2, hp, 1, LANES), lambda b, h: (b, 0, h, 0, 0)),
        ]
        args += [c0, n0.reshape(n_seq, 2, h_, 1, dk),
                 jnp.broadcast_to(m0[..., None, None], (n_seq, 2, h_, 1, LANES))]
    in_specs.append(pl.BlockSpec(memory_space=pl.ANY))
    args.append(y_prev)
    aliases = {len(args) - 1: 0}
    out_specs = [pl.BlockSpec((t, hp * dv), lambda b, h: (b + row_off, h))]
    out_shape = [jax.ShapeDtypeStruct((N_TOK, D_MODEL), BF16)]
    if emit_state:
        out_specs += [
            pl.BlockSpec((1, 2, hp, dk, dv), lambda b, h: (b, 0, h, 0, 0)),
            pl.BlockSpec((1, 2, hp, 1, dk), lambda b, h: (b, 0, h, 0, 0)),
            pl.BlockSpec((1, 2, hp, 1, LANES), lambda b, h: (b, 0, h, 0, 0)),
        ]
        out_shape += [
            jax.ShapeDtypeStruct((n_seq, 2, h_, dk, dv), F32),
            jax.ShapeDtypeStruct((n_seq, 2, h_, 1, dk), F32),
            jax.ShapeDtypeStruct((n_seq, 2, h_, 1, LANES), F32),
        ]
    return pl.pallas_call(
        functools.partial(_mlstm_kernel, has_init, emit_state, hp),
        grid=(n_seq, nhb),
        in_specs=in_specs,
        out_specs=out_specs,
        out_shape=out_shape,
        input_output_aliases=aliases,
        scratch_shapes=[pltpu.VMEM((t, hp * dv), F32), pltpu.VMEM((t, hp * dv), F32)],
        compiler_params=_cparams(2),
        name="mlstm",
    )(*args)


def _rwkv_mix_kernel(x_ref, xp_ref, xn_ref, shift_ref, scale_ref, g_ref, mu_ref, o_ref, h_scr):
    i = pl.program_id(0)
    tm = x_ref.shape[0]
    g, sh, sc = g_ref[...], shift_ref[0], scale_ref[0]
    n_ctx_tiles = N_CTX_TOK // tm
    tiles_per_lat = DEC_SEQ // tm
    tile_in_seq = jnp.where(i < n_ctx_tiles, i % (SEQ // tm), (i - n_ctx_tiles) % tiles_per_lat)
    tiles_in_seq = jnp.where(i < n_ctx_tiles, SEQ // tm, tiles_per_lat)
    has_prev = (tile_in_seq > 0).astype(F32)
    has_next = (tile_in_seq < tiles_in_seq - 1).astype(F32)
    h_scr[0:8, :] = _modulated(xp_ref[...], g, sh, sc) * has_prev
    h_scr[8 + tm:16 + tm, :] = _modulated(xn_ref[...], g, sh, sc) * has_next

    def fill(c, carry):
        r0 = pl.multiple_of(c * ROW_CHUNK, ROW_CHUNK)
        h_scr[pl.ds(8 + r0, ROW_CHUNK), :] = _modulated(x_ref[pl.ds(r0, ROW_CHUNK), :], g, sh, sc)
        return carry

    lax.fori_loop(0, tm // ROW_CHUNK, fill, 0, unroll=4)

    def mix(c, carry):
        r0 = pl.multiple_of(c * ROW_CHUNK, ROW_CHUNK)
        ext = h_scr[pl.ds(r0, ROW_CHUNK + 16), :]
        h = ext[8:8 + ROW_CHUNK]
        prev = pltpu.roll(ext, 1, 0)[8:8 + ROW_CHUNK]
        nxt = pltpu.roll(ext, ROW_CHUNK + 15, 0)[8:8 + ROW_CHUNK]
        cs = 0.5 * (prev + nxt) - h
        for p in range(o_ref.shape[0]):
            o_ref[p, pl.ds(r0, ROW_CHUNK), :] = (h + cs * mu_ref[p:p + 1, :]).astype(BF16)
        return carry

    lax.fori_loop(0, tm // ROW_CHUNK, mix, 0, unroll=2)


def _rwkv_mix(x, mods3, g_row, mu):
    tm = 256
    nb8 = tm // 8
    last = N_TOK // 8 - 1
    return pl.pallas_call(
        _rwkv_mix_kernel,
        grid=(N_TOK // tm,),
        in_specs=[
            pl.BlockSpec((tm, D_MODEL), lambda i: (i, 0)),
            pl.BlockSpec((8, D_MODEL), lambda i: (jnp.maximum(i * nb8 - 1, 0), 0)),
            pl.BlockSpec((8, D_MODEL), lambda i: (jnp.minimum((i + 1) * nb8, last), 0)),
            pl.BlockSpec((1, 1, D_MODEL), lambda i: (_mod_row(i, tm), 0, 3)),
            pl.BlockSpec((1, 1, D_MODEL), lambda i: (_mod_row(i, tm), 0, 4)),
            pl.BlockSpec((1, D_MODEL), lambda i: (0, 0)),
            pl.BlockSpec((6, D_MODEL), lambda i: (0, 0)),
        ],
        out_specs=pl.BlockSpec((6, tm, D_MODEL), lambda i: (0, i, 0)),
        out_shape=jax.ShapeDtypeStruct((6, N_TOK, D_MODEL), BF16),
        scratch_shapes=[pltpu.VMEM((tm + 16, D_MODEL), F32)],
        compiler_params=_cparams(1),
        name="rwkv_mix",
    )(x, x, x, mods3, mods3, g_row, mu)


def _lora_kernel(act, post, x_ref, a_ref, b_ref, bias_ref, o_ref):
    u = _dot(x_ref[0], a_ref[0])
    if act == "tanh":
        u = jnp.tanh(u)
    elif act == "sigmoid":
        u = _sigmoid(u)
    y = _dot(u.astype(BF16), b_ref[0]) + bias_ref[0]
    if post == "log_decay":
        y = -(_sigmoid(y) * jnp.exp(jnp.float32(-0.5)))
    elif post == "sigmoid":
        y = _sigmoid(y)
    o_ref[0] = y


def _lora(xs, p, wa, wb, bias, act, post, name):
    tm = 512
    nz, _, r = wa.shape
    return pl.pallas_call(
        functools.partial(_lora_kernel, act, post),
        grid=(nz, N_TOK // tm),
        in_specs=[
            pl.BlockSpec((1, tm, D_MODEL), lambda z, i: (p, i, 0)),
            pl.BlockSpec((1, D_MODEL, r), lambda z, i: (z, 0, 0)),
            pl.BlockSpec((1, r, D_MODEL), lambda z, i: (z, 0, 0)),
            pl.BlockSpec((1, 1, D_MODEL), lambda z, i: (z, 0, 0)),
        ],
        out_specs=pl.BlockSpec((1, tm, D_MODEL), lambda z, i: (z, i, 0)),
        out_shape=jax.ShapeDtypeStruct((nz, N_TOK, D_MODEL), F32),
        compiler_params=_cparams(2),
        name=name,
    )(xs, wa, wb, bias)


def _rkv_kernel(x_ref, w_ref, o_ref):
    x = x_ref[0]
    for c in range(o_ref.shape[2] // MM_COLS):
        cols = slice(c * MM_COLS, (c + 1) * MM_COLS)
        o_ref[0, :, cols] = _dot(x, w_ref[0, :, cols])


def _rkv(xs, w_rkv):
    tm = 512
    return pl.pallas_call(
        _rkv_kernel,
        grid=(3, N_TOK // tm),
        in_specs=[pl.BlockSpec((1, tm, D_MODEL), lambda p, i: (p, i, 0)),
                  pl.BlockSpec((1, D_MODEL, D_MODEL), lambda p, i: (p, 0, 0))],
        out_specs=pl.BlockSpec((1, tm, D_MODEL), lambda p, i: (p, i, 0)),
        out_shape=jax.ShapeDtypeStruct((3, N_TOK, D_MODEL), F32),
        compiler_params=_cparams(2),
        name="rwkv_rkv",
    )(xs, w_rkv)


def _head_sum(x, first_head):
    s0 = jnp.sum(jnp.where(first_head, x, 0.0), axis=-1, keepdims=True)
    s1 = jnp.sum(jnp.where(first_head, 0.0, x), axis=-1, keepdims=True)
    return jnp.where(first_head, s0, s1)


def _rwkv_kernel(has_init, emit_state, npp, *refs):
    refs = list(refs)
    r_ref, k_ref, v_ref, lw_ref, a_ref, g_ref, kk_w_ref, ka_ref, rk_ref, lng_ref, lnb_ref = refs[:11]
    refs = refs[11:]
    if has_init:
        s0_ref = refs[0]
        refs = refs[1:]
    _y_prev, y_ref = refs[:2]
    refs = refs[2:]
    if emit_state:
        so_ref = refs[0]
        refs = refs[1:]
    kk_scr, yf_scr, yb_scr = refs

    L = RWKV_CHUNK
    N = RWKV_HEAD
    t = r_ref.shape[1]
    nc = t // L
    row = lax.broadcasted_iota(jnp.int32, (L, L), 0)
    col = lax.broadcasted_iota(jnp.int32, (L, L), 1)
    first_head = lax.broadcasted_iota(jnp.int32, (L, LANES), 1) < N
    sq_row = lax.broadcasted_iota(jnp.int32, (LANES, LANES), 0)
    sq_col = lax.broadcasted_iota(jnp.int32, (LANES, LANES), 1)
    same_head = (sq_row < N) == (sq_col < N)

    first_head_t = lax.broadcasted_iota(jnp.int32, (t, LANES), 1) < N

    for pp in range(npp):
        lanes = slice(pp * LANES, (pp + 1) * LANES)
        kkx = k_ref[0, :, lanes] * kk_w_ref[:, lanes]
        kk_scr[:, lanes] = kkx / jnp.maximum(jnp.sqrt(_head_sum(kkx * kkx, first_head_t)), 1e-12)

    def chunk_step(chains):
        heads = (first_head, jnp.logical_not(first_head))
        pick = lambda x0, x1: jnp.where(first_head, x0, x1)
        st = []
        for z, pp, c, S in chains:
            lanes = slice(pp * LANES, (pp + 1) * LANES)
            sl = pl.ds(pl.multiple_of(c * L, L), L)
            r_, k_, v_ = r_ref[0, sl, lanes], k_ref[0, sl, lanes], v_ref[0, sl, lanes]
            lw, a_ = lw_ref[z, sl, lanes], a_ref[z, sl, lanes]
            kk = kk_scr[sl, lanes]
            kd = k_ * (1.0 + (a_ - 1.0) * ka_ref[:, lanes])
            b_ = kk * a_
            cum = _cumsum_rows(lw, reverse=(z == 1))
            cum_end = cum[L - 1:L, :] if z == 0 else cum[0:1, :]
            kp = (kk * jnp.exp(cum - lw)).astype(BF16)
            rp = (r_ * jnp.exp(cum)).astype(BF16)
            inv = jnp.exp(-cum)
            to_end = jnp.exp(cum_end - cum)
            zero = jnp.zeros_like(kp)
            st.append(dict(
                z=z, pp=pp, sl=sl, lanes=lanes, S=S, kp=kp, rp=rp, cum_end=cum_end,
                kt=(kd * inv).astype(BF16), bt=(b_ * inv).astype(BF16),
                kb_end=jnp.concatenate([kd * to_end, -(b_ * to_end)], axis=0).astype(BF16),
                vb=v_.astype(BF16), Sb=S.astype(BF16),
                probe=[jnp.concatenate([jnp.where(hm, kp, zero), jnp.where(hm, rp, zero)], axis=0) for hm in heads],
                strict=(col < row) if z == 0 else (col > row),
                incl=(col <= row) if z == 0 else (col >= row)))
        for d in st:
            d["gk"] = [_dot_nt(pr, d["kt"]) for pr in d["probe"]]
            d["gb"] = [_dot_nt(pr, d["bt"]) for pr in d["probe"]]
        for d in st:
            d["g1"] = [jnp.where(d["strict"], g[:L], 0.0).astype(BF16) for g in d["gk"]]
            d["n"] = [jnp.where(d["strict"], g[:L], 0.0).astype(BF16) for g in d["gb"]]
            d["g3"] = [jnp.where(d["incl"], g[L:], 0.0).astype(BF16) for g in d["gk"]]
            d["g4"] = [jnp.where(d["incl"], g[L:], 0.0).astype(BF16) for g in d["gb"]]
        for d in st:
            d["g1v"] = [_dot(g, d["vb"]) for g in d["g1"]]
            d["kps"] = _dot_nt(d["kp"], d["Sb"])
            d["rps"] = _dot_nt(d["rp"], d["Sb"])
        for d in st:
            d["u"] = d["kps"] + pick(*d["g1v"])
        sign = -1.0
        p = 1
        while p < L:
            for d in st:
                ub = d["u"].astype(BF16)
                d["nu"] = [_dot(m, ub) for m in d["n"]]
                if 2 * p < L:
                    d["n"] = [_dot(m, m).astype(BF16) for m in d["n"]]
            for d in st:
                d["u"] = d["u"] + sign * pick(*d["nu"])
            sign = 1.0
            p *= 2
        for d in st:
            ub = d["u"].astype(BF16)
            d["y3"] = [_dot(g, d["vb"]) for g in d["g3"]]
            d["y4"] = [_dot(g, ub) for g in d["g4"]]
            d["upd"] = _dot_tn(jnp.concatenate([d["vb"], ub], axis=0), d["kb_end"])
        out = []
        for d in st:
            y = d["rps"] + pick(d["y3"][0] - d["y4"][0], d["y3"][1] - d["y4"][1])
            (yf_scr if d["z"] == 0 else yb_scr)[d["sl"], d["lanes"]] = y
            out.append(d["S"] * jnp.exp(d["cum_end"]) + jnp.where(same_head, d["upd"], 0.0))
        return out

    def body(ci, carry):
        chains = []
        for pp in range(npp):
            chains.append((0, pp, ci, carry[2 * pp]))
            chains.append((1, pp, nc - 1 - ci, carry[2 * pp + 1]))
        return tuple(chunk_step(chains))

    init = []
    for pp in range(npp):
        for z in range(2):
            if has_init:
                s = s0_ref[0, z, pp]
                init.append(jnp.where(sq_row < N, s, pltpu.roll(s, N, 1)))
            else:
                init.append(jnp.zeros((LANES, LANES), F32))
    final = lax.fori_loop(0, nc, body, tuple(init))

    for pp in range(npp):
        lanes = slice(pp * LANES, (pp + 1) * LANES)
        r_, k_, v_ = r_ref[0, :, lanes], k_ref[0, :, lanes], v_ref[0, :, lanes]
        y = yf_scr[:, lanes] + yb_scr[:, lanes]
        mean = _head_sum(y, first_head_t) * (1.0 / N)
        yc = y - mean
        var = _head_sum(yc * yc, first_head_t) * (1.0 / N)
        yn = yc * lax.rsqrt(var + RWKV_GN_EPS) * lng_ref[:, lanes] + lnb_ref[:, lanes]
        ka = ka_ref[:, lanes]
        kd_sum = k_ * (1.0 + (a_ref[0, :, lanes] - 1.0) * ka) + k_ * (1.0 + (a_ref[1, :, lanes] - 1.0) * ka)
        bonus = _head_sum(r_ * kd_sum * rk_ref[:, lanes], first_head_t) * v_
        y_ref[:, lanes] = ((yn + bonus) * g_ref[:, lanes]).astype(BF16)
        if emit_state:
            for z in range(2):
                S = final[2 * pp + z]
                so_ref[0, z, pp] = (S + pltpu.roll(S, N, 1))[:, :N]


def _rwkv(rkv, lw, a, g, k_k, k_a, r_k, ln_g, ln_b, s0, y_prev, n_seq, t, row_off, emit_state, npp):
    has_init = s0 is not None
    w = npp * LANES
    tok = lambda lead, first: pl.BlockSpec((lead, t, w), lambda b, p: (first, b + row_off, p))
    par = pl.BlockSpec((1, w), lambda b, p: (0, p))
    in_specs = [tok(1, 0), tok(1, 1), tok(1, 2), tok(2, 0), tok(2, 0),
                pl.BlockSpec((t, w), lambda b, p: (b + row_off, p)),
                par, par, par, par, par]
    args = [rkv, rkv, rkv, lw, a, g, k_k[None, :], k_a[None, :], r_k.reshape(1, D_MODEL), ln_g[None, :], ln_b[None, :]]
    if has_init:
        in_specs.append(pl.BlockSpec((1, 2, npp, LANES, LANES), lambda b, p: (b, 0, p, 0, 0)))
        args.append(s0)
    in_specs.append(pl.BlockSpec(memory_space=pl.ANY))
    args.append(y_prev)
    aliases = {len(args) - 1: 0}
    out_specs = [pl.BlockSpec((t, w), lambda b, p: (b + row_off, p))]
    out_shape = [jax.ShapeDtypeStruct((N_TOK, D_MODEL), BF16)]
    if emit_state:
        out_specs.append(pl.BlockSpec((1, 2, npp, LANES, RWKV_HEAD), lambda b, p: (b, 0, p, 0, 0)))
        out_shape.append(jax.ShapeDtypeStruct((n_seq, 2, RWKV_PAIRS, LANES, RWKV_HEAD), F32))
    return pl.pallas_call(
        functools.partial(_rwkv_kernel, has_init, emit_state, npp),
        grid=(n_seq, RWKV_PAIRS // npp),
        in_specs=in_specs,
        out_specs=out_specs,
        out_shape=out_shape,
        input_output_aliases=aliases,
        scratch_shapes=[pltpu.VMEM((t, w), F32)] * 3,
        compiler_params=_cparams(2),
        name="rwkv",
    )(*args)


def _attn_layer(x, mods3, g_row, cache_k, cache_v, slot, w_qkv, q_norm, k_norm, sink, w_o):
    qkv = _mm_mod(x, mods3, 3, g_row, w_qkv.astype(BF16), 512, "attn_qkv")
    y, k_ctx, v_ctx = _attn_ctx(qkv, sink, q_norm, k_norm, BATCH, SEQ)
    y = _attn_lat(qkv, cache_k, cache_v, slot, sink, q_norm, k_norm, y, DEC_BATCH, DEC_SEQ, N_CTX_TOK // DEC_SEQ)
    x = _mm_res(y, w_o.astype(BF16), x, mods3, 5, "attn_out")
    shape = (BATCH, SEQ, ATTN_KV_HEADS, ATTN_HEAD_DIM)
    return x, k_ctx.reshape(shape), v_ctx.reshape(shape)


def _mlstm_layer(x, mods3, g_row, state_lat, w_in, w_gate, b_gate, out_norm, w_o):
    h_ = MLSTM_HEADS
    proj = _mm_mod(x, mods3, 3, g_row, w_in.astype(BF16), 256, "mlstm_in")
    wg = jnp.transpose(w_gate.reshape(D_MODEL, 4, h_), (0, 2, 1))
    wg = jnp.pad(wg, ((0, 0), (0, 0), (0, LANES - 4))).reshape(D_MODEL, h_ * LANES).astype(BF16)
    bg = jnp.pad(b_gate.reshape(4, h_).T, ((0, 0), (0, LANES - 4)))
    gates = _mm_mod(x, mods3, 3, g_row, wg, 512, "mlstm_gates")
    y, C, n, m = _mlstm(proj, gates, bg, out_norm, None, _token_zeros(), BATCH, SEQ, 0, True, MLSTM_HEADS)
    (y,) = _mlstm(proj, gates, bg, out_norm, state_lat, y, DEC_BATCH, DEC_SEQ, N_CTX_TOK // DEC_SEQ, False, MLSTM_HEADS // 2)
    x = _mm_res(y, w_o.astype(BF16), x, mods3, 5, "mlstm_out")
    return x, C, n[:, :, :, 0, :], m[:, :, :, 0, 0]


def _rwkv_layer(x, mods3, g_row, s0_lat, mu, w_rkv, w0, wA, wB, a0, aA, aB, gA, gB, k_k, k_a, r_k, ln_g, ln_b, w_o):
    xs = _rwkv_mix(x, mods3, g_row, mu)
    rkv = _rkv(xs, w_rkv.astype(BF16))
    pad_r = lambda w, axis: jnp.pad(w, [(0, 0) if ax != axis else (0, LANES - w.shape[axis]) for ax in range(w.ndim)])
    lw = _lora(xs, 3, pad_r(wA, 2).astype(BF16), pad_r(wB, 1).astype(BF16), w0[:, None, :], "tanh", "log_decay", "rwkv_w")
    a = _lora(xs, 4, pad_r(aA, 2).astype(BF16), pad_r(aB, 1).astype(BF16), a0[:, None, :], "none", "sigmoid", "rwkv_a")
    g = _lora(xs, 5, gA[None].astype(BF16), gB[None].astype(BF16), jnp.zeros((1, 1, D_MODEL), F32), "sigmoid", "none", "rwkv_g")[0]
    s0 = jnp.pad(s0_lat.reshape(DEC_BATCH, 2, RWKV_PAIRS, LANES, RWKV_HEAD), ((0, 0),) * 4 + ((0, LANES - RWKV_HEAD),))
    y, S = _rwkv(rkv, lw, a, g, k_k, k_a, r_k, ln_g, ln_b, None, _token_zeros(), BATCH, SEQ, 0, True, RWKV_PAIRS // 2)
    (y,) = _rwkv(rkv, lw, a, g, k_k, k_a, r_k, ln_g, ln_b, s0, y, DEC_BATCH, DEC_SEQ, N_CTX_TOK // DEC_SEQ, False, RWKV_PAIRS // 4)
    x = _mm_res(y, w_o.astype(BF16), x, mods3, 5, "rwkv_out")
    return x, S.reshape(BATCH, 2, RWKV_HEADS, RWKV_HEAD, RWKV_HEAD)


def kernel(x_prompt, x_sample, cache_k, cache_v, state_mlstm_C, state_mlstm_n, state_mlstm_m, state_rwkv, c, c_ctx, mod_w, mod_b, norm_g, ffn_w_in, ffn_w_out, attn_w_qkv, attn_q_norm, attn_k_norm, attn_sink, attn_w_o, mlstm_w_in, mlstm_w_gate, mlstm_b_gate, mlstm_out_norm, mlstm_w_o, rwkv_mu, rwkv_w_rkv, rwkv_w0, rwkv_wA, rwkv_wB, rwkv_a0, rwkv_aA, rwkv_aB, rwkv_gA, rwkv_gB, rwkv_k_k, rwkv_k_a, rwkv_r_k, rwkv_ln_g, rwkv_ln_b, rwkv_w_o):
    x = jnp.concatenate([x_prompt.reshape(N_CTX_TOK, D_MODEL), x_sample.reshape(N_LAT_TOK, D_MODEL)], axis=0)
    cond8 = jnp.concatenate([c_ctx[None, :], c, jnp.zeros((MOD_ROWS - 1 - DEC_BATCH, D_MODEL), F32)], axis=0)
    mods = _adaln_all(cond8, mod_w, mod_b)
    ffn_order = [(i, j) for i in range(DEPTH) for j in range(2)]
    w_in_b = ffn_w_in[0, 0].astype(BF16)
    w_out_b = ffn_w_out[0, 0].astype(BF16)

    def ffn(x, mods3, slot, g_row, k, w_in_b, w_out_b):
        nxt = (ffn_w_in, ffn_w_out) + ffn_order[k + 1] if k + 1 < len(ffn_order) else None
        res = _ffn(x, mods3, slot, g_row, w_in_b, w_out_b, nxt)
        return res if nxt is not None else (res[0], None, None)

    new_k, new_v, new_C, new_n, new_m, new_S = [], [], [], [], [], []
    for i in range(DEPTH):
        kind, slot = i % N_MIXERS, i // N_MIXERS
        mods3 = mods[i].reshape(MOD_ROWS, 1, N_MOD * D_MODEL)
        x, w_in_b, w_out_b = ffn(x, mods3, 0, norm_g[i, 0][None, :], 2 * i, w_in_b, w_out_b)
        g_row = norm_g[i, 1][None, :]
        if kind == 0:
            x, k_ctx, v_ctx = _attn_layer(x, mods3, g_row, cache_k, cache_v, slot, attn_w_qkv[slot], attn_q_norm[slot],
                                          attn_k_norm[slot], attn_sink[slot], attn_w_o[slot])
            new_k.append(k_ctx)
            new_v.append(v_ctx)
        elif kind == 1:
            state_lat = (state_mlstm_C[:, slot], state_mlstm_n[:, slot], state_mlstm_m[:, slot])
            x, C, n, m = _mlstm_layer(x, mods3, g_row, state_lat, mlstm_w_in[slot], mlstm_w_gate[slot], mlstm_b_gate[slot],
                                      mlstm_out_norm[slot], mlstm_w_o[slot])
            new_C.append(C)
            new_n.append(n)
            new_m.append(m)
        else:
            x, S = _rwkv_layer(x, mods3, g_row, state_rwkv[:, slot], rwkv_mu[slot], rwkv_w_rkv[slot], rwkv_w0[slot],
                               rwkv_wA[slot], rwkv_wB[slot], rwkv_a0[slot], rwkv_aA[slot], rwkv_aB[slot], rwkv_gA[slot],
                               rwkv_gB[slot], rwkv_k_k[slot], rwkv_k_a[slot], rwkv_r_k[slot], rwkv_ln_g[slot],
                               rwkv_ln_b[slot], rwkv_w_o[slot])
            new_S.append(S)
        x, w_in_b, w_out_b = ffn(x, mods3, 6, norm_g[i, 2][None, :], 2 * i + 1, w_in_b, w_out_b)

    y_prompt = x[:N_CTX_TOK].reshape(BATCH, SEQ, D_MODEL)
    y_sample = x[N_CTX_TOK:].reshape(DEC_BATCH, DEC_SEQ, D_MODEL)
    return (y_prompt, y_sample, jnp.stack(new_k, axis=1), jnp.stack(new_v, axis=1),
            jnp.stack(new_C, axis=1), jnp.stack(new_n, axis=1), jnp.stack(new_m, axis=1), jnp.stack(new_S, axis=1))
```

```python
import functools

import jax
import jax.numpy as jnp
from jax import lax
from jax.experimental import pallas as pl
from jax.experimental.pallas import tpu as pltpu

D_MODEL = 2048
BATCH = 32
SEQ = 256
DEPTH = 4
DEC_BATCH = 4
DEC_SEQ = 1024
GRID_W = 64
N_MIXERS = 3
N_MOD = 9
D_FF = 5632
EPS = 1e-6

ATTN_HEADS = 16
ATTN_KV_HEADS = 4
ATTN_GROUP = ATTN_HEADS // ATTN_KV_HEADS
ATTN_HEAD_DIM = D_MODEL // ATTN_HEADS
ATTN_WINDOW = 128
ATTN_BLOCK = 128
ROPE_THETA = 10000.0

MLSTM_HEADS = 8
MLSTM_DV = D_MODEL // MLSTM_HEADS
MLSTM_DK = MLSTM_DV // 2
MLSTM_CHUNK = 256

RWKV_HEAD = 64
RWKV_HEADS = D_MODEL // RWKV_HEAD
RWKV_PAIRS = RWKV_HEADS // 2
RWKV_CHUNK = 64
RWKV_GN_EPS = 64e-5

N_CTX_TOK = BATCH * SEQ
N_LAT_TOK = DEC_BATCH * DEC_SEQ
N_TOK = N_CTX_TOK + N_LAT_TOK
MOD_ROWS = 8
LANES = 128

VMEM_LIMIT = 60 * 1024 * 1024
BF16 = jnp.bfloat16
F32 = jnp.float32
NEG_INF = float("-inf")
MM_COLS = 512


def _cparams(n_axes):
    return pltpu.CompilerParams(dimension_semantics=("arbitrary",) * n_axes, vmem_limit_bytes=VMEM_LIMIT)


def _mod_row(i, tm):
    n_ctx_tiles = N_CTX_TOK // tm
    tiles_per_lat = DEC_SEQ // tm
    return jnp.where(i < n_ctx_tiles, 0, 1 + (i - n_ctx_tiles) // tiles_per_lat)


def _mod_spec(slot, tm):
    return pl.BlockSpec((1, 1, D_MODEL), lambda i, j, s=slot: (_mod_row(i, tm), 0, s))


def _dot(a, b):
    return jnp.dot(a, b, preferred_element_type=F32)


def _dot_nt(a, b):
    return lax.dot_general(a, b, (((1,), (1,)), ((), ())), preferred_element_type=F32)


def _dot_tn(a, b):
    return lax.dot_general(a, b, (((0,), (0,)), ((), ())), preferred_element_type=F32)


def _sigmoid(x):
    return jax.nn.sigmoid(x)


def _modulated(x, g, shift, scale):
    ms = jnp.mean(x * x, axis=-1, keepdims=True)
    y = x * lax.rsqrt(ms + EPS) * g
    return y * (1.0 + scale) + shift


def _adaln_kernel(cond_ref, w_ref, b_ref, o_ref):
    c = cond_ref[...]
    a = (c * _sigmoid(c)).astype(BF16)
    o_ref[0] = _dot(a, w_ref[0].astype(BF16)) + b_ref[0]


def _adaln_all(cond8, mod_w, mod_b):
    tn = 1024
    n_out = N_MOD * D_MODEL
    return pl.pallas_call(
        _adaln_kernel,
        grid=(DEPTH, n_out // tn),
        in_specs=[
            pl.BlockSpec((MOD_ROWS, D_MODEL), lambda l, j: (0, 0)),
            pl.BlockSpec((1, D_MODEL, tn), lambda l, j: (l, 0, j)),
            pl.BlockSpec((1, 1, tn), lambda l, j: (l, 0, j)),
        ],
        out_specs=pl.BlockSpec((1, MOD_ROWS, tn), lambda l, j: (l, 0, j)),
        out_shape=jax.ShapeDtypeStruct((DEPTH, MOD_ROWS, n_out), F32),
        compiler_params=_cparams(2),
        name="adaln",
    )(cond8, mod_w, mod_b.reshape(DEPTH, 1, n_out))


FFN_TM = 1024
FFN_TF = 512
ROW_CHUNK = 32


def _cast_rows(total, n_steps):
    r = 16
    while total % r or total // r > n_steps:
        r *= 2
    return r


def _ffn_kernel(has_next, *refs):
    x_ref, shift_ref, scale_ref, gate_ref, g_ref, wg_ref, wu_ref, wo_ref = refs[:8]
    refs = refs[8:]
    if has_next:
        nin_ref, nout_ref, o_ref, cin_ref, cout_ref, h_scr = refs
    else:
        o_ref, h_scr = refs
    f = pl.program_id(1)

    @pl.when(f == 0)
    def _():
        def rows(c, carry):
            r = pl.ds(pl.multiple_of(c * ROW_CHUNK, ROW_CHUNK), ROW_CHUNK)
            h_scr[r, :] = _modulated(x_ref[r, :], g_ref[...], shift_ref[0], scale_ref[0]).astype(BF16)
            return carry
        lax.fori_loop(0, x_ref.shape[0] // ROW_CHUNK, rows, 0, unroll=8)

    h = h_scr[...]
    a = _dot(h, wg_ref[...])
    b = _dot(h, wu_ref[...])
    act = ((a * _sigmoid(a)) * b).astype(BF16)
    if has_next:
        cin_ref[...] = nin_ref[0, 0].astype(BF16)
        cout_ref[...] = nout_ref[0, 0].astype(BF16)
    for c in range(o_ref.shape[1] // MM_COLS):
        cols = slice(c * MM_COLS, (c + 1) * MM_COLS)
        o_ref[:, cols] = jnp.where(f == 0, 0.0, o_ref[:, cols]) + _dot(act, wo_ref[:, cols])

    @pl.when(f == pl.num_programs(1) - 1)
    def _():
        o_ref[...] = x_ref[...] + (0.5 * gate_ref[0]) * o_ref[...]


def _ffn(x, mods3, slot, g_row, w_in, w_out, nxt):
    tm, tf = FFN_TM, FFN_TF
    nf = D_FF // tf
    in_specs = [
        pl.BlockSpec((tm, D_MODEL), lambda i, f: (i, 0)),
        _mod_spec(slot, tm), _mod_spec(slot + 1, tm), _mod_spec(slot + 2, tm),
        pl.BlockSpec((1, D_MODEL), lambda i, f: (0, 0)),
        pl.BlockSpec((D_MODEL, tf), lambda i, f: (0, f)),
        pl.BlockSpec((D_MODEL, tf), lambda i, f: (0, f + nf)),
        pl.BlockSpec((tf, D_MODEL), lambda i, f: (f, 0)),
    ]
    args = [x, mods3, mods3, mods3, g_row, w_in, w_in, w_out]
    out_specs = [pl.BlockSpec((tm, D_MODEL), lambda i, f: (i, 0))]
    out_shape = [jax.ShapeDtypeStruct((N_TOK, D_MODEL), F32)]
    if nxt is not None:
        n_in, n_out, layer, which = nxt
        n_steps = (N_TOK // tm) * nf
        rows_in, rows_out = _cast_rows(D_MODEL, n_steps), _cast_rows(D_FF, n_steps)
        last_in, last_out = D_MODEL // rows_in - 1, D_FF // rows_out - 1
        blk_in = lambda i, f: jnp.minimum(i * nf + f, last_in)
        blk_out = lambda i, f: jnp.minimum(i * nf + f, last_out)
        in_specs += [
            pl.BlockSpec((1, 1, rows_in, 2 * D_FF), lambda i, f: (layer, which, blk_in(i, f), 0)),
            pl.BlockSpec((1, 1, rows_out, D_MODEL), lambda i, f: (layer, which, blk_out(i, f), 0)),
        ]
        args += [n_in, n_out]
        out_specs += [
            pl.BlockSpec((rows_in, 2 * D_FF), lambda i, f: (blk_in(i, f), 0)),
            pl.BlockSpec((rows_out, D_MODEL), lambda i, f: (blk_out(i, f), 0)),
        ]
        out_shape += [jax.ShapeDtypeStruct((D_MODEL, 2 * D_FF), BF16), jax.ShapeDtypeStruct((D_FF, D_MODEL), BF16)]
    return pl.pallas_call(
        functools.partial(_ffn_kernel, nxt is not None),
        grid=(N_TOK // tm, nf),
        in_specs=in_specs,
        out_specs=out_specs,
        out_shape=out_shape,
        scratch_shapes=[pltpu.VMEM((tm, D_MODEL), BF16)],
        compiler_params=_cparams(2),
        name="ffn",
    )(*args)


def _resident(shape):
    return pl.BlockSpec(shape, lambda *_: (0,) * len(shape), pipeline_mode=pl.Buffered(1))


def _mm_mod_kernel(x_ref, shift_ref, scale_ref, g_ref, w_ref, o_ref):
    h = _modulated(x_ref[...], g_ref[...], shift_ref[0], scale_ref[0]).astype(BF16)
    for c in range(o_ref.shape[1] // MM_COLS):
        cols = slice(c * MM_COLS, (c + 1) * MM_COLS)
        o_ref[:, cols] = _dot(h, w_ref[:, cols])


def _mm_mod(x, mods3, slot, g_row, w, tm, name):
    n = w.shape[1]
    mod = lambda s: pl.BlockSpec((1, 1, D_MODEL), lambda i: (_mod_row(i, tm), 0, s))
    return pl.pallas_call(
        _mm_mod_kernel,
        grid=(N_TOK // tm,),
        in_specs=[
            pl.BlockSpec((tm, D_MODEL), lambda i: (i, 0)),
            mod(slot), mod(slot + 1),
            _resident((1, D_MODEL)),
            _resident((D_MODEL, n)),
        ],
        out_specs=pl.BlockSpec((tm, n), lambda i: (i, 0)),
        out_shape=jax.ShapeDtypeStruct((N_TOK, n), F32),
        compiler_params=_cparams(1),
        name=name,
    )(x, mods3, mods3, g_row, w)


def _mm_res_kernel(y_ref, w_ref, x_ref, gate_ref, o_ref):
    y = y_ref[...]
    for c in range(o_ref.shape[1] // MM_COLS):
        cols = slice(c * MM_COLS, (c + 1) * MM_COLS)
        o_ref[:, cols] = x_ref[:, cols] + gate_ref[0][:, cols] * _dot(y, w_ref[:, cols])


def _mm_res(y, w, x, mods3, slot, name):
    tm = 512
    k = y.shape[1]
    return pl.pallas_call(
        _mm_res_kernel,
        grid=(N_TOK // tm,),
        in_specs=[
            pl.BlockSpec((tm, k), lambda i: (i, 0)),
            _resident((k, D_MODEL)),
            pl.BlockSpec((tm, D_MODEL), lambda i: (i, 0)),
            pl.BlockSpec((1, 1, D_MODEL), lambda i: (_mod_row(i, tm), 0, slot)),
        ],
        out_specs=pl.BlockSpec((tm, D_MODEL), lambda i: (i, 0)),
        out_shape=jax.ShapeDtypeStruct((N_TOK, D_MODEL), F32),
        compiler_params=_cparams(1),
        name=name,
    )(y, w, x, mods3)


def _token_zeros():
    return jnp.zeros((N_TOK, D_MODEL), BF16)


def _rms_w(x, w):
    return x * lax.rsqrt(jnp.mean(x * x, axis=-1, keepdims=True) + EPS) * w


def _softmax_parts(scores, sink):
    m = sink
    for s in scores:
        m = jnp.maximum(jnp.max(s, axis=-1, keepdims=True), m)
    es = [jnp.exp(s - m) for s in scores]
    den = jnp.exp(sink - m)
    for e in es:
        den = den + jnp.sum(e, axis=-1, keepdims=True)
    return es, den


ATTN_CTX_KVH = 4


def _attn_ctx_kernel(sink_ref, q_ref, k_ref, v_ref, qn_ref, kn_ref, _y_init, o_ref, ko_ref, vo_ref):
    h2 = pl.program_id(1)
    hd = ATTN_HEAD_DIM
    chains = []
    for kv in range(ATTN_CTX_KVH):
        cols = slice(kv * hd, (kv + 1) * hd)
        k = _rms_w(k_ref[:, cols], kn_ref[...])
        v = v_ref[:, cols]
        ko_ref[:, cols] = k
        vo_ref[:, cols] = v
        kb, vb = k.astype(BF16), v.astype(BF16)
        for g in range(ATTN_GROUP):
            head = kv * ATTN_GROUP + g
            q = _rms_w(q_ref[:, head * hd:(head + 1) * hd], qn_ref[...]).astype(BF16)
            chains.append(dict(head=head, q=q, kb=kb, vb=vb))
    for d in chains:
        d["s"] = _dot_nt(d["q"], d["kb"]) * hd ** -0.5
    for d in chains:
        (e,), den = _softmax_parts([d["s"]], sink_ref[h2 * ATTN_CTX_KVH * ATTN_GROUP + d["head"]])
        d["p"] = (e / den).astype(BF16)
    for d in chains:
        d["o"] = _dot(d["p"], d["vb"])
    for d in chains:
        o_ref[:, d["head"] * hd:(d["head"] + 1) * hd] = d["o"].astype(BF16)


def _attn_ctx(qkv, sink, q_norm, k_norm, n_seq, t):
    hd = ATTN_HEAD_DIM
    nkv = ATTN_CTX_KVH
    gw = nkv * ATTN_GROUP * hd
    kcol = ATTN_HEADS // nkv
    vcol = kcol + ATTN_KV_HEADS // nkv
    n_tok = n_seq * t
    return pl.pallas_call(
        _attn_ctx_kernel,
        grid=(n_seq, ATTN_KV_HEADS // nkv),
        in_specs=[
            pl.BlockSpec(memory_space=pltpu.SMEM),
            pl.BlockSpec((t, gw), lambda b, h: (b, h)),
            pl.BlockSpec((t, nkv * hd), lambda b, h: (b, kcol + h)),
            pl.BlockSpec((t, nkv * hd), lambda b, h: (b, vcol + h)),
            pl.BlockSpec((1, hd), lambda b, h: (0, 0)),
            pl.BlockSpec((1, hd), lambda b, h: (0, 0)),
            pl.BlockSpec(memory_space=pl.ANY),
        ],
        out_specs=[
            pl.BlockSpec((t, gw), lambda b, h: (b, h)),
            pl.BlockSpec((t, nkv * hd), lambda b, h: (b, h)),
            pl.BlockSpec((t, nkv * hd), lambda b, h: (b, h)),
        ],
        input_output_aliases={6: 0},
        out_shape=[
            jax.ShapeDtypeStruct((N_TOK, ATTN_HEADS * hd), BF16),
            jax.ShapeDtypeStruct((n_tok, ATTN_KV_HEADS * hd), F32),
            jax.ShapeDtypeStruct((n_tok, ATTN_KV_HEADS * hd), F32),
        ],
        compiler_params=_cparams(2),
        name="attn_ctx",
    )(sink, qkv, qkv, qkv, q_norm[None, :], k_norm[None, :], _token_zeros())


def _rope(x, cos, sin_a, sin_b):
    quarter = ATTN_HEAD_DIM // 4
    return x * cos + pltpu.roll(x, ATTN_HEAD_DIM - quarter, 1) * sin_a + pltpu.roll(x, quarter, 1) * sin_b


def _rope_tables(t):
    hd = ATTN_HEAD_DIM
    quarter = hd // 4
    pos = jnp.arange(t)
    row = (pos // GRID_W).astype(F32)
    col = (pos % GRID_W).astype(F32)
    inv_freq = ROPE_THETA ** (-jnp.arange(quarter, dtype=F32) / quarter)
    lane = jnp.arange(hd)
    ang = jnp.where(lane[None, :] < hd // 2, row[:, None], col[:, None]) * inv_freq[lane % quarter][None, :]
    first = (lane % (hd // 2)) < quarter
    sin = jnp.sin(ang)
    return jnp.cos(ang), jnp.where(first[None, :], -sin, 0.0), jnp.where(first[None, :], 0.0, sin)


ATTN_LAT_NQB = 2


def _attn_lat_kernel(sink_ref, q_ref, k_ref, v_ref, ck_ref, cv_ref, qn_ref, kn_ref, cos_ref, sa_ref, sb_ref, _y_ctx,
                     o_ref, kb_scr, vb_scr):
    kvh = pl.program_id(1)
    hd = ATTN_HEAD_DIM
    t = q_ref.shape[0]
    span = ATTN_BLOCK + 2 * ATTN_WINDOW
    k = _rope(_rms_w(k_ref[...], kn_ref[...]), cos_ref[...], sa_ref[...], sb_ref[...])
    kb_scr[...] = k.astype(BF16)
    vb_scr[...] = v_ref[...].astype(BF16)
    ckb = ck_ref[0, 0].astype(BF16)
    cvb = cv_ref[0, 0].astype(BF16)
    q_iota = lax.broadcasted_iota(jnp.int32, (ATTN_BLOCK, span), 0)
    k_iota = lax.broadcasted_iota(jnp.int32, (ATTN_BLOCK, span), 1)

    def blocks(j, carry):
        ch = []
        for sub in range(ATTN_LAT_NQB):
            q0 = pl.multiple_of((j * ATTN_LAT_NQB + sub) * ATTN_BLOCK, ATTN_BLOCK)
            ws = pl.multiple_of(jnp.clip(q0 - ATTN_WINDOW, 0, t - span), ATTN_BLOCK)
            rows = pl.ds(q0, ATTN_BLOCK)
            valid = jnp.abs((k_iota + ws) - (q_iota + q0)) <= ATTN_WINDOW
            cos, sa, sb = cos_ref[rows, :], sa_ref[rows, :], sb_ref[rows, :]
            for g in range(ATTN_GROUP):
                qb = _rope(_rms_w(q_ref[rows, g * hd:(g + 1) * hd], qn_ref[...]), cos, sa, sb).astype(BF16)
                ch.append(dict(g=g, rows=rows, qb=qb, valid=valid,
                               kw=kb_scr[pl.ds(ws, span), :], vw=vb_scr[pl.ds(ws, span), :]))
        for d in ch:
            d["s_win"] = jnp.where(d["valid"], _dot_nt(d["qb"], d["kw"]) * hd ** -0.5, NEG_INF)
            d["s_ctx"] = _dot_nt(d["qb"], ckb) * hd ** -0.5
        for d in ch:
            (e_win, e_ctx), den = _softmax_parts([d["s_win"], d["s_ctx"]], sink_ref[kvh * ATTN_GROUP + d["g"]])
            d["p"] = ((e_win / den).astype(BF16), (e_ctx / den).astype(BF16))
        for d in ch:
            d["o"] = _dot(d["p"][0], d["vw"]) + _dot(d["p"][1], cvb)
        for d in ch:
            o_ref[d["rows"], d["g"] * hd:(d["g"] + 1) * hd] = d["o"].astype(BF16)
        return carry

    lax.fori_loop(0, t // (ATTN_BLOCK * ATTN_LAT_NQB), blocks, 0)


def _attn_lat(qkv, cache_k, cache_v, slot, sink, q_norm, k_norm, y_ctx, n_seq, t, row_off):
    hd = ATTN_HEAD_DIM
    gw = ATTN_GROUP * hd
    kcol = ATTN_HEADS
    vcol = kcol + ATTN_KV_HEADS
    n_past = cache_k.shape[2]
    ck = cache_k.reshape(cache_k.shape[0], cache_k.shape[1], n_past, ATTN_KV_HEADS * hd)
    cv = cache_v.reshape(ck.shape)
    cos, sin_a, sin_b = _rope_tables(t)
    tab = pl.BlockSpec((t, hd), lambda b, h: (0, 0))
    return pl.pallas_call(
        _attn_lat_kernel,
        grid=(n_seq, ATTN_KV_HEADS),
        in_specs=[
            pl.BlockSpec(memory_space=pltpu.SMEM),
            pl.BlockSpec((t, gw), lambda b, h: (b + row_off, h)),
            pl.BlockSpec((t, hd), lambda b, h: (b + row_off, kcol + h)),
            pl.BlockSpec((t, hd), lambda b, h: (b + row_off, vcol + h)),
            pl.BlockSpec((1, 1, n_past, hd), lambda b, h: (b, slot, 0, h)),
            pl.BlockSpec((1, 1, n_past, hd), lambda b, h: (b, slot, 0, h)),
            pl.BlockSpec((1, hd), lambda b, h: (0, 0)),
            pl.BlockSpec((1, hd), lambda b, h: (0, 0)),
            tab, tab, tab,
            pl.BlockSpec(memory_space=pl.ANY),
        ],
        out_specs=pl.BlockSpec((t, gw), lambda b, h: (b + row_off, h)),
        out_shape=jax.ShapeDtypeStruct(y_ctx.shape, BF16),
        input_output_aliases={11: 0},
        scratch_shapes=[pltpu.VMEM((t, hd), BF16), pltpu.VMEM((t, hd), BF16)],
        compiler_params=_cparams(2),
        name="attn_lat",
    )(sink, qkv, qkv, qkv, ck, cv, q_norm[None, :], k_norm[None, :], cos, sin_a, sin_b, y_ctx)


def _cumsum_rows(x, reverse):
    n = x.shape[0]
    row = lax.broadcasted_iota(jnp.int32, x.shape, 0)
    sh = 1
    while sh < n:
        if reverse:
            x = x + jnp.where(row < n - sh, pltpu.roll(x, n - sh, 0), 0.0)
        else:
            x = x + jnp.where(row >= sh, pltpu.roll(x, sh, 0), 0.0)
        sh *= 2
    return x


def _log_sigmoid(x):
    return jnp.minimum(x, 0.0) - jnp.log(1.0 + jnp.exp(-jnp.abs(x)))


def _mlstm_kernel(has_init, emit_state, hp, *refs):
    refs = list(refs)
    q_ref, k_ref, v_ref, og_ref, gt_ref, bg_ref, on_ref = refs[:7]
    refs = refs[7:]
    if has_init:
        c0_ref, n0_ref, m0_ref = refs[:3]
        refs = refs[3:]
    _y_prev, y_ref = refs[:2]
    refs = refs[2:]
    if emit_state:
        co_ref, no_ref, mo_ref = refs[:3]
        refs = refs[3:]
    hf_scr, hb_scr = refs

    L = MLSTM_CHUNK
    t = q_ref.shape[0]
    nc = t // L
    dk, dv = MLSTM_DK, MLSTM_DV
    row = lax.broadcasted_iota(jnp.int32, (L, L), 0)
    col = lax.broadcasted_iota(jnp.int32, (L, L), 1)
    lane = lax.broadcasted_iota(jnp.int32, (L, LANES), 1)

    def chunk_step(chains):
        st = []
        for z, hh, c, C, n, m in chains:
            rows = pl.ds(pl.multiple_of(c * L, L), L)
            qc = q_ref[rows, hh * dk:(hh + 1) * dk] * dk ** -0.5
            kc = k_ref[rows, hh * dk:(hh + 1) * dk]
            vc = v_ref[rows, hh * dv:(hh + 1) * dv]
            G = gt_ref[rows, hh * LANES:(hh + 1) * LANES] + bg_ref[hh]
            CUM = _cumsum_rows(_log_sigmoid(G), reverse=(z == 1))
            ci, cf = 2 * z, 2 * z + 1
            ZT = jnp.where(lane == cf, CUM, G).T
            i_row, cum_row = ZT[ci:ci + 1, :L], ZT[cf:cf + 1, :L]
            i_col, cum_col = G[:, ci:ci + 1], CUM[:, cf:cf + 1]
            mask = (col <= row) if z == 0 else (col >= row)
            dmat = jnp.where(mask, cum_col - cum_row + i_row, NEG_INF)
            inter = cum_col + m
            m_c = jnp.maximum(inter, jnp.max(dmat, axis=-1, keepdims=True))
            bl = cum_row[:, L - 1:L] if z == 0 else cum_row[:, 0:1]
            m_new = jnp.maximum(bl + m, jnp.max(bl - cum_row + i_row, axis=-1, keepdims=True))
            kw = kc * jnp.exp(bl - cum_col + i_col - m_new)
            st.append(dict(
                z=z, hh=hh, rows=rows, C=C, n=n, qc=qc, m_c=m_c, m_new=m_new,
                qb=qc.astype(BF16), kb=kc.astype(BF16), vb=vc.astype(BF16), Cb=C.astype(BF16),
                w=jnp.exp(dmat - m_c), a=jnp.exp(inter - m_c), decay=jnp.exp(bl + m - m_new),
                kw=kw, kwb=kw.astype(BF16)))
        for d in st:
            d["qk"] = _dot_nt(d["qb"], d["kb"])
            d["qC"] = _dot(d["qb"], d["Cb"])
            d["kv"] = _dot_tn(d["kwb"], d["vb"])
        for d in st:
            d["s"] = d["qk"] * d["w"]
        for d in st:
            d["sv"] = _dot(d["s"].astype(BF16), d["vb"])
        out = []
        for d in st:
            num = d["sv"] + d["a"] * d["qC"]
            den = jnp.sum(d["s"], axis=-1, keepdims=True) + d["a"] * jnp.sum(d["qc"] * d["n"], axis=-1, keepdims=True)
            scr = hf_scr if d["z"] == 0 else hb_scr
            scr[d["rows"], d["hh"] * dv:(d["hh"] + 1) * dv] = num / jnp.maximum(jnp.abs(den), jnp.exp(-d["m_c"]))
            out += [d["decay"] * d["C"] + d["kv"],
                    d["decay"] * d["n"] + jnp.sum(d["kw"], axis=0, keepdims=True),
                    d["m_new"]]
        return out

    def body(ci, carry):
        chains = []
        for hh in range(hp):
            Cf, nf, mf, Cb, nb, mb = carry[6 * hh:6 * hh + 6]
            chains.append((0, hh, ci, Cf, nf, mf))
            chains.append((1, hh, nc - 1 - ci, Cb, nb, mb))
        return tuple(chunk_step(chains))

    init = []
    for hh in range(hp):
        for z in range(2):
            if has_init:
                init += [c0_ref[0, z, hh], n0_ref[0, z, hh], m0_ref[0, z, hh][:, 0:1]]
            else:
                init += [jnp.zeros((dk, dv), F32), jnp.zeros((1, dk), F32), jnp.zeros((1, 1), F32)]
    final = lax.fori_loop(0, nc, body, tuple(init))

    for hh in range(hp):
        cols = slice(hh * dv, (hh + 1) * dv)
        hs = hf_scr[:, cols] + hb_scr[:, cols]
        y_ref[:, cols] = (_rms_w(hs, on_ref[:, cols]) * _sigmoid(og_ref[:, cols])).astype(BF16)
        if emit_state:
            for z in range(2):
                C, n, m = final[6 * hh + 3 * z:6 * hh + 3 * z + 3]
                co_ref[0, z, hh] = C
                no_ref[0, z, hh] = n
                mo_ref[0, z, hh] = jnp.broadcast_to(m, (1, LANES))


def _mlstm(proj, gates, b_gate_h, out_norm, state0, y_prev, n_seq, t, row_off, emit_state, hp):
    h_, dk, dv = MLSTM_HEADS, MLSTM_DK, MLSTM_DV
    nhb = h_ // hp
    has_init = state0 is not None
    in_specs = [
        pl.BlockSpec((t, hp * dk), lambda b, h: (b + row_off, h)),
        pl.BlockSpec((t, hp * dk), lambda b, h: (b + row_off, nhb + h)),
        pl.BlockSpec((t, hp * dv), lambda b, h: (b + row_off, nhb + h)),
        pl.BlockSpec((t, hp * dv), lambda b, h: (b + row_off, 2 * nhb + h)),
        pl.BlockSpec((t, hp * LANES), lambda b, h: (b + row_off, h)),
        pl.BlockSpec((hp, 1, LANES), lambda b, h: (h, 0, 0)),
        pl.BlockSpec((1, hp * dv), lambda b, h: (0, h)),
    ]
    args = [proj, proj, proj, proj, gates, b_gate_h[:, None, :], out_norm[None, :]]
    if has_init:
        c0, n0, m0 = state0
        in_specs += [
            pl.BlockSpec((1, 2, hp, dk, dv), lambda b, h: (b, 0, h, 0, 0)),
            pl.BlockSpec((1, 2, hp, 1, dk), lambda b, h: (b, 0, h, 0, 0)),
            pl.BlockSpec((1, 2, hp, 1, LANES), lambda b, h: (b, 0, h, 0, 0)),
        ]
        args += [c0, n0.reshape(n_seq, 2, h_, 1, dk),
                 jnp.broadcast_to(m0[..., None, None], (n_seq, 2, h_, 1, LANES))]
    in_specs.append(pl.BlockSpec(memory_space=pl.ANY))
    args.append(y_prev)
    aliases = {len(args) - 1: 0}
    out_specs = [pl.BlockSpec((t, hp * dv), lambda b, h: (b + row_off, h))]
    out_shape = [jax.ShapeDtypeStruct((N_TOK, D_MODEL), BF16)]
    if emit_state:
        out_specs += [
            pl.BlockSpec((1, 2, hp, dk, dv), lambda b, h: (b, 0, h, 0, 0)),
            pl.BlockSpec((1, 2, hp, 1, dk), lambda b, h: (b, 0, h, 0, 0)),
            pl.BlockSpec((1, 2, hp, 1, LANES), lambda b, h: (b, 0, h, 0, 0)),
        ]
        out_shape += [
            jax.ShapeDtypeStruct((n_seq, 2, h_, dk, dv), F32),
            jax.ShapeDtypeStruct((n_seq, 2, h_, 1, dk), F32),
            jax.ShapeDtypeStruct((n_seq, 2, h_, 1, LANES), F32),
        ]
    return pl.pallas_call(
        functools.partial(_mlstm_kernel, has_init, emit_state, hp),
        grid=(n_seq, nhb),
        in_specs=in_specs,
        out_specs=out_specs,
        out_shape=out_shape,
        input_output_aliases=aliases,
        scratch_shapes=[pltpu.VMEM((t, hp * dv), F32), pltpu.VMEM((t, hp * dv), F32)],
        compiler_params=_cparams(2),
        name="mlstm",
    )(*args)


def _rwkv_mix_kernel(x_ref, xp_ref, xn_ref, shift_ref, scale_ref, g_ref, mu_ref, o_ref, h_scr):
    i = pl.program_id(0)
    tm = x_ref.shape[0]
    g, sh, sc = g_ref[...], shift_ref[0], scale_ref[0]
    n_ctx_tiles = N_CTX_TOK // tm
    tiles_per_lat = DEC_SEQ // tm
    tile_in_seq = jnp.where(i < n_ctx_tiles, i % (SEQ // tm), (i - n_ctx_tiles) % tiles_per_lat)
    tiles_in_seq = jnp.where(i < n_ctx_tiles, SEQ // tm, tiles_per_lat)
    has_prev = (tile_in_seq > 0).astype(F32)
    has_next = (tile_in_seq < tiles_in_seq - 1).astype(F32)
    h_scr[0:8, :] = _modulated(xp_ref[...], g, sh, sc) * has_prev
    h_scr[8 + tm:16 + tm, :] = _modulated(xn_ref[...], g, sh, sc) * has_next

    def fill(c, carry):
        r0 = pl.multiple_of(c * ROW_CHUNK, ROW_CHUNK)
        h_scr[pl.ds(8 + r0, ROW_CHUNK), :] = _modulated(x_ref[pl.ds(r0, ROW_CHUNK), :], g, sh, sc)
        return carry

    lax.fori_loop(0, tm // ROW_CHUNK, fill, 0, unroll=4)

    def mix(c, carry):
        r0 = pl.multiple_of(c * ROW_CHUNK, ROW_CHUNK)
        ext = h_scr[pl.ds(r0, ROW_CHUNK + 16), :]
        h = ext[8:8 + ROW_CHUNK]
        prev = pltpu.roll(ext, 1, 0)[8:8 + ROW_CHUNK]
        nxt = pltpu.roll(ext, ROW_CHUNK + 15, 0)[8:8 + ROW_CHUNK]
        cs = 0.5 * (prev + nxt) - h
        for p in range(o_ref.shape[0]):
            o_ref[p, pl.ds(r0, ROW_CHUNK), :] = (h + cs * mu_ref[p:p + 1, :]).astype(BF16)
        return carry

    lax.fori_loop(0, tm // ROW_CHUNK, mix, 0, unroll=2)


def _rwkv_mix(x, mods3, g_row, mu):
    tm = 256
    nb8 = tm // 8
    last = N_TOK // 8 - 1
    return pl.pallas_call(
        _rwkv_mix_kernel,
        grid=(N_TOK // tm,),
        in_specs=[
            pl.BlockSpec((tm, D_MODEL), lambda i: (i, 0)),
            pl.BlockSpec((8, D_MODEL), lambda i: (jnp.maximum(i * nb8 - 1, 0), 0)),
            pl.BlockSpec((8, D_MODEL), lambda i: (jnp.minimum((i + 1) * nb8, last), 0)),
            pl.BlockSpec((1, 1, D_MODEL), lambda i: (_mod_row(i, tm), 0, 3)),
            pl.BlockSpec((1, 1, D_MODEL), lambda i: (_mod_row(i, tm), 0, 4)),
            pl.BlockSpec((1, D_MODEL), lambda i: (0, 0)),
            pl.BlockSpec((6, D_MODEL), lambda i: (0, 0)),
        ],
        out_specs=pl.BlockSpec((6, tm, D_MODEL), lambda i: (0, i, 0)),
        out_shape=jax.ShapeDtypeStruct((6, N_TOK, D_MODEL), BF16),
        scratch_shapes=[pltpu.VMEM((tm + 16, D_MODEL), F32)],
        compiler_params=_cparams(1),
        name="rwkv_mix",
    )(x, x, x, mods3, mods3, g_row, mu)


def _lora_kernel(act, post, x_ref, a_ref, b_ref, bias_ref, o_ref):
    u = _dot(x_ref[0], a_ref[0])
    if act == "tanh":
        u = jnp.tanh(u)
    elif act == "sigmoid":
        u = _sigmoid(u)
    y = _dot(u.astype(BF16), b_ref[0]) + bias_ref[0]
    if post == "log_decay":
        y = -(_sigmoid(y) * jnp.exp(jnp.float32(-0.5)))
    elif post == "sigmoid":
        y = _sigmoid(y)
    o_ref[0] = y


def _lora(xs, p, wa, wb, bias, act, post, name):
    tm = 512
    nz, _, r = wa.shape
    return pl.pallas_call(
        functools.partial(_lora_kernel, act, post),
        grid=(nz, N_TOK // tm),
        in_specs=[
            pl.BlockSpec((1, tm, D_MODEL), lambda z, i: (p, i, 0)),
            pl.BlockSpec((1, D_MODEL, r), lambda z, i: (z, 0, 0)),
            pl.BlockSpec((1, r, D_MODEL), lambda z, i: (z, 0, 0)),
            pl.BlockSpec((1, 1, D_MODEL), lambda z, i: (z, 0, 0)),
        ],
        out_specs=pl.BlockSpec((1, tm, D_MODEL), lambda z, i: (z, i, 0)),
        out_shape=jax.ShapeDtypeStruct((nz, N_TOK, D_MODEL), F32),
        compiler_params=_cparams(2),
        name=name,
    )(xs, wa, wb, bias)


def _rkv_kernel(x_ref, w_ref, o_ref):
    x = x_ref[0]
    for c in range(o_ref.shape[2] // MM_COLS):
        cols = slice(c * MM_COLS, (c + 1) * MM_COLS)
        o_ref[0, :, cols] = _dot(x, w_ref[0, :, cols])


def _rkv(xs, w_rkv):
    tm = 512
    return pl.pallas_call(
        _rkv_kernel,
        grid=(3, N_TOK // tm),
        in_specs=[pl.BlockSpec((1, tm, D_MODEL), lambda p, i: (p, i, 0)),
                  pl.BlockSpec((1, D_MODEL, D_MODEL), lambda p, i: (p, 0, 0))],
        out_specs=pl.BlockSpec((1, tm, D_MODEL), lambda p, i: (p, i, 0)),
        out_shape=jax.ShapeDtypeStruct((3, N_TOK, D_MODEL), F32),
        compiler_params=_cparams(2),
        name="rwkv_rkv",
    )(xs, w_rkv)


def _head_sum(x, first_head):
    s0 = jnp.sum(jnp.where(first_head, x, 0.0), axis=-1, keepdims=True)
    s1 = jnp.sum(jnp.where(first_head, 0.0, x), axis=-1, keepdims=True)
    return jnp.where(first_head, s0, s1)


def _rwkv_kernel(has_init, emit_state, npp, *refs):
    refs = list(refs)
    r_ref, k_ref, v_ref, lw_ref, a_ref, g_ref, kk_w_ref, ka_ref, rk_ref, lng_ref, lnb_ref = refs[:11]
    refs = refs[11:]
    if has_init:
        s0_ref = refs[0]
        refs = refs[1:]
    _y_prev, y_ref = refs[:2]
    refs = refs[2:]
    if emit_state:
        so_ref = refs[0]
        refs = refs[1:]
    kk_scr, yf_scr, yb_scr = refs

    L = RWKV_CHUNK
    N = RWKV_HEAD
    t = r_ref.shape[1]
    nc = t // L
    row = lax.broadcasted_iota(jnp.int32, (L, L), 0)
    col = lax.broadcasted_iota(jnp.int32, (L, L), 1)
    first_head = lax.broadcasted_iota(jnp.int32, (L, LANES), 1) < N
    sq_row = lax.broadcasted_iota(jnp.int32, (LANES, LANES), 0)
    sq_col = lax.broadcasted_iota(jnp.int32, (LANES, LANES), 1)
    same_head = (sq_row < N) == (sq_col < N)

    first_head_t = lax.broadcasted_iota(jnp.int32, (t, LANES), 1) < N

    for pp in range(npp):
        lanes = slice(pp * LANES, (pp + 1) * LANES)
        kkx = k_ref[0, :, lanes] * kk_w_ref[:, lanes]
        kk_scr[:, lanes] = kkx / jnp.maximum(jnp.sqrt(_head_sum(kkx * kkx, first_head_t)), 1e-12)

    def chunk_step(chains):
        heads = (first_head, jnp.logical_not(first_head))
        pick = lambda x0, x1: jnp.where(first_head, x0, x1)
        st = []
        for z, pp, c, S in chains:
            lanes = slice(pp * LANES, (pp + 1) * LANES)
            sl = pl.ds(pl.multiple_of(c * L, L), L)
            r_, k_, v_ = r_ref[0, sl, lanes], k_ref[0, sl, lanes], v_ref[0, sl, lanes]
            lw, a_ = lw_ref[z, sl, lanes], a_ref[z, sl, lanes]
            kk = kk_scr[sl, lanes]
            kd = k_ * (1.0 + (a_ - 1.0) * ka_ref[:, lanes])
            b_ = kk * a_
            cum = _cumsum_rows(lw, reverse=(z == 1))
            cum_end = cum[L - 1:L, :] if z == 0 else cum[0:1, :]
            kp = (kk * jnp.exp(cum - lw)).astype(BF16)
            rp = (r_ * jnp.exp(cum)).astype(BF16)
            inv = jnp.exp(-cum)
            to_end = jnp.exp(cum_end - cum)
            zero = jnp.zeros_like(kp)
            st.append(dict(
                z=z, pp=pp, sl=sl, lanes=lanes, S=S, kp=kp, rp=rp, cum_end=cum_end,
                kt=(kd * inv).astype(BF16), bt=(b_ * inv).astype(BF16),
                kb_end=jnp.concatenate([kd * to_end, -(b_ * to_end)], axis=0).astype(BF16),
                vb=v_.astype(BF16), Sb=S.astype(BF16),
                probe=[jnp.concatenate([jnp.where(hm, kp, zero), jnp.where(hm, rp, zero)], axis=0) for hm in heads],
                strict=(col < row) if z == 0 else (col > row),
                incl=(col <= row) if z == 0 else (col >= row)))
        for d in st:
            d["gk"] = [_dot_nt(pr, d["kt"]) for pr in d["probe"]]
            d["gb"] = [_dot_nt(pr, d["bt"]) for pr in d["probe"]]
        for d in st:
            d["g1"] = [jnp.where(d["strict"], g[:L], 0.0).astype(BF16) for g in d["gk"]]
            d["n"] = [jnp.where(d["strict"], g[:L], 0.0).astype(BF16) for g in d["gb"]]
            d["g3"] = [jnp.where(d["incl"], g[L:], 0.0).astype(BF16) for g in d["gk"]]
            d["g4"] = [jnp.where(d["incl"], g[L:], 0.0).astype(BF16) for g in d["gb"]]
        for d in st:
            d["g1v"] = [_dot(g, d["vb"]) for g in d["g1"]]
            d["kps"] = _dot_nt(d["kp"], d["Sb"])
            d["rps"] = _dot_nt(d["rp"], d["Sb"])
        for d in st:
            d["u"] = d["kps"] + pick(*d["g1v"])
        sign = -1.0
        p = 1
        while p < L:
            for d in st:
                ub = d["u"].astype(BF16)
                d["nu"] = [_dot(m, ub) for m in d["n"]]
                if 2 * p < L:
                    d["n"] = [_dot(m, m).astype(BF16) for m in d["n"]]
            for d in st:
                d["u"] = d["u"] + sign * pick(*d["nu"])
            sign = 1.0
            p *= 2
        for d in st:
            ub = d["u"].astype(BF16)
            d["y3"] = [_dot(g, d["vb"]) for g in d["g3"]]
            d["y4"] = [_dot(g, ub) for g in d["g4"]]
            d["upd"] = _dot_tn(jnp.concatenate([d["vb"], ub], axis=0), d["kb_end"])
        out = []
        for d in st:
            y = d["rps"] + pick(d["y3"][0] - d["y4"][0], d["y3"][1] - d["y4"][1])
            (yf_scr if d["z"] == 0 else yb_scr)[d["sl"], d["lanes"]] = y
            out.append(d["S"] * jnp.exp(d["cum_end"]) + jnp.where(same_head, d["upd"], 0.0))
        return out

    def body(ci, carry):
        chains = []
        for pp in range(npp):
            chains.append((0, pp, ci, carry[2 * pp]))
            chains.append((1, pp, nc - 1 - ci, carry[2 * pp + 1]))
        return tuple(chunk_step(chains))

    init = []
    for pp in range(npp):
        for z in range(2):
            if has_init:
                s = s0_ref[0, z, pp]
                init.append(jnp.where(sq_row < N, s, pltpu.roll(s, N, 1)))
            else:
                init.append(jnp.zeros((LANES, LANES), F32))
    final = lax.fori_loop(0, nc, body, tuple(init))

    for pp in range(npp):
        lanes = slice(pp * LANES, (pp + 1) * LANES)
        r_, k_, v_ = r_ref[0, :, lanes], k_ref[0, :, lanes], v_ref[0, :, lanes]
        y = yf_scr[:, lanes] + yb_scr[:, lanes]
        mean = _head_sum(y, first_head_t) * (1.0 / N)
        yc = y - mean
        var = _head_sum(yc * yc, first_head_t) * (1.0 / N)
        yn = yc * lax.rsqrt(var + RWKV_GN_EPS) * lng_ref[:, lanes] + lnb_ref[:, lanes]
        ka = ka_ref[:, lanes]
        kd_sum = k_ * (1.0 + (a_ref[0, :, lanes] - 1.0) * ka) + k_ * (1.0 + (a_ref[1, :, lanes] - 1.0) * ka)
        bonus = _head_sum(r_ * kd_sum * rk_ref[:, lanes], first_head_t) * v_
        y_ref[:, lanes] = ((yn + bonus) * g_ref[:, lanes]).astype(BF16)
        if emit_state:
            for z in range(2):
                S = final[2 * pp + z]
                so_ref[0, z, pp] = (S + pltpu.roll(S, N, 1))[:, :N]


def _rwkv(rkv, lw, a, g, k_k, k_a, r_k, ln_g, ln_b, s0, y_prev, n_seq, t, row_off, emit_state, npp):
    has_init = s0 is not None
    w = npp * LANES
    tok = lambda lead, first: pl.BlockSpec((lead, t, w), lambda b, p: (first, b + row_off, p))
    par = pl.BlockSpec((1, w), lambda b, p: (0, p))
    in_specs = [tok(1, 0), tok(1, 1), tok(1, 2), tok(2, 0), tok(2, 0),
                pl.BlockSpec((t, w), lambda b, p: (b + row_off, p)),
                par, par, par, par, par]
    args = [rkv, rkv, rkv, lw, a, g, k_k[None, :], k_a[None, :], r_k.reshape(1, D_MODEL), ln_g[None, :], ln_b[None, :]]
    if has_init:
        in_specs.append(pl.BlockSpec((1, 2, npp, LANES, LANES), lambda b, p: (b, 0, p, 0, 0)))
        args.append(s0)
    in_specs.append(pl.BlockSpec(memory_space=pl.ANY))
    args.append(y_prev)
    aliases = {len(args) - 1: 0}
    out_specs = [pl.BlockSpec((t, w), lambda b, p: (b + row_off, p))]
    out_shape = [jax.ShapeDtypeStruct((N_TOK, D_MODEL), BF16)]
    if emit_state:
        out_specs.append(pl.BlockSpec((1, 2, npp, LANES, RWKV_HEAD), lambda b, p: (b, 0, p, 0, 0)))
        out_shape.append(jax.ShapeDtypeStruct((n_seq, 2, RWKV_PAIRS, LANES, RWKV_HEAD), F32))
    return pl.pallas_call(
        functools.partial(_rwkv_kernel, has_init, emit_state, npp),
        grid=(n_seq, RWKV_PAIRS // npp),
        in_specs=in_specs,
        out_specs=out_specs,
        out_shape=out_shape,
        input_output_aliases=aliases,
        scratch_shapes=[pltpu.VMEM((t, w), F32)] * 3,
        compiler_params=_cparams(2),
        name="rwkv",
    )(*args)


def _attn_layer(x, mods3, g_row, cache_k, cache_v, slot, w_qkv, q_norm, k_norm, sink, w_o):
    qkv = _mm_mod(x, mods3, 3, g_row, w_qkv.astype(BF16), 512, "attn_qkv")
    y, k_ctx, v_ctx = _attn_ctx(qkv, sink, q_norm, k_norm, BATCH, SEQ)
    y = _attn_lat(qkv, cache_k, cache_v, slot, sink, q_norm, k_norm, y, DEC_BATCH, DEC_SEQ, N_CTX_TOK // DEC_SEQ)
    x = _mm_res(y, w_o.astype(BF16), x, mods3, 5, "attn_out")
    shape = (BATCH, SEQ, ATTN_KV_HEADS, ATTN_HEAD_DIM)
    return x, k_ctx.reshape(shape), v_ctx.reshape(shape)


def _mlstm_layer(x, mods3, g_row, state_lat, w_in, w_gate, b_gate, out_norm, w_o):
    h_ = MLSTM_HEADS
    proj = _mm_mod(x, mods3, 3, g_row, w_in.astype(BF16), 256, "mlstm_in")
    wg = jnp.transpose(w_gate.reshape(D_MODEL, 4, h_), (0, 2, 1))
    wg = jnp.pad(wg, ((0, 0), (0, 0), (0, LANES - 4))).reshape(D_MODEL, h_ * LANES).astype(BF16)
    bg = jnp.pad(b_gate.reshape(4, h_).T, ((0, 0), (0, LANES - 4)))
    gates = _mm_mod(x, mods3, 3, g_row, wg, 512, "mlstm_gates")
    y, C, n, m = _mlstm(proj, gates, bg, out_norm, None, _token_zeros(), BATCH, SEQ, 0, True, MLSTM_HEADS)
    (y,) = _mlstm(proj, gates, bg, out_norm, state_lat, y, DEC_BATCH, DEC_SEQ, N_CTX_TOK // DEC_SEQ, False, MLSTM_HEADS // 2)
    x = _mm_res(y, w_o.astype(BF16), x, mods3, 5, "mlstm_out")
    return x, C, n[:, :, :, 0, :], m[:, :, :, 0, 0]


def _rwkv_layer(x, mods3, g_row, s0_lat, mu, w_rkv, w0, wA, wB, a0, aA, aB, gA, gB, k_k, k_a, r_k, ln_g, ln_b, w_o):
    xs = _rwkv_mix(x, mods3, g_row, mu)
    rkv = _rkv(xs, w_rkv.astype(BF16))
    pad_r = lambda w, axis: jnp.pad(w, [(0, 0) if ax != axis else (0, LANES - w.shape[axis]) for ax in range(w.ndim)])
    lw = _lora(xs, 3, pad_r(wA, 2).astype(BF16), pad_r(wB, 1).astype(BF16), w0[:, None, :], "tanh", "log_decay", "rwkv_w")
    a = _lora(xs, 4, pad_r(aA, 2).astype(BF16), pad_r(aB, 1).astype(BF16), a0[:, None, :], "none", "sigmoid", "rwkv_a")
    g = _lora(xs, 5, gA[None].astype(BF16), gB[None].astype(BF16), jnp.zeros((1, 1, D_MODEL), F32), "sigmoid", "none", "rwkv_g")[0]
    s0 = jnp.pad(s0_lat.reshape(DEC_BATCH, 2, RWKV_PAIRS, LANES, RWKV_HEAD), ((0, 0),) * 4 + ((0, LANES - RWKV_HEAD),))
    y, S = _rwkv(rkv, lw, a, g, k_k, k_a, r_k, ln_g, ln_b, None, _token_zeros(), BATCH, SEQ, 0, True, RWKV_PAIRS // 2)
    (y,) = _rwkv(rkv, lw, a, g, k_k, k_a, r_k, ln_g, ln_b, s0, y, DEC_BATCH, DEC_SEQ, N_CTX_TOK // DEC_SEQ, False, RWKV_PAIRS // 4)
    x = _mm_res(y, w_o.astype(BF16), x, mods3, 5, "rwkv_out")
    return x, S.reshape(BATCH, 2, RWKV_HEADS, RWKV_HEAD, RWKV_HEAD)


def kernel(x_prompt, x_sample, cache_k, cache_v, state_mlstm_C, state_mlstm_n, state_mlstm_m, state_rwkv, c, c_ctx, mod_w, mod_b, norm_g, ffn_w_in, ffn_w_out, attn_w_qkv, attn_q_norm, attn_k_norm, attn_sink, attn_w_o, mlstm_w_in, mlstm_w_gate, mlstm_b_gate, mlstm_out_norm, mlstm_w_o, rwkv_mu, rwkv_w_rkv, rwkv_w0, rwkv_wA, rwkv_wB, rwkv_a0, rwkv_aA, rwkv_aB, rwkv_gA, rwkv_gB, rwkv_k_k, rwkv_k_a, rwkv_r_k, rwkv_ln_g, rwkv_ln_b, rwkv_w_o):
    x = jnp.concatenate([x_prompt.reshape(N_CTX_TOK, D_MODEL), x_sample.reshape(N_LAT_TOK, D_MODEL)], axis=0)
    cond8 = jnp.concatenate([c_ctx[None, :], c, jnp.zeros((MOD_ROWS - 1 - DEC_BATCH, D_MODEL), F32)], axis=0)
    mods = _adaln_all(cond8, mod_w, mod_b)
    ffn_order = [(i, j) for i in range(DEPTH) for j in range(2)]
    w_in_b = ffn_w_in[0, 0].astype(BF16)
    w_out_b = ffn_w_out[0, 0].astype(BF16)

    def ffn(x, mods3, slot, g_row, k, w_in_b, w_out_b):
        nxt = (ffn_w_in, ffn_w_out) + ffn_order[k + 1] if k + 1 < len(ffn_order) else None
        res = _ffn(x, mods3, slot, g_row, w_in_b, w_out_b, nxt)
        return res if nxt is not None else (res[0], None, None)

    new_k, new_v, new_C, new_n, new_m, new_S = [], [], [], [], [], []
    for i in range(DEPTH):
        kind, slot = i % N_MIXERS, i // N_MIXERS
        mods3 = mods[i].reshape(MOD_ROWS, 1, N_MOD * D_MODEL)
        x, w_in_b, w_out_b = ffn(x, mods3, 0, norm_g[i, 0][None, :], 2 * i, w_in_b, w_out_b)
        g_row = norm_g[i, 1][None, :]
        if kind == 0:
            x, k_ctx, v_ctx = _attn_layer(x, mods3, g_row, cache_k, cache_v, slot, attn_w_qkv[slot], attn_q_norm[slot],
                                          attn_k_norm[slot], attn_sink[slot], attn_w_o[slot])
            new_k.append(k_ctx)
            new_v.append(v_ctx)
        elif kind == 1:
            state_lat = (state_mlstm_C[:, slot], state_mlstm_n[:, slot], state_mlstm_m[:, slot])
            x, C, n, m = _mlstm_layer(x, mods3, g_row, state_lat, mlstm_w_in[slot], mlstm_w_gate[slot], mlstm_b_gate[slot],
                                      mlstm_out_norm[slot], mlstm_w_o[slot])
            new_C.append(C)
            new_n.append(n)
            new_m.append(m)
        else:
            x, S = _rwkv_layer(x, mods3, g_row, state_rwkv[:, slot], rwkv_mu[slot], rwkv_w_rkv[slot], rwkv_w0[slot],
                               rwkv_wA[slot], rwkv_wB[slot], rwkv_a0[slot], rwkv_aA[slot], rwkv_aB[slot], rwkv_gA[slot],
                               rwkv_gB[slot], rwkv_k_k[slot], rwkv_k_a[slot], rwkv_r_k[slot], rwkv_ln_g[slot],
                               rwkv_ln_b[slot], rwkv_w_o[slot])
            new_S.append(S)
        x, w_in_b, w_out_b = ffn(x, mods3, 6, norm_g[i, 2][None, :], 2 * i + 1, w_in_b, w_out_b)

    y_prompt = x[:N_CTX_TOK].reshape(BATCH, SEQ, D_MODEL)
    y_sample = x[N_CTX_TOK:].reshape(DEC_BATCH, DEC_SEQ, D_MODEL)
    return (y_prompt, y_sample, jnp.stack(new_k, axis=1), jnp.stack(new_v, axis=1),
            jnp.stack(new_C, axis=1), jnp.stack(new_n, axis=1), jnp.stack(new_m, axis=1), jnp.stack(new_S, axis=1))
```
